```python
import jax
import jax.numpy as jnp
from jax import lax
import numpy as np

D_MODEL = 1024
BATCH = 2
SEQ = 8192
DEPTH = 1
DEC_BATCH = 128
DEC_SEQ = 1
PAST_LEN = 8192
PAGE_SIZE = 128

HEAD_DIM = 64
HEADS_PER_GROUP = 4
DILATED_GROUPS = ((128, 1), (512, 4), (2048, 16))
N_A_HEADS = HEADS_PER_GROUP * len(DILATED_GROUPS)
A_WIDTH = N_A_HEADS * HEAD_DIM
A_OUT = HEADS_PER_GROUP * HEAD_DIM
ROT_DIM = HEAD_DIM // 4
ROPE_THETA = 500000.0
Q_BLOCK = 128

GLA_HEADS = 4
GLA_DK = D_MODEL // 2 // GLA_HEADS
GLA_DV = D_MODEL // GLA_HEADS
GLA_K = GLA_HEADS * GLA_DK
GLA_V = GLA_HEADS * GLA_DV
GATE_RANK = 16
GATE_NORMALIZER = 16.0
GLA_CHUNK = 64

D_FF = -(-8 * D_MODEL // (3 * 256)) * 256
PLE_DIM = 256
DN_ALPHA = (2 * DEPTH) ** 0.25
DN_BETA = (8 * DEPTH) ** -0.25
LN_EPS = 1e-5
RMS_EPS = 1e-6
IN_COLS = 3 * A_WIDTH + 2 * GLA_K + 2 * GLA_V + GATE_RANK + 2 * D_MODEL

kernel_name = 'dilated_swa_gla_hybrid_step'


def _split_points():
    sizes = (A_WIDTH, A_WIDTH, A_WIDTH, GLA_K, GLA_K, GLA_V, GLA_V, GATE_RANK, D_MODEL, D_MODEL)
    return [int(c) for c in np.cumsum(sizes)[:-1]]


def _layernorm(x, g, b):
    xf = x.astype(jnp.float32)
    xc = xf - jnp.mean(xf, -1, keepdims=True)
    var = jnp.mean(xc * xc, -1, keepdims=True)
    y = xc * lax.rsqrt(var + LN_EPS) * g.astype(jnp.float32) + b.astype(jnp.float32)
    return y.astype(x.dtype)


def _rope_partial(x, pos):
    half = ROT_DIM // 2
    inv_freq = ROPE_THETA ** (-jnp.arange(0, ROT_DIM, 2, dtype=jnp.float32) / ROT_DIM)
    ang = pos.astype(jnp.float32)[:, None] * inv_freq[None, :]
    cos = jnp.cos(ang)[:, None, :]
    sin = jnp.sin(ang)[:, None, :]
    xr = x[..., :ROT_DIM].astype(jnp.float32)
    x1, x2 = xr[..., :half], xr[..., half:]
    rot = jnp.concatenate([x1 * cos - x2 * sin, x2 * cos + x1 * sin], -1)
    return jnp.concatenate([rot.astype(x.dtype), x[..., ROT_DIM:]], -1)


def _dilated_prompt(q, k, v, window, dil):
    b, s, h, e = q.shape
    nj = window // dil
    l = s // dil
    nb = -(-l // Q_BLOCK)
    lp = nb * Q_BLOCK

    def to_sub(t):
        t = t.reshape(b, l, dil, h, e).transpose(0, 2, 3, 1, 4)
        return jnp.pad(t, ((0, 0), (0, 0), (0, 0), (0, lp - l), (0, 0)))

    def band(t):
        t = jnp.pad(t, ((0, 0), (0, 0), (0, 0), (Q_BLOCK, 0), (0, 0)))
        t = t.reshape(b, dil, h, nb + 1, Q_BLOCK, e)
        return jnp.concatenate([t[:, :, :, :-1], t[:, :, :, 1:]], axis=4)

    qb = to_sub(q).reshape(b, dil, h, nb, Q_BLOCK, e).astype(jnp.float32)
    kb = band(to_sub(k)).astype(jnp.float32)
    vb = band(to_sub(v)).astype(jnp.float32)
    qi = np.arange(Q_BLOCK)[:, None]
    kc = np.arange(2 * Q_BLOCK)[None, :]
    back = qi + Q_BLOCK - kc
    key_idx = np.arange(nb)[:, None, None] * Q_BLOCK + kc[None] - Q_BLOCK
    mask = (back >= 0) & (back <= nj) & (key_idx >= 0)
    scores = jnp.einsum('bghnqe,bghnke->bghnqk', qb, kb) * (e ** -0.5)
    scores = jnp.where(mask, scores, -jnp.inf)
    lse = jax.nn.logsumexp(scores, axis=-1)
    probs = jnp.exp(scores - lse[..., None])
    o = jnp.einsum('bghnqk,bghnke->bghnqe', probs, vb)
    o = o.reshape(b, dil, h, lp, e)[:, :, :, :l].transpose(0, 3, 1, 2, 4).reshape(b, s, h, e)
    lse = lse.reshape(b, dil, h, lp)[..., :l].transpose(0, 3, 1, 2).reshape(b, s, h)
    return o, lse


def _dilated_sample(q, k_new, v_new, buf, window, dil):
    n, t, h, e = q.shape
    lb = buf.shape[1]
    nj = window // dil
    kf = jnp.concatenate([buf[:, :, 0], k_new.astype(buf.dtype)], axis=1)
    vf = jnp.concatenate([buf[:, :, 1], v_new.astype(buf.dtype)], axis=1)
    idx = lb + np.arange(t)[:, None] - dil * np.arange(nj + 1)[None, :]
    valid = idx >= 0
    idx = np.maximum(idx, 0)
    kg = kf[:, idx].astype(jnp.float32)
    vg = vf[:, idx].astype(jnp.float32)
    scores = jnp.einsum('nthe,ntjhe->nthj', q.astype(jnp.float32), kg) * (e ** -0.5)
    scores = jnp.where(valid[None, :, None, :], scores, -jnp.inf)
    lse = jax.nn.logsumexp(scores, axis=-1)
    probs = jnp.exp(scores - lse[..., None])
    o = jnp.einsum('nthj,ntjhe->nthe', probs, vg)
    keep = min(window, lb + t)
    new_buf = jnp.stack([kf[:, lb + t - keep:], vf[:, lb + t - keep:]], axis=2)
    return o, lse, new_buf


def _combine_groups(outs, lses):
    o = jnp.stack(outs, 0)
    w = jax.nn.softmax(jnp.stack(lses, 0), axis=0)
    o = jnp.sum(w[..., None] * o, axis=0)
    return o.reshape(o.shape[0], o.shape[1], A_OUT)


def _gla_prompt(q, k, v, loga):
    b, s, h, _ = q.shape
    nc = s // GLA_CHUNK
    causal = np.tril(np.ones((GLA_CHUNK, GLA_CHUNK), dtype=bool))[:, :, None]

    def chunks(t):
        return t.reshape(b, nc, GLA_CHUNK, h, t.shape[-1]).transpose(1, 0, 3, 2, 4)

    def step(st, inp):
        qc, kc, vc, gc = inp
        cum = jnp.cumsum(gc, axis=2)
        o_inter = jnp.einsum('bhcd,bhdv->bhcv', qc * jnp.exp(cum), st)
        diff = cum[:, :, :, None, :] - cum[:, :, None, :, :]
        decay = jnp.exp(jnp.where(causal, diff, -jnp.inf))
        att = jnp.einsum('bhid,bhjd,bhijd->bhij', qc, kc, decay)
        o_intra = jnp.einsum('bhij,bhjv->bhiv', att, vc)
        last = cum[:, :, -1:, :]
        k_dec = kc * jnp.exp(last - cum)
        st = jnp.exp(last[:, :, 0, :])[..., None] * st + jnp.einsum('bhjd,bhjv->bhdv', k_dec, vc)
        return st, o_inter + o_intra

    st0 = jnp.zeros((b, h, GLA_DK, GLA_DV), jnp.float32)
    st, o = lax.scan(step, st0, (chunks(q), chunks(k), chunks(v), chunks(loga)))
    o = o.transpose(1, 0, 3, 2, 4).reshape(b, s, h, GLA_DV)
    return o, st


def _gla_sample(q, k, v, loga, st0):
    def step(st, inp):
        qt, kt, vt, gt = inp
        st = jnp.exp(gt)[..., None] * st + kt[..., :, None] * vt[..., None, :]
        return st, jnp.einsum('nhd,nhdv->nhv', qt, st)

    seq = tuple(t.transpose(1, 0, 2, 3) for t in (q, k, v, loga))
    st, o = lax.scan(step, st0.astype(jnp.float32), seq)
    return o.transpose(1, 0, 2, 3), st


def _head_rmsnorm(o, g):
    o = o * lax.rsqrt(jnp.mean(o * o, -1, keepdims=True) + RMS_EPS)
    o = o * g.astype(jnp.float32).reshape(GLA_HEADS, GLA_DV)
    return o.reshape(o.shape[0], o.shape[1], GLA_V)


def _layer(x, pe, pos, bufs, gla_state, w_in, w_gate_up, b_gate, gla_norm_g, w_a_out, w_b_out, w_o,
           ln1_g, ln1_b, w_ff_gate, w_ff_up, w_ff_down, ln2_g, ln2_b, w_ple_gate, w_ple_proj):
    nb, t, _ = x.shape
    qa, ka, va, qg, kg, vg, rg, glr, ga, gb = jnp.split(x @ w_in, _split_points(), axis=-1)
    qa = _rope_partial(qa.reshape(nb, t, N_A_HEADS, HEAD_DIM), pos)
    ka = _rope_partial(ka.reshape(nb, t, N_A_HEADS, HEAD_DIM), pos)
    va = va.reshape(nb, t, N_A_HEADS, HEAD_DIM)
    outs, lses, new_bufs = [], [], []
    for g, (window, dil) in enumerate(DILATED_GROUPS):
        sl = slice(g * HEADS_PER_GROUP, (g + 1) * HEADS_PER_GROUP)
        qh, kh, vh = qa[:, :, sl], ka[:, :, sl], va[:, :, sl]
        if bufs is None:
            o, lse = _dilated_prompt(qh, kh, vh, window, dil)
            keep = min(window, t)
            nbuf = jnp.stack([kh[:, t - keep:], vh[:, t - keep:]], axis=2)
        else:
            o, lse, nbuf = _dilated_sample(qh, kh, vh, bufs[g], window, dil)
        outs.append(o)
        lses.append(lse)
        new_bufs.append(nbuf)
    o_a = _combine_groups(outs, lses).astype(x.dtype)
    loga = jax.nn.log_sigmoid((glr @ w_gate_up + b_gate).astype(jnp.float32)) / GATE_NORMALIZER
    loga = loga.reshape(nb, t, GLA_HEADS, GLA_DK)
    qg = qg.astype(jnp.float32).reshape(nb, t, GLA_HEADS, GLA_DK) * (GLA_DK ** -0.5)
    kg = kg.astype(jnp.float32).reshape(nb, t, GLA_HEADS, GLA_DK)
    vg = vg.astype(jnp.float32).reshape(nb, t, GLA_HEADS, GLA_DV)
    if gla_state is None:
        o_b, st = _gla_prompt(qg, kg, vg, loga)
        st = st.astype(x.dtype)
    else:
        o_b, st = _gla_sample(qg, kg, vg, loga, gla_state)
        st = st.astype(gla_state.dtype)
    o_b = (_head_rmsnorm(o_b, gla_norm_g) * jax.nn.silu(rg.astype(jnp.float32))).astype(x.dtype)
    merged = jax.nn.sigmoid(ga) * (o_a @ w_a_out) + jax.nn.sigmoid(gb) * (o_b @ w_b_out)
    x = _layernorm(DN_ALPHA * x + merged @ w_o, ln1_g, ln1_b)
    ff = (jax.nn.silu(x @ w_ff_gate) * (x @ w_ff_up)) @ w_ff_down
    x = _layernorm(DN_ALPHA * x + ff, ln2_g, ln2_b)
    x = x + jax.nn.sigmoid(x @ w_ple_gate) * (pe.astype(x.dtype) @ w_ple_proj)
    return x, new_bufs, st


def setup_inputs(seed: int = 0) -> dict:
    key = jax.random.key(seed)
    ks = jax.random.split(key, 24)

    def nrm(k, shape, scale):
        return scale * jax.random.normal(k, shape, jnp.float32)

    lb = [min(w, PAST_LEN) for w, _ in DILATED_GROUPS]
    return {
        'x_prompt': nrm(ks[0], (BATCH, SEQ, D_MODEL), 1.0),
        'x_sample': nrm(ks[1], (DEC_BATCH, DEC_SEQ, D_MODEL), 1.0),
        'cache_a1_kv': nrm(ks[2], (DEPTH, DEC_BATCH, lb[0], 2, HEADS_PER_GROUP, HEAD_DIM), 1.0),
        'cache_a2_kv': nrm(ks[3], (DEPTH, DEC_BATCH, lb[1], 2, HEADS_PER_GROUP, HEAD_DIM), 1.0),
        'cache_a3_kv': nrm(ks[4], (DEPTH, DEC_BATCH, lb[2], 2, HEADS_PER_GROUP, HEAD_DIM), 1.0),
        'state_gla': nrm(ks[5], (DEPTH, DEC_BATCH, GLA_HEADS, GLA_DK, GLA_DV), 1.0),
        'p_prompt': nrm(ks[6], (DEPTH, BATCH, SEQ, PLE_DIM), 1.0),
        'p_sample': nrm(ks[7], (DEPTH, DEC_BATCH, DEC_SEQ, PLE_DIM), 1.0),
        'w_in': nrm(ks[8], (DEPTH, D_MODEL, IN_COLS), D_MODEL ** -0.5),
        'w_gate_up': nrm(ks[9], (DEPTH, GATE_RANK, GLA_K), GATE_RANK ** -0.5),
        'b_gate': nrm(ks[10], (DEPTH, GLA_K), 0.1),
        'gla_norm_g': 1.0 + nrm(ks[11], (DEPTH, GLA_V), 0.02),
        'w_a_out': nrm(ks[12], (DEPTH, A_OUT, D_MODEL), A_OUT ** -0.5),
        'w_b_out': nrm(ks[13], (DEPTH, GLA_V, D_MODEL), GLA_V ** -0.5),
        'w_o': nrm(ks[14], (DEPTH, D_MODEL, D_MODEL), DN_BETA * D_MODEL ** -0.5),
        'ln1_g': 1.0 + nrm(ks[15], (DEPTH, D_MODEL), 0.02),
        'ln1_b': nrm(ks[16], (DEPTH, D_MODEL), 0.02),
        'w_ff_gate': nrm(ks[17], (DEPTH, D_MODEL, D_FF), D_MODEL ** -0.5),
        'w_ff_up': nrm(ks[18], (DEPTH, D_MODEL, D_FF), D_MODEL ** -0.5),
        'w_ff_down': nrm(ks[19], (DEPTH, D_FF, D_MODEL), DN_BETA * D_FF ** -0.5),
        'ln2_g': 1.0 + nrm(ks[20], (DEPTH, D_MODEL), 0.02),
        'ln2_b': nrm(ks[21], (DEPTH, D_MODEL), 0.02),
        'w_ple_gate': nrm(ks[22], (DEPTH, D_MODEL, D_MODEL), D_MODEL ** -0.5),
        'w_ple_proj': nrm(ks[23], (DEPTH, PLE_DIM, D_MODEL), PLE_DIM ** -0.5),
    }


def reference(x_prompt, x_sample, cache_a1_kv, cache_a2_kv, cache_a3_kv, state_gla, p_prompt, p_sample,
              w_in, w_gate_up, b_gate, gla_norm_g, w_a_out, w_b_out, w_o, ln1_g, ln1_b,
              w_ff_gate, w_ff_up, w_ff_down, ln2_g, ln2_b, w_ple_gate, w_ple_proj):
    pos_p = jnp.arange(x_prompt.shape[1])
    pos_s = PAST_LEN + jnp.arange(x_sample.shape[1])
    x_p, x_s = x_prompt, x_sample
    kv_p = [[], [], []]
    kv_s = [[], [], []]
    st_p, st_s = [], []
    for i in range(DEPTH):
        lw = (w_in[i], w_gate_up[i], b_gate[i], gla_norm_g[i], w_a_out[i], w_b_out[i], w_o[i],
              ln1_g[i], ln1_b[i], w_ff_gate[i], w_ff_up[i], w_ff_down[i], ln2_g[i], ln2_b[i],
              w_ple_gate[i], w_ple_proj[i])
        x_p, bufs_p, sp = _layer(x_p, p_prompt[i], pos_p, None, None, *lw)
        x_s, bufs_s, ss = _layer(x_s, p_sample[i], pos_s,
                                 (cache_a1_kv[i], cache_a2_kv[i], cache_a3_kv[i]), state_gla[i], *lw)
        for g in range(len(DILATED_GROUPS)):
            kv_p[g].append(bufs_p[g])
            kv_s[g].append(bufs_s[g])
        st_p.append(sp)
        st_s.append(ss)
    kv_a1_prompt = jnp.stack(kv_p[0])
    kv_a2_prompt = jnp.stack(kv_p[1])
    kv_a3_prompt = jnp.stack(kv_p[2])
    state_gla_prompt = jnp.stack(st_p)
    kv_a1_sample = jnp.stack(kv_s[0])
    kv_a2_sample = jnp.stack(kv_s[1])
    kv_a3_sample = jnp.stack(kv_s[2])
    state_gla_sample = jnp.stack(st_s)
    return (x_p, x_s, kv_a1_prompt, kv_a2_prompt, kv_a3_prompt, state_gla_prompt,
            kv_a1_sample, kv_a2_sample, kv_a3_sample, state_gla_sample)
```

```python
import functools

import numpy as np
import jax
import jax.numpy as jnp
from jax import lax
from jax.experimental import pallas as pl
from jax.experimental.pallas import tpu as pltpu

F32 = jnp.float32
BF16 = jnp.bfloat16

D_MODEL = 1024
HEAD_DIM = 64
HEADS_PER_GROUP = 4
DILATED_GROUPS = ((128, 1), (512, 4), (2048, 16))
GROUP_COLS = HEADS_PER_GROUP * HEAD_DIM
A_WIDTH = len(DILATED_GROUPS) * GROUP_COLS
ROT_DIM = HEAD_DIM // 4
ROPE_THETA = 500000.0
KEYS_PER_WINDOW = 128
GLA_HEADS = 4
GLA_DK = 128
GLA_DV = 256
GLA_K = GLA_HEADS * GLA_DK
GLA_V = GLA_HEADS * GLA_DV
GATE_RANK = 16
GATE_NORMALIZER = 16.0
GLA_CHUNK = 64
D_FF = 2816
PLE_DIM = 256
DN_ALPHA = 2.0 ** 0.25
LN_EPS = 1e-5
RMS_EPS = 1e-6
PAST_LEN = 8192

LANES = 128
SUBLANES = 8
VMEM_LIMIT = 56 * 1024 * 1024

GLA_FACTORED_RANGE = 40.0


def _dot(a, b):
    return jnp.dot(a, b, preferred_element_type=F32)


def _dot_nt(a, b):
    return lax.dot_general(a, b, (((1,), (1,)), ((), ())), preferred_element_type=F32)


def _dot_tn(a, b):
    return lax.dot_general(a, b, (((0,), (0,)), ((), ())), preferred_element_type=F32)


def _const_spec(shape):
    return pl.BlockSpec(shape, lambda *_: (0,) * len(shape))


def _layernorm(x, g, b):
    xc = x - jnp.mean(x, -1, keepdims=True)
    var = jnp.mean(xc * xc, -1, keepdims=True)
    return xc * lax.rsqrt(var + LN_EPS) * g + b


def _sigmoid(x):
    return 1.0 / (1.0 + jnp.exp(-x))


def _silu(x):
    return x * _sigmoid(x)


def _split3_bf16(x):
    hi = x.astype(BF16)
    r1 = x - hi.astype(F32)
    mid = r1.astype(BF16)
    lo = (r1 - mid.astype(F32)).astype(BF16)
    return hi, mid, lo


def _in_proj_kernel(x_ref, rope_ref, wa_ref, wg_ref, wlr_ref, wgates_ref, wup_ref, bg_ref,
                    q_ref, k_ref, v_ref, qg_ref, kg_ref, vg_ref, rg_ref, la_ref, gates_ref,
                    *, sample):
    tm = x_ref.shape[0]
    xb = x_ref[...].astype(BF16)
    cos = rope_ref[:, 0:LANES]
    sin_lo = rope_ref[:, LANES:2 * LANES]
    sin_hi = rope_ref[:, 2 * LANES:3 * LANES]

    def rope(t):
        return (t * cos + pltpu.roll(t, LANES - ROT_DIM // 2, 1) * sin_lo
                + pltpu.roll(t, ROT_DIM // 2, 1) * sin_hi)

    def put(ref, col, val):
        if sample:
            ref[col * LANES:(col + 1) * LANES, :] = val.T
        else:
            ref[:, col * LANES:(col + 1) * LANES] = val

    tiles = GROUP_COLS // LANES
    for c in range(A_WIDTH // GROUP_COLS):
        lo, hi = c * GROUP_COLS, (c + 1) * GROUP_COLS
        qc = _dot(xb, wa_ref[:, lo:hi])
        kc = _dot(xb, wa_ref[:, A_WIDTH + lo:A_WIDTH + hi])
        vc = _dot(xb, wa_ref[:, 2 * A_WIDTH + lo:2 * A_WIDTH + hi])
        for s in range(tiles):
            sl = slice(s * LANES, (s + 1) * LANES)
            put(q_ref, c * tiles + s, rope(qc[:, sl]) * (HEAD_DIM ** -0.5))
            put(k_ref, c * tiles + s, rope(kc[:, sl]))
            put(v_ref, c * tiles + s, vc[:, sl])

    glr = _dot(xb, wlr_ref[...])
    z = _dot(glr.astype(BF16), wup_ref[...]) + bg_ref[...]
    loga = -(jnp.maximum(-z, 0.0) + jnp.log1p(jnp.exp(-jnp.abs(z)))) * (1.0 / GATE_NORMALIZER)
    if not sample:
        row = lax.broadcasted_iota(jnp.int32, (tm, tm), 0)
        col = lax.broadcasted_iota(jnp.int32, (tm, tm), 1)
        tri = ((row // GLA_CHUNK == col // GLA_CHUNK) & (col <= row)).astype(BF16)
        hi, mid, lo = _split3_bf16(loga)
        loga = _dot(tri, hi) + _dot(tri, mid) + _dot(tri, lo)
    qg = _dot(xb, wg_ref[:, 0:GLA_K]) * (GLA_DK ** -0.5)
    kg = _dot(xb, wg_ref[:, GLA_K:2 * GLA_K])
    for s in range(GLA_K // LANES):
        sl = slice(s * LANES, (s + 1) * LANES)
        put(qg_ref, s, qg[:, sl])
        put(kg_ref, s, kg[:, sl])
        put(la_ref, s, loga[:, sl])
    vg_ref[...] = _dot(xb, wg_ref[:, 2 * GLA_K:2 * GLA_K + GLA_V])
    rg_ref[...] = _dot(xb, wg_ref[:, 2 * GLA_K + GLA_V:2 * GLA_K + 2 * GLA_V]).astype(BF16)

    gates_ref[...] = _sigmoid(_dot(xb, wgates_ref[...])).astype(BF16)


def _in_proj(x, rope_tab, weights, *, tm, rope_period_tiles, sample):
    rows = x.shape[0]
    assert not sample or tm == rows == LANES
    wa, wg, wlr, wgates, wup, bg = weights

    def out(width, dtype, transposed=False):
        if transposed and sample:
            return jax.ShapeDtypeStruct((width, rows), dtype), _const_spec((width, rows))
        return jax.ShapeDtypeStruct((rows, width), dtype), pl.BlockSpec((tm, width), lambda i: (i, 0))

    outs = (
        out(A_WIDTH, F32, True),
        out(A_WIDTH, F32, True),
        out(A_WIDTH, F32, True),
        out(GLA_K, F32, True),
        out(GLA_K, F32, True),
        out(GLA_V, F32),
        out(GLA_V, BF16),
        out(GLA_K, F32, True),
        out(2 * D_MODEL, BF16),
    )
    return pl.pallas_call(
        functools.partial(_in_proj_kernel, sample=sample),
        grid=(rows // tm,),
        in_specs=[
            pl.BlockSpec((tm, D_MODEL), lambda i: (i, 0)),
            pl.BlockSpec((tm, 3 * LANES), lambda i: (i % rope_period_tiles, 0)),
            _const_spec(wa.shape), _const_spec(wg.shape), _const_spec(wlr.shape),
            _const_spec(wgates.shape), _const_spec(wup.shape), _const_spec(bg.shape),
        ],
        out_specs=tuple(o[1] for o in outs),
        out_shape=tuple(o[0] for o in outs),
        compiler_params=pltpu.CompilerParams(
            dimension_semantics=("parallel",), vmem_limit_bytes=VMEM_LIMIT),
        name="in_proj_sample" if sample else "in_proj_prompt",
    )(x, rope_tab, wa, wg, wlr, wgates, wup, bg)


def _rope_table(pos):
    half = ROT_DIM // 2
    inv_freq = ROPE_THETA ** (-jnp.arange(0, ROT_DIM, 2, dtype=F32) / ROT_DIM)
    ang = pos.astype(F32)[:, None] * inv_freq[None, :]
    cos, sin = jnp.cos(ang), jnp.sin(ang)
    lane = np.arange(LANES) % HEAD_DIM
    fidx = lane % half
    first = lane < half
    second = (lane >= half) & (lane < ROT_DIM)
    cos_t = jnp.where((first | second)[None, :], cos[:, fidx], 1.0)
    sin_lo = jnp.where(first[None, :], -sin[:, fidx], 0.0)
    sin_hi = jnp.where(second[None, :], sin[:, fidx], 0.0)
    return jnp.concatenate([cos_t, sin_lo, sin_hi], axis=1)


ATT_TILE = 2048
Q_BLOCK = 128
HEADS_PER_TILE = LANES // HEAD_DIM


def _attn_prompt_kernel(q_ref, kc_ref, kp_ref, vc_ref, vp_ref, o_ref, kk, vv, og, lg):
    t = pl.program_id(1)
    g = pl.program_id(3)
    qi = lax.broadcasted_iota(jnp.int32, (Q_BLOCK, 2 * Q_BLOCK), 0)
    kc = lax.broadcasted_iota(jnp.int32, (Q_BLOCK, 2 * Q_BLOCK), 1)
    back = qi + Q_BLOCK - kc
    band = (back >= 0) & (back <= KEYS_PER_WINDOW)
    in_prev = kc < Q_BLOCK

    for gi, (window, dil) in enumerate(DILATED_GROUPS):

        @pl.when(g == gi)
        def _(gi=gi, window=window, dil=dil):
            kk[0:window, :] = kp_ref[ATT_TILE - window:ATT_TILE, :]
            kk[window:window + ATT_TILE, :] = kc_ref[...]
            vv[0:window, :] = vp_ref[ATT_TILE - window:ATT_TILE, :]
            vv[window:window + ATT_TILE, :] = vc_ref[...]
            nblk = ATT_TILE // dil // Q_BLOCK

            def rows(start, size):
                if dil == 1:
                    return pl.ds(pl.multiple_of(start, Q_BLOCK), size)
                return pl.ds(start, size, stride=dil)

            def body(idx, carry):
                r = idx // nblk
                n = idx % nblk
                base = r + dil * Q_BLOCK * n
                q2 = q_ref[rows(base, Q_BLOCK), :]
                k2 = kk[rows(base, 2 * Q_BLOCK), :]
                v2 = vv[rows(base, 2 * Q_BLOCK), :]
                mask = band & (jnp.logical_not(in_prev) | (t > 0) | (n > 0))
                outs, lses = [], []
                for h in range(HEADS_PER_TILE):
                    hs = slice(h * HEAD_DIM, (h + 1) * HEAD_DIM)
                    s = _dot_nt(q2[:, hs].astype(BF16), k2[:, hs].astype(BF16))
                    s = jnp.where(mask, s, -jnp.inf)
                    m = jnp.max(s, axis=-1, keepdims=True)
                    p = jnp.exp(s - m)
                    l = jnp.sum(p, axis=-1, keepdims=True)
                    outs.append(_dot(p.astype(BF16), v2[:, hs].astype(BF16)) / l)
                    lses.append(jnp.broadcast_to(m + jnp.log(l), (Q_BLOCK, HEAD_DIM)))
                og[gi, rows(base, Q_BLOCK), :] = jnp.concatenate(outs, axis=1)
                lg[gi, rows(base, Q_BLOCK), :] = jnp.concatenate(lses, axis=1)
                return carry

            lax.fori_loop(0, dil * nblk, body, 0)

    @pl.when(g == len(DILATED_GROUPS) - 1)
    def _():
        rows = 256

        def combine(c, carry):
            r0 = pl.multiple_of(c * rows, rows)
            l0 = lg[0, pl.ds(r0, rows), :]
            l1 = lg[1, pl.ds(r0, rows), :]
            l2 = lg[2, pl.ds(r0, rows), :]
            mx = jnp.maximum(jnp.maximum(l0, l1), l2)
            e0, e1, e2 = jnp.exp(l0 - mx), jnp.exp(l1 - mx), jnp.exp(l2 - mx)
            acc = (e0 * og[0, pl.ds(r0, rows), :] + e1 * og[1, pl.ds(r0, rows), :]
                   + e2 * og[2, pl.ds(r0, rows), :])
            o_ref[pl.ds(r0, rows), :] = (acc / (e0 + e1 + e2)).astype(o_ref.dtype)
            return carry

        lax.fori_loop(0, ATT_TILE // rows, combine, 0)


def _attn_prompt(q, k, v):
    b, s, _ = q.shape
    tiles_per_group = GROUP_COLS // LANES
    cur = pl.BlockSpec((None, ATT_TILE, LANES), lambda bi, t, c, g: (bi, t, g * tiles_per_group + c))
    prev = pl.BlockSpec((None, ATT_TILE, LANES),
                        lambda bi, t, c, g: (bi, jnp.maximum(t - 1, 0), g * tiles_per_group + c))
    max_window = max(w for w, _ in DILATED_GROUPS)
    ngroups = len(DILATED_GROUPS)
    return pl.pallas_call(
        _attn_prompt_kernel,
        grid=(b, s // ATT_TILE, tiles_per_group, ngroups),
        in_specs=[cur, cur, prev, cur, prev],
        out_specs=pl.BlockSpec((None, ATT_TILE, LANES), lambda bi, t, c, g: (bi, t, c)),
        out_shape=jax.ShapeDtypeStruct((b, s, GROUP_COLS), BF16),
        scratch_shapes=[
            pltpu.VMEM((max_window + ATT_TILE, LANES), F32),
            pltpu.VMEM((max_window + ATT_TILE, LANES), F32),
            pltpu.VMEM((ngroups, ATT_TILE, LANES), F32),
            pltpu.VMEM((ngroups, ATT_TILE, LANES), F32),
        ],
        compiler_params=pltpu.CompilerParams(
            dimension_semantics=("parallel", "parallel", "parallel", "arbitrary"),
            vmem_limit_bytes=VMEM_LIMIT),
        name="attn_prompt",
    )(q, k, k, v, v)


GLA_TILE = 512


def _gla_epilogue(o, r, g):
    o = o * lax.rsqrt(jnp.mean(o * o, -1, keepdims=True) + RMS_EPS)
    return o * g * _silu(r.astype(F32))


def _gla_prompt_kernel(q_ref, k_ref, v_ref, cum_ref, r_ref, g_ref, o_ref, st_ref, st_t, a_scr):
    step = pl.program_id(2)
    c = GLA_CHUNK

    @pl.when(step == 0)
    def _():
        st_t[...] = jnp.zeros_like(st_t)

    ri = lax.broadcasted_iota(jnp.int32, (c, c), 0)
    ci = lax.broadcasted_iota(jnp.int32, (c, c), 1)
    row_id = lax.broadcasted_iota(jnp.int32, (c, GLA_DK), 0)

    def chunk(ch, carry):
        r0 = pl.multiple_of(ch * c, c)
        q = q_ref[pl.ds(r0, c), :]
        k = k_ref[pl.ds(r0, c), :]
        v = v_ref[pl.ds(r0, c), :].astype(BF16)
        cum = cum_ref[pl.ds(r0, c), :]
        last = cum_ref[pl.ds(r0 + c - 1, 1), :]
        qe = q * jnp.exp(cum)
        s_t = st_t[...]
        o = _dot_nt(qe.astype(BF16), s_t.astype(BF16))

        total = jnp.max(-last)

        @pl.when(total < GLA_FACTORED_RANGE)
        def _():
            ke = k * jnp.exp(-cum)
            a = _dot_nt(qe.astype(BF16), ke.astype(BF16))
            a_scr[...] = jnp.where(ci <= ri, a, 0.0)

        @pl.when(total >= GLA_FACTORED_RANGE)
        def _():
            def col(j, a):
                cj = cum_ref[pl.ds(r0 + j, 1), :]
                kj = k_ref[pl.ds(r0 + j, 1), :]
                e = jnp.exp(jnp.where(row_id >= j, cum - cj, -jnp.inf))
                sj = jnp.sum(q * kj * e, axis=-1, keepdims=True)
                return jnp.where(ci == j, sj, a)

            a_scr[...] = lax.fori_loop(0, c, col, jnp.zeros((c, c), F32))

        o = o + _dot(a_scr[...].astype(BF16), v)
        o_ref[pl.ds(r0, c), :] = _gla_epilogue(o, r_ref[pl.ds(r0, c), :], g_ref[...]).astype(o_ref.dtype)

        kd = k * jnp.exp(last - cum)
        st_t[...] = s_t * jnp.exp(last) + _dot_tn(v, kd.astype(BF16))
        return carry

    lax.fori_loop(0, q_ref.shape[0] // c, chunk, 0)

    @pl.when(step == pl.num_programs(2) - 1)
    def _():
        st_ref[...] = st_t[...].T


def _gla_prompt(qg, kg, vg, cum, rg, norm_g):
    b, s, _ = qg.shape
    kspec = pl.BlockSpec((None, GLA_TILE, GLA_DK), lambda bi, h, t: (bi, t, h))
    vspec = pl.BlockSpec((None, GLA_TILE, GLA_DV), lambda bi, h, t: (bi, t, h))
    return pl.pallas_call(
        _gla_prompt_kernel,
        grid=(b, GLA_HEADS, s // GLA_TILE),
        in_specs=[kspec, kspec, vspec, kspec, vspec, pl.BlockSpec((1, GLA_DV), lambda bi, h, t: (0, h))],
        out_specs=(vspec, pl.BlockSpec((None, None, GLA_DK, GLA_DV), lambda bi, h, t: (bi, h, 0, 0))),
        out_shape=(jax.ShapeDtypeStruct((b, s, GLA_V), BF16),
                   jax.ShapeDtypeStruct((b, GLA_HEADS, GLA_DK, GLA_DV), F32)),
        scratch_shapes=[pltpu.VMEM((GLA_DV, GLA_DK), F32), pltpu.VMEM((GLA_CHUNK, GLA_CHUNK), F32)],
        compiler_params=pltpu.CompilerParams(
            dimension_semantics=("parallel", "parallel", "arbitrary"), vmem_limit_bytes=VMEM_LIMIT),
        name="gla_prompt",
    )(qg, kg, vg, cum, rg, norm_g)


def _post_kernel(x_ref, oa_ref, ob_ref, gates_ref, pe_ref,
                 wa_ref, wb_ref, wo_ref, wfg_ref, wfu_ref, wfd_ref, wpg_ref, wpp_ref,
                 ln1g_ref, ln1b_ref, ln2g_ref, ln2b_ref, y_ref):
    ga = gates_ref[:, 0:D_MODEL].astype(F32)
    gb = gates_ref[:, D_MODEL:2 * D_MODEL].astype(F32)
    merged = (ga * _dot(oa_ref[...].astype(BF16), wa_ref[...])
              + gb * _dot(ob_ref[...].astype(BF16), wb_ref[...]))
    x1 = _layernorm(DN_ALPHA * x_ref[...] + _dot(merged.astype(BF16), wo_ref[...]),
                    ln1g_ref[...], ln1b_ref[...])
    x1b = x1.astype(BF16)
    act = _silu(_dot(x1b, wfg_ref[...])) * _dot(x1b, wfu_ref[...])
    x2 = _layernorm(DN_ALPHA * x1 + _dot(act.astype(BF16), wfd_ref[...]), ln2g_ref[...], ln2b_ref[...])
    gate = _sigmoid(_dot(x2.astype(BF16), wpg_ref[...]))
    y_ref[...] = x2 + gate * _dot(pe_ref[...].astype(BF16), wpp_ref[...])


def _post(x, oa, ob, gates, pe, weights, *, tm):
    rows = x.shape[0]
    row_spec = lambda n: pl.BlockSpec((tm, n), lambda i: (i, 0))
    return pl.pallas_call(
        _post_kernel,
        grid=(rows // tm,),
        in_specs=[row_spec(D_MODEL), row_spec(GROUP_COLS), row_spec(GLA_V), row_spec(2 * D_MODEL),
                  row_spec(PLE_DIM)] + [_const_spec(w.shape) for w in weights],
        out_specs=row_spec(D_MODEL),
        out_shape=jax.ShapeDtypeStruct((rows, D_MODEL), F32),
        compiler_params=pltpu.CompilerParams(
            dimension_semantics=("parallel",), vmem_limit_bytes=VMEM_LIMIT),
        name="post",
    )(x, oa, ob, gates, pe, *weights)


def _pick_column(ref, n):
    lane = lax.broadcasted_iota(jnp.int32, ref.shape, 1)
    return jnp.sum(jnp.where(lane == n, ref[...], 0.0), axis=1, keepdims=True)


def _attn_sample_kernel(qt_ref, kt_ref, vt_ref, c1_ref, c2_ref, c3_ref,
                        n1_ref, n2_ref, n3_ref, oa_ref, o_cols):
    n = pl.program_id(0)
    caches = (c1_ref, c2_ref, c3_ref)
    news = (n1_ref, n2_ref, n3_ref)
    ngroups = len(DILATED_GROUPS)
    qcol = _pick_column(qt_ref, n)
    kcol = _pick_column(kt_ref, n)
    vcol = _pick_column(vt_ref, n)

    @pl.when(n == 0)
    def _():
        o_cols[...] = jnp.zeros_like(o_cols)

    def shifted(x, new):
        lb = x.shape[1]
        lane = lax.broadcasted_iota(jnp.int32, x.shape, 1)
        return jnp.where(lane == lb - 1, new, pltpu.roll(x, lb - 1, 1))

    heads = []
    for h in range(HEADS_PER_GROUP):
        s_old, s_new = [], []
        for gi, (window, dil) in enumerate(DILATED_GROUPS):
            lb = caches[gi].shape[1]
            r0 = gi * GROUP_COLS + h * HEAD_DIM
            q1 = qcol[r0:r0 + HEAD_DIM]
            k1 = kcol[r0:r0 + HEAD_DIM]
            keys = caches[gi][h * HEAD_DIM:(h + 1) * HEAD_DIM, :]
            news[gi][h * HEAD_DIM:(h + 1) * HEAD_DIM, :] = shifted(keys, k1)
            s = jnp.sum(keys * q1, axis=0, keepdims=True)
            pos = lax.broadcasted_iota(jnp.int32, (1, lb), 1)
            s_old.append(jnp.where(pos % dil == 0, s, -jnp.inf))
            s_new.append(jnp.sum(k1 * q1, axis=0, keepdims=True))
        m = s_new[0]
        for gi in range(ngroups):
            m = jnp.maximum(m, jnp.maximum(s_new[gi], jnp.max(s_old[gi], axis=1, keepdims=True)))
        den = jnp.zeros((1, 1), F32)
        acc = jnp.zeros((HEAD_DIM, 1), F32)
        for gi in range(ngroups):
            r0 = gi * GROUP_COLS + h * HEAD_DIM
            v1 = vcol[r0:r0 + HEAD_DIM]
            p_old = jnp.exp(s_old[gi] - m)
            p_new = jnp.exp(s_new[gi] - m)
            den = den + jnp.sum(p_old, axis=1, keepdims=True) + p_new
            rows = slice(GROUP_COLS + h * HEAD_DIM, GROUP_COLS + (h + 1) * HEAD_DIM)
            vals = caches[gi][rows, :]
            news[gi][rows, :] = shifted(vals, v1)
            acc = acc + jnp.sum(vals * p_old, axis=1, keepdims=True) + v1 * p_new
        heads.append(acc / den)
    o_col = jnp.concatenate(heads, axis=0)
    lane = lax.broadcasted_iota(jnp.int32, o_cols.shape, 1)
    o_cols[...] = jnp.where(lane == n, o_col, o_cols[...])

    @pl.when(n == pl.num_programs(0) - 1)
    def _():
        for c in range(GROUP_COLS // LANES):
            oa_ref[:, c * LANES:(c + 1) * LANES] = o_cols[c * LANES:(c + 1) * LANES, :].T


def _attn_sample(qt, kt, vt, caches):
    n = qt.shape[1]
    assert n == LANES
    for (window, _), c in zip(DILATED_GROUPS, caches):
        assert c.shape[2] == window, "every key distance must fall inside the cache"
    cache_specs = [pl.BlockSpec((None,) + c.shape[1:], lambda i: (i, 0, 0)) for c in caches]
    return pl.pallas_call(
        _attn_sample_kernel,
        grid=(n,),
        in_specs=[_const_spec(qt.shape)] * 3 + cache_specs,
        out_specs=tuple(cache_specs) + (_const_spec((n, GROUP_COLS)),),
        out_shape=tuple(jax.ShapeDtypeStruct(c.shape, c.dtype) for c in caches)
        + (jax.ShapeDtypeStruct((n, GROUP_COLS), F32),),
        scratch_shapes=[pltpu.VMEM((GROUP_COLS, n), F32)],
        compiler_params=pltpu.CompilerParams(
            dimension_semantics=("arbitrary",), vmem_limit_bytes=VMEM_LIMIT),
        name="attn_sample",
    )(qt, kt, vt, *caches)


SAMPLE_BLOCK = 8


def _gla_sample_kernel(qt_ref, kt_ref, lat_ref, v_ref, r_ref, g_ref, st_ref, o_ref, st_out_ref, o_scr):
    nb = v_ref.shape[0]
    first = pl.program_id(0) * nb

    def one(j, carry):
        n = first + j
        a = jnp.exp(_pick_column(lat_ref, n))
        k = _pick_column(kt_ref, n)
        q = _pick_column(qt_ref, n)
        for h in range(GLA_HEADS):
            ks = slice(h * GLA_DK, (h + 1) * GLA_DK)
            vs = slice(h * GLA_DV, (h + 1) * GLA_DV)
            st = a[ks] * st_ref[j, h] + k[ks] * v_ref[pl.ds(j, 1), vs]
            st_out_ref[j, h] = st
            o_scr[pl.ds(j, 1), vs] = jnp.sum(q[ks] * st, axis=0, keepdims=True)
        return carry

    lax.fori_loop(0, nb, one, 0)
    for h in range(GLA_HEADS):
        vs = slice(h * GLA_DV, (h + 1) * GLA_DV)
        o_ref[:, vs] = _gla_epilogue(o_scr[:, vs], r_ref[:, vs], g_ref[:, vs]).astype(o_ref.dtype)


def _gla_sample(qgt, kgt, lat, vg, rg, norm_g, state):
    n = vg.shape[0]
    nb = SAMPLE_BLOCK
    row = lambda w: pl.BlockSpec((nb, w), lambda i: (i, 0))
    st_spec = pl.BlockSpec((nb, GLA_HEADS, GLA_DK, GLA_DV), lambda i: (i, 0, 0, 0))
    return pl.pallas_call(
        _gla_sample_kernel,
        grid=(n // nb,),
        in_specs=[_const_spec(qgt.shape), _const_spec(kgt.shape), _const_spec(lat.shape),
                  row(GLA_V), row(GLA_V), _const_spec(norm_g.shape), st_spec],
        out_specs=(row(GLA_V), st_spec),
        out_shape=(jax.ShapeDtypeStruct((n, GLA_V), BF16), jax.ShapeDtypeStruct(state.shape, F32)),
        scratch_shapes=[pltpu.VMEM((nb, GLA_V), F32)],
        compiler_params=pltpu.CompilerParams(
            dimension_semantics=("parallel",), vmem_limit_bytes=VMEM_LIMIT),
        name="gla_sample",
    )(qgt, kgt, lat, vg, rg, norm_g, state)


PROMPT_TM = 256


def _layer_weights(w_in, w_gate_up, b_gate, w_a_out, w_b_out, w_o, ln1_g, ln1_b,
                   w_ff_gate, w_ff_up, w_ff_down, ln2_g, ln2_b, w_ple_gate, w_ple_proj):
    c_gla = 3 * A_WIDTH
    c_lr = c_gla + 2 * GLA_K + 2 * GLA_V
    c_gates = c_lr + GATE_RANK
    wb = w_in.astype(BF16)
    proj = (wb[:, :c_gla], wb[:, c_gla:c_lr], wb[:, c_lr:c_gates], wb[:, c_gates:],
            w_gate_up.astype(BF16), b_gate.reshape(1, GLA_K))
    row = lambda p: p.reshape(1, -1)
    post = (w_a_out.astype(BF16), w_b_out.astype(BF16), w_o.astype(BF16), w_ff_gate.astype(BF16),
            w_ff_up.astype(BF16), w_ff_down.astype(BF16), w_ple_gate.astype(BF16),
            w_ple_proj.astype(BF16), row(ln1_g), row(ln1_b), row(ln2_g), row(ln2_b))
    return proj, post


def _kv_tail(k, v, gi, keep):
    b, s, _ = k.shape
    cols = slice(gi * GROUP_COLS, (gi + 1) * GROUP_COLS)
    kt = k[:, s - keep:, cols].reshape(b, keep, HEADS_PER_GROUP, HEAD_DIM)
    vt = v[:, s - keep:, cols].reshape(b, keep, HEADS_PER_GROUP, HEAD_DIM)
    return jnp.stack([kt, vt], axis=2)


def _positions_minor(cache):
    n, lb = cache.shape[:2]
    return jnp.transpose(cache, (0, 2, 3, 4, 1)).reshape(n, 2 * GROUP_COLS, lb)


def _positions_major(cache):
    n, _, lb = cache.shape
    return jnp.transpose(cache.reshape(n, 2, HEADS_PER_GROUP, HEAD_DIM, lb), (0, 4, 1, 2, 3))


def kernel(x_prompt, x_sample, cache_a1_kv, cache_a2_kv, cache_a3_kv, state_gla, p_prompt, p_sample, w_in, w_gate_up, b_gate, gla_norm_g, w_a_out, w_b_out, w_o, ln1_g, ln1_b, w_ff_gate, w_ff_up, w_ff_down, ln2_g, ln2_b, w_ple_gate, w_ple_proj):
    depth = w_in.shape[0]
    assert depth == 1 and x_sample.shape[1] == 1
    b, s, d = x_prompt.shape
    n = x_sample.shape[0]
    proj_w, post_w = _layer_weights(
        w_in[0], w_gate_up[0], b_gate[0], w_a_out[0], w_b_out[0], w_o[0], ln1_g[0], ln1_b[0],
        w_ff_gate[0], w_ff_up[0], w_ff_down[0], ln2_g[0], ln2_b[0], w_ple_gate[0], w_ple_proj[0])
    norm_g = gla_norm_g[0].reshape(1, GLA_V)

    xp = x_prompt.reshape(b * s, d)
    rope_p = _rope_table(jnp.arange(s))
    q, k, v, qg, kg, vg, rg, cum, gates = _in_proj(
        xp, rope_p, proj_w, tm=PROMPT_TM, rope_period_tiles=s // PROMPT_TM, sample=False)
    three = lambda t: t.reshape(b, s, t.shape[-1])
    k3, v3 = three(k), three(v)
    oa = _attn_prompt(three(q), k3, v3)
    ob, st_p = _gla_prompt(three(qg), three(kg), three(vg), three(cum), three(rg), norm_g)
    y_p = _post(xp, oa.reshape(b * s, GROUP_COLS), ob.reshape(b * s, GLA_V), gates,
                p_prompt[0].reshape(b * s, PLE_DIM), post_w, tm=PROMPT_TM)
    kv_p = [_kv_tail(k3, v3, gi, min(w, s))[None] for gi, (w, _) in enumerate(DILATED_GROUPS)]

    xs = x_sample.reshape(n, d)
    rope_s = _rope_table(jnp.full((n,), PAST_LEN, jnp.int32))
    qt, kt, vt, qgt, kgt, vg, rg, lat, gates = _in_proj(
        xs, rope_s, proj_w, tm=n, rope_period_tiles=1, sample=True)
    caches = [_positions_minor(c[0]) for c in (cache_a1_kv, cache_a2_kv, cache_a3_kv)]
    *new_caches, oa = _attn_sample(qt, kt, vt, caches)
    ob, st_s = _gla_sample(qgt, kgt, lat, vg, rg, norm_g, state_gla[0])
    y_s = _post(xs, oa, ob, gates, p_sample[0].reshape(n, PLE_DIM), post_w, tm=n)
    kv_s = [_positions_major(c)[None] for c in new_caches]

    return (y_p.reshape(b, s, d), y_s.reshape(n, 1, d), kv_p[0], kv_p[1], kv_p[2], st_p[None],
            kv_s[0], kv_s[1], kv_s[2], st_s[None])
```

```python
import functools

import numpy as np
import jax
import jax.numpy as jnp
from jax import lax
from jax.experimental import pallas as pl
from jax.experimental.pallas import tpu as pltpu

F32 = jnp.float32
BF16 = jnp.bfloat16

D_MODEL = 1024
HEAD_DIM = 64
HEADS_PER_GROUP = 4
DILATED_GROUPS = ((128, 1), (512, 4), (2048, 16))
GROUP_COLS = HEADS_PER_GROUP * HEAD_DIM
A_WIDTH = len(DILATED_GROUPS) * GROUP_COLS
ROT_DIM = HEAD_DIM // 4
ROPE_THETA = 500000.0
KEYS_PER_WINDOW = 128
GLA_HEADS = 4
GLA_DK = 128
GLA_DV = 256
GLA_K = GLA_HEADS * GLA_DK
GLA_V = GLA_HEADS * GLA_DV
GATE_RANK = 16
GATE_NORMALIZER = 16.0
GLA_CHUNK = 128
D_FF = 2816
PLE_DIM = 256
DN_ALPHA = 2.0 ** 0.25
LN_EPS = 1e-5
RMS_EPS = 1e-6
PAST_LEN = 8192

LANES = 128
SUBLANES = 8
VMEM_LIMIT = 56 * 1024 * 1024

GLA_FACTORED_RANGE = 40.0


def _dot(a, b):
    return jnp.dot(a, b, preferred_element_type=F32)


def _dot_nt(a, b):
    return lax.dot_general(a, b, (((1,), (1,)), ((), ())), preferred_element_type=F32)


def _dot_tn(a, b):
    return lax.dot_general(a, b, (((0,), (0,)), ((), ())), preferred_element_type=F32)


def _const_spec(shape):
    return pl.BlockSpec(shape, lambda *_: (0,) * len(shape))


def _layernorm(x, g, b):
    xc = x - jnp.mean(x, -1, keepdims=True)
    var = jnp.mean(xc * xc, -1, keepdims=True)
    return xc * lax.rsqrt(var + LN_EPS) * g + b


def _sigmoid(x):
    return 1.0 / (1.0 + jnp.exp(-x))


def _silu(x):
    return x * _sigmoid(x)


def _split3_bf16(x):
    hi = x.astype(BF16)
    r1 = x - hi.astype(F32)
    mid = r1.astype(BF16)
    lo = (r1 - mid.astype(F32)).astype(BF16)
    return hi, mid, lo


def _in_proj_kernel(x_ref, rope_ref, wa_ref, wg_ref, wlr_ref, wgates_ref, wup_ref, bg_ref,
                    q_ref, k_ref, v_ref, qg_ref, kg_ref, vg_ref, rg_ref, la_ref, gates_ref,
                    *, sample):
    tm = x_ref.shape[0]
    xb = x_ref[...].astype(BF16)
    cos = rope_ref[:, 0:LANES]
    sin_lo = rope_ref[:, LANES:2 * LANES]
    sin_hi = rope_ref[:, 2 * LANES:3 * LANES]

    def rope(t):
        return (t * cos + pltpu.roll(t, LANES - ROT_DIM // 2, 1) * sin_lo
                + pltpu.roll(t, ROT_DIM // 2, 1) * sin_hi)

    def put(ref, col, val):
        if sample:
            ref[col * LANES:(col + 1) * LANES, :] = val.T
        else:
            ref[:, col * LANES:(col + 1) * LANES] = val

    tiles = GROUP_COLS // LANES
    for c in range(A_WIDTH // GROUP_COLS):
        lo, hi = c * GROUP_COLS, (c + 1) * GROUP_COLS
        qc = _dot(xb, wa_ref[:, lo:hi])
        kc = _dot(xb, wa_ref[:, A_WIDTH + lo:A_WIDTH + hi])
        vc = _dot(xb, wa_ref[:, 2 * A_WIDTH + lo:2 * A_WIDTH + hi])
        for s in range(tiles):
            sl = slice(s * LANES, (s + 1) * LANES)
            put(q_ref, c * tiles + s, rope(qc[:, sl]) * (HEAD_DIM ** -0.5))
            put(k_ref, c * tiles + s, rope(kc[:, sl]))
            put(v_ref, c * tiles + s, vc[:, sl])

    glr = _dot(xb, wlr_ref[...])
    z = _dot(glr.astype(BF16), wup_ref[...]) + bg_ref[...]
    loga = -(jnp.maximum(-z, 0.0) + jnp.log1p(jnp.exp(-jnp.abs(z)))) * (1.0 / GATE_NORMALIZER)
    if not sample:
        row = lax.broadcasted_iota(jnp.int32, (tm, tm), 0)
        col = lax.broadcasted_iota(jnp.int32, (tm, tm), 1)
        tri = ((row // GLA_CHUNK == col // GLA_CHUNK) & (col <= row)).astype(BF16)
        hi, mid, lo = _split3_bf16(loga)
        loga = _dot(tri, hi) + _dot(tri, mid) + _dot(tri, lo)
    qg = _dot(xb, wg_ref[:, 0:GLA_K]) * (GLA_DK ** -0.5)
    kg = _dot(xb, wg_ref[:, GLA_K:2 * GLA_K])
    for s in range(GLA_K // LANES):
        sl = slice(s * LANES, (s + 1) * LANES)
        put(qg_ref, s, qg[:, sl])
        put(kg_ref, s, kg[:, sl])
        put(la_ref, s, loga[:, sl])
    vg_ref[...] = _dot(xb, wg_ref[:, 2 * GLA_K:2 * GLA_K + GLA_V])
    rg_ref[...] = _dot(xb, wg_ref[:, 2 * GLA_K + GLA_V:2 * GLA_K + 2 * GLA_V]).astype(BF16)

    gates_ref[...] = _sigmoid(_dot(xb, wgates_ref[...])).astype(BF16)


def _in_proj(x, rope_tab, weights, *, tm, rope_period_tiles, sample):
    rows = x.shape[0]
    assert not sample or tm == rows == LANES
    wa, wg, wlr, wgates, wup, bg = weights

    def out(width, dtype, transposed=False):
        if transposed and sample:
            return jax.ShapeDtypeStruct((width, rows), dtype), _const_spec((width, rows))
        return jax.ShapeDtypeStruct((rows, width), dtype), pl.BlockSpec((tm, width), lambda i: (i, 0))

    outs = (
        out(A_WIDTH, F32, True),
        out(A_WIDTH, F32, True),
        out(A_WIDTH, F32, True),
        out(GLA_K, F32, True),
        out(GLA_K, F32, True),
        out(GLA_V, F32),
        out(GLA_V, BF16),
        out(GLA_K, F32, True),
        out(2 * D_MODEL, BF16),
    )
    return pl.pallas_call(
        functools.partial(_in_proj_kernel, sample=sample),
        grid=(rows // tm,),
        in_specs=[
            pl.BlockSpec((tm, D_MODEL), lambda i: (i, 0)),
            pl.BlockSpec((tm, 3 * LANES), lambda i: (i % rope_period_tiles, 0)),
            _const_spec(wa.shape), _const_spec(wg.shape), _const_spec(wlr.shape),
            _const_spec(wgates.shape), _const_spec(wup.shape), _const_spec(bg.shape),
        ],
        out_specs=tuple(o[1] for o in outs),
        out_shape=tuple(o[0] for o in outs),
        compiler_params=pltpu.CompilerParams(
            dimension_semantics=("parallel",), vmem_limit_bytes=VMEM_LIMIT),
        name="in_proj_sample" if sample else "in_proj_prompt",
    )(x, rope_tab, wa, wg, wlr, wgates, wup, bg)


def _rope_table(pos):
    half = ROT_DIM // 2
    inv_freq = ROPE_THETA ** (-jnp.arange(0, ROT_DIM, 2, dtype=F32) / ROT_DIM)
    ang = pos.astype(F32)[:, None] * inv_freq[None, :]
    cos, sin = jnp.cos(ang), jnp.sin(ang)
    lane = np.arange(LANES) % HEAD_DIM
    fidx = lane % half
    first = lane < half
    second = (lane >= half) & (lane < ROT_DIM)
    cos_t = jnp.where((first | second)[None, :], cos[:, fidx], 1.0)
    sin_lo = jnp.where(first[None, :], -sin[:, fidx], 0.0)
    sin_hi = jnp.where(second[None, :], sin[:, fidx], 0.0)
    return jnp.concatenate([cos_t, sin_lo, sin_hi], axis=1)


ATT_TILE = 2048
Q_BLOCK = 128
HEADS_PER_TILE = LANES // HEAD_DIM


def _attn_prompt_kernel(q_ref, kc_ref, kp_ref, vc_ref, vp_ref, o_ref, kk, vv, og, lg, bias):
    t = pl.program_id(1)
    g = pl.program_id(3)
    band_keys = 2 * Q_BLOCK
    qi = lax.broadcasted_iota(jnp.int32, (Q_BLOCK, HEADS_PER_TILE * band_keys), 0)
    kc = lax.broadcasted_iota(jnp.int32, (Q_BLOCK, HEADS_PER_TILE * band_keys), 1) % band_keys
    back = qi + Q_BLOCK - kc
    band = (back >= 0) & (back <= KEYS_PER_WINDOW)
    bias[0] = jnp.where(band, 0.0, -jnp.inf)
    bias[1] = jnp.where(band & (kc >= Q_BLOCK), 0.0, -jnp.inf)
    head0 = lax.broadcasted_iota(jnp.int32, (band_keys, LANES), 1) < HEAD_DIM
    head0_q = lax.broadcasted_iota(jnp.int32, (Q_BLOCK, LANES), 1) < HEAD_DIM

    for gi, (window, dil) in enumerate(DILATED_GROUPS):

        @pl.when(g == gi)
        def _(gi=gi, window=window, dil=dil):
            kk[0:window, :] = kp_ref[ATT_TILE - window:ATT_TILE, :]
            kk[window:window + ATT_TILE, :] = kc_ref[...]
            vv[0:window, :] = vp_ref[ATT_TILE - window:ATT_TILE, :]
            vv[window:window + ATT_TILE, :] = vc_ref[...]
            nblk = ATT_TILE // dil // Q_BLOCK

            def rows(start, size):
                if dil == 1:
                    return pl.ds(pl.multiple_of(start, Q_BLOCK), size)
                return pl.ds(start, size, stride=dil)

            def body(idx, carry):
                r = idx // nblk
                n = idx % nblk
                base = r + dil * Q_BLOCK * n
                q2 = q_ref[rows(base, Q_BLOCK), :].astype(BF16)
                k2 = kk[rows(base, band_keys), :]
                v2 = vv[rows(base, band_keys), :]
                kz = jnp.concatenate([jnp.where(head0, k2, 0.0), jnp.where(head0, 0.0, k2)], axis=0)
                no_prev = jnp.logical_and(t == 0, n == 0).astype(jnp.int32)
                s = _dot_nt(q2, kz.astype(BF16)) + bias[no_prev]
                s0, s1 = s[:, :band_keys], s[:, band_keys:]
                m0 = jnp.max(s0, axis=-1, keepdims=True)
                m1 = jnp.max(s1, axis=-1, keepdims=True)
                p = jnp.concatenate([jnp.exp(s0 - m0), jnp.exp(s1 - m1)], axis=1).astype(BF16)
                ones0 = head0.astype(F32)
                vz = jnp.concatenate([
                    jnp.concatenate([jnp.where(head0, v2, 0.0), ones0], axis=1),
                    jnp.concatenate([jnp.where(head0, 0.0, v2), 1.0 - ones0], axis=1)], axis=0)
                res = _dot(p, vz.astype(BF16))
                den = res[:, LANES:]
                og[gi, rows(base, Q_BLOCK), :] = res[:, :LANES] / den
                m = jnp.where(head0_q, jnp.broadcast_to(m0, den.shape), jnp.broadcast_to(m1, den.shape))
                lg[gi, rows(base, Q_BLOCK), :] = m + jnp.log(den)
                return carry

            lax.fori_loop(0, dil * nblk, body, 0, unroll=4)

    @pl.when(g == len(DILATED_GROUPS) - 1)
    def _():
        rows = 256

        def combine(c, carry):
            r0 = pl.multiple_of(c * rows, rows)
            l0 = lg[0, pl.ds(r0, rows), :]
            l1 = lg[1, pl.ds(r0, rows), :]
            l2 = lg[2, pl.ds(r0, rows), :]
            mx = jnp.maximum(jnp.maximum(l0, l1), l2)
            e0, e1, e2 = jnp.exp(l0 - mx), jnp.exp(l1 - mx), jnp.exp(l2 - mx)
            acc = (e0 * og[0, pl.ds(r0, rows), :] + e1 * og[1, pl.ds(r0, rows), :]
                   + e2 * og[2, pl.ds(r0, rows), :])
            o_ref[pl.ds(r0, rows), :] = (acc / (e0 + e1 + e2)).astype(o_ref.dtype)
            return carry

        lax.fori_loop(0, ATT_TILE // rows, combine, 0)


def _attn_prompt(q, k, v):
    b, s, _ = q.shape
    tiles_per_group = GROUP_COLS // LANES
    cur = pl.BlockSpec((None, ATT_TILE, LANES), lambda bi, t, c, g: (bi, t, g * tiles_per_group + c))
    prev = pl.BlockSpec((None, ATT_TILE, LANES),
                        lambda bi, t, c, g: (bi, jnp.maximum(t - 1, 0), g * tiles_per_group + c))
    max_window = max(w for w, _ in DILATED_GROUPS)
    ngroups = len(DILATED_GROUPS)
    return pl.pallas_call(
        _attn_prompt_kernel,
        grid=(b, s // ATT_TILE, tiles_per_group, ngroups),
        in_specs=[cur, cur, prev, cur, prev],
        out_specs=pl.BlockSpec((None, ATT_TILE, LANES), lambda bi, t, c, g: (bi, t, c)),
        out_shape=jax.ShapeDtypeStruct((b, s, GROUP_COLS), BF16),
        scratch_shapes=[
            pltpu.VMEM((max_window + ATT_TILE, LANES), F32),
            pltpu.VMEM((max_window + ATT_TILE, LANES), F32),
            pltpu.VMEM((ngroups, ATT_TILE, LANES), F32),
            pltpu.VMEM((ngroups, ATT_TILE, LANES), F32),
            pltpu.VMEM((2, Q_BLOCK, HEADS_PER_TILE * 2 * Q_BLOCK), F32),
        ],
        compiler_params=pltpu.CompilerParams(
            dimension_semantics=("parallel", "parallel", "parallel", "arbitrary"),
            vmem_limit_bytes=VMEM_LIMIT),
        name="attn_prompt",
    )(q, k, k, v, v)


GLA_TILE = 512


def _gla_epilogue(o, r, g):
    o = o * lax.rsqrt(jnp.mean(o * o, -1, keepdims=True) + RMS_EPS)
    return o * g * _silu(r.astype(F32))


def _gla_prompt_kernel(q_ref, k_ref, v_ref, cum_ref, r_ref, g_ref, o_ref, st_ref, st_t):
    step = pl.program_id(1)
    c = GLA_CHUNK
    nchunks = q_ref.shape[0] // c

    @pl.when(step == 0)
    def _():
        st_t[...] = jnp.zeros_like(st_t)

    ri = lax.broadcasted_iota(jnp.int32, (c, c), 0)
    ci = lax.broadcasted_iota(jnp.int32, (c, c), 1)
    row_id = lax.broadcasted_iota(jnp.int32, (c, GLA_DK), 0)

    def chunk(ch, factored):
        r0 = ch * c if isinstance(ch, int) else pl.multiple_of(ch * c, c)
        for h in range(GLA_HEADS):
            ks = slice(h * GLA_DK, (h + 1) * GLA_DK)
            vs = slice(h * GLA_DV, (h + 1) * GLA_DV)
            q = q_ref[pl.ds(r0, c), ks]
            k = k_ref[pl.ds(r0, c), ks]
            v = v_ref[pl.ds(r0, c), vs].astype(BF16)
            cum = cum_ref[pl.ds(r0, c), ks]
            last = cum[c - 1:c]
            decay = jnp.exp(last)
            qe = (q * jnp.exp(cum)).astype(BF16)
            if factored:
                ke = k * jnp.exp(-cum)
                a = jnp.where(ci <= ri, _dot_nt(qe, ke.astype(BF16)), 0.0)
                kd = ke * decay
            else:
                def col(j, a):
                    cj = jnp.sum(jnp.where(row_id == j, cum, 0.0), axis=0, keepdims=True)
                    kj = jnp.sum(jnp.where(row_id == j, k, 0.0), axis=0, keepdims=True)
                    e = jnp.exp(jnp.where(row_id >= j, cum - cj, -jnp.inf))
                    sj = jnp.sum(q * kj * e, axis=-1, keepdims=True)
                    return jnp.where(ci == j, sj, a)

                a = lax.fori_loop(0, c, col, jnp.zeros((c, c), F32))
                kd = k * jnp.exp(last - cum)
            s_t = st_t[h]
            o = _dot_nt(qe, s_t.astype(BF16)) + _dot(a.astype(BF16), v)
            o_ref[pl.ds(r0, c), vs] = _gla_epilogue(o, r_ref[pl.ds(r0, c), vs], g_ref[:, vs]).astype(o_ref.dtype)
            st_t[h] = s_t * decay + _dot_tn(v, kd.astype(BF16))

    total = -jnp.min(cum_ref[...])

    @pl.when(total < GLA_FACTORED_RANGE)
    def _():
        for ch in range(nchunks):
            chunk(ch, True)

    @pl.when(total >= GLA_FACTORED_RANGE)
    def _():
        def body(ch, carry):
            chunk(ch, False)
            return carry

        lax.fori_loop(0, nchunks, body, 0)

    @pl.when(step == pl.num_programs(1) - 1)
    def _():
        for h in range(GLA_HEADS):
            st_ref[h] = st_t[h].T


def _gla_prompt(qg, kg, vg, cum, rg, norm_g):
    b, s, _ = qg.shape
    kspec = pl.BlockSpec((None, GLA_TILE, GLA_K), lambda bi, t: (bi, t, 0))
    vspec = pl.BlockSpec((None, GLA_TILE, GLA_V), lambda bi, t: (bi, t, 0))
    return pl.pallas_call(
        _gla_prompt_kernel,
        grid=(b, s // GLA_TILE),
        in_specs=[kspec, kspec, vspec, kspec, vspec, _const_spec(norm_g.shape)],
        out_specs=(vspec, pl.BlockSpec((None, GLA_HEADS, GLA_DK, GLA_DV), lambda bi, t: (bi, 0, 0, 0))),
        out_shape=(jax.ShapeDtypeStruct((b, s, GLA_V), BF16),
                   jax.ShapeDtypeStruct((b, GLA_HEADS, GLA_DK, GLA_DV), F32)),
        scratch_shapes=[pltpu.VMEM((GLA_HEADS, GLA_DV, GLA_DK), F32)],
        compiler_params=pltpu.CompilerParams(
            dimension_semantics=("parallel", "arbitrary"), vmem_limit_bytes=VMEM_LIMIT),
        name="gla_prompt",
    )(qg, kg, vg, cum, rg, norm_g)


def _post_kernel(x_ref, oa_ref, ob_ref, gates_ref, pe_ref,
                 wa_ref, wb_ref, wo_ref, wfg_ref, wfu_ref, wfd_ref, wpg_ref, wpp_ref,
                 ln1g_ref, ln1b_ref, ln2g_ref, ln2b_ref, y_ref):
    ga = gates_ref[:, 0:D_MODEL].astype(F32)
    gb = gates_ref[:, D_MODEL:2 * D_MODEL].astype(F32)
    merged = (ga * _dot(oa_ref[...].astype(BF16), wa_ref[...])
              + gb * _dot(ob_ref[...].astype(BF16), wb_ref[...]))
    x1 = _layernorm(DN_ALPHA * x_ref[...] + _dot(merged.astype(BF16), wo_ref[...]),
                    ln1g_ref[...], ln1b_ref[...])
    x1b = x1.astype(BF16)
    act = _silu(_dot(x1b, wfg_ref[...])) * _dot(x1b, wfu_ref[...])
    x2 = _layernorm(DN_ALPHA * x1 + _dot(act.astype(BF16), wfd_ref[...]), ln2g_ref[...], ln2b_ref[...])
    gate = _sigmoid(_dot(x2.astype(BF16), wpg_ref[...]))
    y_ref[...] = x2 + gate * _dot(pe_ref[...].astype(BF16), wpp_ref[...])


def _post(x, oa, ob, gates, pe, weights, *, tm):
    rows = x.shape[0]
    row_spec = lambda n: pl.BlockSpec((tm, n), lambda i: (i, 0))
    return pl.pallas_call(
        _post_kernel,
        grid=(rows // tm,),
        in_specs=[row_spec(D_MODEL), row_spec(GROUP_COLS), row_spec(GLA_V), row_spec(2 * D_MODEL),
                  row_spec(PLE_DIM)] + [_const_spec(w.shape) for w in weights],
        out_specs=row_spec(D_MODEL),
        out_shape=jax.ShapeDtypeStruct((rows, D_MODEL), F32),
        compiler_params=pltpu.CompilerParams(
            dimension_semantics=("parallel",), vmem_limit_bytes=VMEM_LIMIT),
        name="post",
    )(x, oa, ob, gates, pe, *weights)


def _pick_column(ref, n):
    lane = lax.broadcasted_iota(jnp.int32, ref.shape, 1)
    return jnp.sum(jnp.where(lane == n, ref[...], 0.0), axis=1, keepdims=True)


def _attn_sample_kernel(qt_ref, kt_ref, vt_ref, c1_ref, c2_ref, c3_ref,
                        n1_ref, n2_ref, n3_ref, oa_ref, o_cols):
    n = pl.program_id(0)
    caches = (c1_ref, c2_ref, c3_ref)
    news = (n1_ref, n2_ref, n3_ref)
    ngroups = len(DILATED_GROUPS)
    qcol = _pick_column(qt_ref, n)
    kcol = _pick_column(kt_ref, n)
    vcol = _pick_column(vt_ref, n)

    @pl.when(n == 0)
    def _():
        o_cols[...] = jnp.zeros_like(o_cols)

    def shifted(x, new):
        lb = x.shape[1]
        lane = lax.broadcasted_iota(jnp.int32, x.shape, 1)
        return jnp.where(lane == lb - 1, new, pltpu.roll(x, lb - 1, 1))

    heads = []
    for h in range(HEADS_PER_GROUP):
        s_old, s_new = [], []
        for gi, (window, dil) in enumerate(DILATED_GROUPS):
            lb = caches[gi].shape[1]
            r0 = gi * GROUP_COLS + h * HEAD_DIM
            q1 = qcol[r0:r0 + HEAD_DIM]
            k1 = kcol[r0:r0 + HEAD_DIM]
            keys = caches[gi][h * HEAD_DIM:(h + 1) * HEAD_DIM, :]
            news[gi][h * HEAD_DIM:(h + 1) * HEAD_DIM, :] = shifted(keys, k1)
            s = jnp.sum(keys * q1, axis=0, keepdims=True)
            pos = lax.broadcasted_iota(jnp.int32, (1, lb), 1)
            s_old.append(jnp.where(pos % dil == 0, s, -jnp.inf))
            s_new.append(jnp.sum(k1 * q1, axis=0, keepdims=True))
        m = s_new[0]
        for gi in range(ngroups):
            m = jnp.maximum(m, jnp.maximum(s_new[gi], jnp.max(s_old[gi], axis=1, keepdims=True)))
        den = jnp.zeros((1, 1), F32)
        acc = jnp.zeros((HEAD_DIM, 1), F32)
        for gi in range(ngroups):
            r0 = gi * GROUP_COLS + h * HEAD_DIM
            v1 = vcol[r0:r0 + HEAD_DIM]
            p_old = jnp.exp(s_old[gi] - m)
            p_new = jnp.exp(s_new[gi] - m)
            den = den + jnp.sum(p_old, axis=1, keepdims=True) + p_new
            rows = slice(GROUP_COLS + h * HEAD_DIM, GROUP_COLS + (h + 1) * HEAD_DIM)
            vals = caches[gi][rows, :]
            news[gi][rows, :] = shifted(vals, v1)
            acc = acc + jnp.sum(vals * p_old, axis=1, keepdims=True) + v1 * p_new
        heads.append(acc / den)
    o_col = jnp.concatenate(heads, axis=0)
    lane = lax.broadcasted_iota(jnp.int32, o_cols.shape, 1)
    o_cols[...] = jnp.where(lane == n, o_col, o_cols[...])

    @pl.when(n == pl.num_programs(0) - 1)
    def _():
        for c in range(GROUP_COLS // LANES):
            oa_ref[:, c * LANES:(c + 1) * LANES] = o_cols[c * LANES:(c + 1) * LANES, :].T


def _attn_sample(qt, kt, vt, caches):
    n = qt.shape[1]
    assert n == LANES
    for (window, _), c in zip(DILATED_GROUPS, caches):
        assert c.shape[2] == window, "every key distance must fall inside the cache"
    cache_specs = [pl.BlockSpec((None,) + c.shape[1:], lambda i: (i, 0, 0)) for c in caches]
    return pl.pallas_call(
        _attn_sample_kernel,
        grid=(n,),
        in_specs=[_const_spec(qt.shape)] * 3 + cache_specs,
        out_specs=tuple(cache_specs) + (_const_spec((n, GROUP_COLS)),),
        out_shape=tuple(jax.ShapeDtypeStruct(c.shape, c.dtype) for c in caches)
        + (jax.ShapeDtypeStruct((n, GROUP_COLS), F32),),
        scratch_shapes=[pltpu.VMEM((GROUP_COLS, n), F32)],
        compiler_params=pltpu.CompilerParams(
            dimension_semantics=("arbitrary",), vmem_limit_bytes=VMEM_LIMIT),
        name="attn_sample",
    )(qt, kt, vt, *caches)


SAMPLE_BLOCK = 8


def _gla_sample_kernel(qt_ref, kt_ref, lat_ref, v_ref, r_ref, g_ref, st_ref, o_ref, st_out_ref, o_scr):
    nb = v_ref.shape[0]
    first = pl.program_id(0) * nb

    def one(j, carry):
        n = first + j
        a = jnp.exp(_pick_column(lat_ref, n))
        k = _pick_column(kt_ref, n)
        q = _pick_column(qt_ref, n)
        for h in range(GLA_HEADS):
            ks = slice(h * GLA_DK, (h + 1) * GLA_DK)
            vs = slice(h * GLA_DV, (h + 1) * GLA_DV)
            st = a[ks] * st_ref[j, h] + k[ks] * v_ref[pl.ds(j, 1), vs]
            st_out_ref[j, h] = st
            o_scr[pl.ds(j, 1), vs] = jnp.sum(q[ks] * st, axis=0, keepdims=True)
        return carry

    lax.fori_loop(0, nb, one, 0)
    for h in range(GLA_HEADS):
        vs = slice(h * GLA_DV, (h + 1) * GLA_DV)
        o_ref[:, vs] = _gla_epilogue(o_scr[:, vs], r_ref[:, vs], g_ref[:, vs]).astype(o_ref.dtype)


def _gla_sample(qgt, kgt, lat, vg, rg, norm_g, state):
    n = vg.shape[0]
    nb = SAMPLE_BLOCK
    row = lambda w: pl.BlockSpec((nb, w), lambda i: (i, 0))
    st_spec = pl.BlockSpec((nb, GLA_HEADS, GLA_DK, GLA_DV), lambda i: (i, 0, 0, 0))
    return pl.pallas_call(
        _gla_sample_kernel,
        grid=(n // nb,),
        in_specs=[_const_spec(qgt.shape), _const_spec(kgt.shape), _const_spec(lat.shape),
                  row(GLA_V), row(GLA_V), _const_spec(norm_g.shape), st_spec],
        out_specs=(row(GLA_V), st_spec),
        out_shape=(jax.ShapeDtypeStruct((n, GLA_V), BF16), jax.ShapeDtypeStruct(state.shape, F32)),
        scratch_shapes=[pltpu.VMEM((nb, GLA_V), F32)],
        compiler_params=pltpu.CompilerParams(
            dimension_semantics=("parallel",), vmem_limit_bytes=VMEM_LIMIT),
        name="gla_sample",
    )(qgt, kgt, lat, vg, rg, norm_g, state)


PROMPT_TM = 256


def _layer_weights(w_in, w_gate_up, b_gate, w_a_out, w_b_out, w_o, ln1_g, ln1_b,
                   w_ff_gate, w_ff_up, w_ff_down, ln2_g, ln2_b, w_ple_gate, w_ple_proj):
    c_gla = 3 * A_WIDTH
    c_lr = c_gla + 2 * GLA_K + 2 * GLA_V
    c_gates = c_lr + GATE_RANK
    wb = w_in.astype(BF16)
    proj = (wb[:, :c_gla], wb[:, c_gla:c_lr], wb[:, c_lr:c_gates], wb[:, c_gates:],
            w_gate_up.astype(BF16), b_gate.reshape(1, GLA_K))
    row = lambda p: p.reshape(1, -1)
    post = (w_a_out.astype(BF16), w_b_out.astype(BF16), w_o.astype(BF16), w_ff_gate.astype(BF16),
            w_ff_up.astype(BF16), w_ff_down.astype(BF16), w_ple_gate.astype(BF16),
            w_ple_proj.astype(BF16), row(ln1_g), row(ln1_b), row(ln2_g), row(ln2_b))
    return proj, post


def _kv_tail(k, v, gi, keep):
    b, s, _ = k.shape
    cols = slice(gi * GROUP_COLS, (gi + 1) * GROUP_COLS)
    kt = k[:, s - keep:, cols].reshape(b, keep, HEADS_PER_GROUP, HEAD_DIM)
    vt = v[:, s - keep:, cols].reshape(b, keep, HEADS_PER_GROUP, HEAD_DIM)
    return jnp.stack([kt, vt], axis=2)


def _positions_minor(cache):
    n, lb = cache.shape[:2]
    return jnp.transpose(cache, (0, 2, 3, 4, 1)).reshape(n, 2 * GROUP_COLS, lb)


def _positions_major(cache):
    n, _, lb = cache.shape
    return jnp.transpose(cache.reshape(n, 2, HEADS_PER_GROUP, HEAD_DIM, lb), (0, 4, 1, 2, 3))


def kernel(x_prompt, x_sample, cache_a1_kv, cache_a2_kv, cache_a3_kv, state_gla, p_prompt, p_sample, w_in, w_gate_up, b_gate, gla_norm_g, w_a_out, w_b_out, w_o, ln1_g, ln1_b, w_ff_gate, w_ff_up, w_ff_down, ln2_g, ln2_b, w_ple_gate, w_ple_proj):
    depth = w_in.shape[0]
    assert depth == 1 and x_sample.shape[1] == 1
    b, s, d = x_prompt.shape
    n = x_sample.shape[0]
    proj_w, post_w = _layer_weights(
        w_in[0], w_gate_up[0], b_gate[0], w_a_out[0], w_b_out[0], w_o[0], ln1_g[0], ln1_b[0],
        w_ff_gate[0], w_ff_up[0], w_ff_down[0], ln2_g[0], ln2_b[0], w_ple_gate[0], w_ple_proj[0])
    norm_g = gla_norm_g[0].reshape(1, GLA_V)

    xp = x_prompt.reshape(b * s, d)
    rope_p = _rope_table(jnp.arange(s))
    q, k, v, qg, kg, vg, rg, cum, gates = _in_proj(
        xp, rope_p, proj_w, tm=PROMPT_TM, rope_period_tiles=s // PROMPT_TM, sample=False)
    three = lambda t: t.reshape(b, s, t.shape[-1])
    k3, v3 = three(k), three(v)
    oa = _attn_prompt(three(q), k3, v3)
    ob, st_p = _gla_prompt(three(qg), three(kg), three(vg), three(cum), three(rg), norm_g)
    y_p = _post(xp, oa.reshape(b * s, GROUP_COLS), ob.reshape(b * s, GLA_V), gates,
                p_prompt[0].reshape(b * s, PLE_DIM), post_w, tm=PROMPT_TM)
    kv_p = [_kv_tail(k3, v3, gi, min(w, s))[None] for gi, (w, _) in enumerate(DILATED_GROUPS)]

    xs = x_sample.reshape(n, d)
    rope_s = _rope_table(jnp.full((n,), PAST_LEN, jnp.int32))
    qt, kt, vt, qgt, kgt, vg, rg, lat, gates = _in_proj(
        xs, rope_s, proj_w, tm=n, rope_period_tiles=1, sample=True)
    caches = [_positions_minor(c[0]) for c in (cache_a1_kv, cache_a2_kv, cache_a3_kv)]
    *new_caches, oa = _attn_sample(qt, kt, vt, caches)
    ob, st_s = _gla_sample(qgt, kgt, lat, vg, rg, norm_g, state_gla[0])
    y_s = _post(xs, oa, ob, gates, p_sample[0].reshape(n, PLE_DIM), post_w, tm=n)
    kv_s = [_positions_major(c)[None] for c in new_caches]

    return (y_p.reshape(b, s, d), y_s.reshape(n, 1, d), kv_p[0], kv_p[1], kv_p[2], st_p[None],
            kv_s[0], kv_s[1], kv_s[2], st_s[None])
```

```python
import functools

import numpy as np
import jax
import jax.numpy as jnp
from jax import lax
from jax.experimental import pallas as pl
from jax.experimental.pallas import tpu as pltpu

F32 = jnp.float32
BF16 = jnp.bfloat16

D_MODEL = 1024
HEAD_DIM = 64
HEADS_PER_GROUP = 4
DILATED_GROUPS = ((128, 1), (512, 4), (2048, 16))
GROUP_COLS = HEADS_PER_GROUP * HEAD_DIM
A_WIDTH = len(DILATED_GROUPS) * GROUP_COLS
ROT_DIM = HEAD_DIM // 4
ROPE_THETA = 500000.0
KEYS_PER_WINDOW = 128
GLA_HEADS = 4
GLA_DK = 128
GLA_DV = 256
GLA_K = GLA_HEADS * GLA_DK
GLA_V = GLA_HEADS * GLA_DV
GATE_RANK = 16
GATE_NORMALIZER = 16.0
GLA_CHUNK = 128
D_FF = 2816
PLE_DIM = 256
DN_ALPHA = 2.0 ** 0.25
LN_EPS = 1e-5
RMS_EPS = 1e-6
PAST_LEN = 8192

LANES = 128
SUBLANES = 8
VMEM_LIMIT = 56 * 1024 * 1024

GLA_FACTORED_RANGE = 40.0


def _dot(a, b):
    return jnp.dot(a, b, preferred_element_type=F32)


def _dot_nt(a, b):
    return lax.dot_general(a, b, (((1,), (1,)), ((), ())), preferred_element_type=F32)


def _dot_tn(a, b):
    return lax.dot_general(a, b, (((0,), (0,)), ((), ())), preferred_element_type=F32)


def _const_spec(shape):
    return pl.BlockSpec(shape, lambda *_: (0,) * len(shape), pipeline_mode=pl.Buffered(1))


def _layernorm(x, g, b):
    xc = x - jnp.mean(x, -1, keepdims=True)
    var = jnp.mean(xc * xc, -1, keepdims=True)
    return xc * lax.rsqrt(var + LN_EPS) * g + b


def _sigmoid(x):
    return 1.0 / (1.0 + jnp.exp(-x))


def _silu(x):
    return x * _sigmoid(x)


def _split3_bf16(x):
    hi = x.astype(BF16)
    r1 = x - hi.astype(F32)
    mid = r1.astype(BF16)
    lo = (r1 - mid.astype(F32)).astype(BF16)
    return hi, mid, lo


def _in_proj_kernel(x_ref, rope_ref, wa_ref, wg_ref, wlr_ref, wgates_ref, wup_ref, bg_ref,
                    q_ref, k_ref, v_ref, qg_ref, kg_ref, vg_ref, rg_ref, la_ref, gates_ref,
                    *, sample):
    tm = x_ref.shape[0]
    xb = x_ref[...].astype(BF16)
    cos = rope_ref[:, 0:LANES]
    sin_lo = rope_ref[:, LANES:2 * LANES]
    sin_hi = rope_ref[:, 2 * LANES:3 * LANES]

    def rope(t):
        return (t * cos + pltpu.roll(t, LANES - ROT_DIM // 2, 1) * sin_lo
                + pltpu.roll(t, ROT_DIM // 2, 1) * sin_hi)

    def put(ref, col, val):
        if sample:
            ref[col * LANES:(col + 1) * LANES, :] = val.T
        else:
            ref[:, col * LANES:(col + 1) * LANES] = val

    tiles = GROUP_COLS // LANES
    for c in range(A_WIDTH // GROUP_COLS):
        lo, hi = c * GROUP_COLS, (c + 1) * GROUP_COLS
        qc = _dot(xb, wa_ref[:, lo:hi])
        kc = _dot(xb, wa_ref[:, A_WIDTH + lo:A_WIDTH + hi])
        vc = _dot(xb, wa_ref[:, 2 * A_WIDTH + lo:2 * A_WIDTH + hi])
        for s in range(tiles):
            sl = slice(s * LANES, (s + 1) * LANES)
            put(q_ref, c * tiles + s, rope(qc[:, sl]) * (HEAD_DIM ** -0.5))
            put(k_ref, c * tiles + s, rope(kc[:, sl]))
            put(v_ref, c * tiles + s, vc[:, sl])

    glr = _dot(xb, wlr_ref[...])
    z = _dot(glr.astype(BF16), wup_ref[...]) + bg_ref[...]
    loga = -(jnp.maximum(-z, 0.0) + jnp.log1p(jnp.exp(-jnp.abs(z)))) * (1.0 / GATE_NORMALIZER)
    if not sample:
        row = lax.broadcasted_iota(jnp.int32, (tm, tm), 0)
        col = lax.broadcasted_iota(jnp.int32, (tm, tm), 1)
        tri = ((row // GLA_CHUNK == col // GLA_CHUNK) & (col <= row)).astype(BF16)
        hi, mid, lo = _split3_bf16(loga)
        loga = _dot(tri, hi) + _dot(tri, mid) + _dot(tri, lo)
    qg = _dot(xb, wg_ref[:, 0:GLA_K]) * (GLA_DK ** -0.5)
    kg = _dot(xb, wg_ref[:, GLA_K:2 * GLA_K])
    for s in range(GLA_K // LANES):
        sl = slice(s * LANES, (s + 1) * LANES)
        put(qg_ref, s, qg[:, sl])
        put(kg_ref, s, kg[:, sl])
        put(la_ref, s, loga[:, sl])
    vg_ref[...] = _dot(xb, wg_ref[:, 2 * GLA_K:2 * GLA_K + GLA_V])
    rg_ref[...] = _dot(xb, wg_ref[:, 2 * GLA_K + GLA_V:2 * GLA_K + 2 * GLA_V]).astype(BF16)

    gates_ref[...] = _sigmoid(_dot(xb, wgates_ref[...])).astype(BF16)


def _in_proj(x, rope_tab, weights, *, tm, rope_period_tiles, sample):
    rows = x.shape[0]
    assert not sample or tm == rows == LANES
    wa, wg, wlr, wgates, wup, bg = weights

    def out(width, dtype, transposed=False):
        if transposed and sample:
            return jax.ShapeDtypeStruct((width, rows), dtype), pl.BlockSpec((width, rows), lambda i: (0, 0))
        return jax.ShapeDtypeStruct((rows, width), dtype), pl.BlockSpec((tm, width), lambda i: (i, 0))

    outs = (
        out(A_WIDTH, F32, True),
        out(A_WIDTH, F32, True),
        out(A_WIDTH, F32, True),
        out(GLA_K, F32, True),
        out(GLA_K, F32, True),
        out(GLA_V, F32),
        out(GLA_V, BF16),
        out(GLA_K, F32, True),
        out(2 * D_MODEL, BF16),
    )
    return pl.pallas_call(
        functools.partial(_in_proj_kernel, sample=sample),
        grid=(rows // tm,),
        in_specs=[
            pl.BlockSpec((tm, D_MODEL), lambda i: (i, 0)),
            pl.BlockSpec((tm, 3 * LANES), lambda i: (i % rope_period_tiles, 0)),
            _const_spec(wa.shape), _const_spec(wg.shape), _const_spec(wlr.shape),
            _const_spec(wgates.shape), _const_spec(wup.shape), _const_spec(bg.shape),
        ],
        out_specs=tuple(o[1] for o in outs),
        out_shape=tuple(o[0] for o in outs),
        compiler_params=pltpu.CompilerParams(
            dimension_semantics=("parallel",), vmem_limit_bytes=VMEM_LIMIT),
        name="in_proj_sample" if sample else "in_proj_prompt",
    )(x, rope_tab, wa, wg, wlr, wgates, wup, bg)


def _rope_table(pos):
    half = ROT_DIM // 2
    inv_freq = ROPE_THETA ** (-jnp.arange(0, ROT_DIM, 2, dtype=F32) / ROT_DIM)
    ang = pos.astype(F32)[:, None] * inv_freq[None, :]
    cos, sin = jnp.cos(ang), jnp.sin(ang)
    lane = np.arange(LANES) % HEAD_DIM
    fidx = lane % half
    first = lane < half
    second = (lane >= half) & (lane < ROT_DIM)
    cos_t = jnp.where((first | second)[None, :], cos[:, fidx], 1.0)
    sin_lo = jnp.where(first[None, :], -sin[:, fidx], 0.0)
    sin_hi = jnp.where(second[None, :], sin[:, fidx], 0.0)
    return jnp.concatenate([cos_t, sin_lo, sin_hi], axis=1)


ATT_TILE = 2048
Q_BLOCK = 128
HEADS_PER_TILE = LANES // HEAD_DIM


def _attn_prompt_kernel(q_ref, kc_ref, kp_ref, vc_ref, vp_ref, o_ref, kk, vv, og, lg, bias):
    t = pl.program_id(1)
    g = pl.program_id(3)
    band_keys = 2 * Q_BLOCK
    qi = lax.broadcasted_iota(jnp.int32, (Q_BLOCK, HEADS_PER_TILE * band_keys), 0)
    kc = lax.broadcasted_iota(jnp.int32, (Q_BLOCK, HEADS_PER_TILE * band_keys), 1) % band_keys
    back = qi + Q_BLOCK - kc
    band = (back >= 0) & (back <= KEYS_PER_WINDOW)
    bias[0] = jnp.where(band, 0.0, -jnp.inf)
    bias[1] = jnp.where(band & (kc >= Q_BLOCK), 0.0, -jnp.inf)
    head0 = lax.broadcasted_iota(jnp.int32, (band_keys, LANES), 1) < HEAD_DIM
    head0_q = lax.broadcasted_iota(jnp.int32, (Q_BLOCK, LANES), 1) < HEAD_DIM

    for gi, (window, dil) in enumerate(DILATED_GROUPS):

        @pl.when(g == gi)
        def _(gi=gi, window=window, dil=dil):
            kk[0:window, :] = kp_ref[ATT_TILE - window:ATT_TILE, :]
            kk[window:window + ATT_TILE, :] = kc_ref[...]
            vv[0:window, :] = vp_ref[ATT_TILE - window:ATT_TILE, :]
            vv[window:window + ATT_TILE, :] = vc_ref[...]
            nblk = ATT_TILE // dil // Q_BLOCK

            def rows(start, size):
                if dil == 1:
                    return pl.ds(pl.multiple_of(start, Q_BLOCK), size)
                return pl.ds(start, size, stride=dil)

            def body(idx, carry):
                r = idx // nblk
                n = idx % nblk
                base = r + dil * Q_BLOCK * n
                q2 = q_ref[rows(base, Q_BLOCK), :].astype(BF16)
                k2 = kk[rows(base, band_keys), :]
                v2 = vv[rows(base, band_keys), :]
                kz = jnp.concatenate([jnp.where(head0, k2, 0.0), jnp.where(head0, 0.0, k2)], axis=0)
                no_prev = jnp.logical_and(t == 0, n == 0).astype(jnp.int32)
                s = _dot_nt(q2, kz.astype(BF16)) + bias[no_prev]
                s0, s1 = s[:, :band_keys], s[:, band_keys:]
                m0 = jnp.max(s0, axis=-1, keepdims=True)
                m1 = jnp.max(s1, axis=-1, keepdims=True)
                p = jnp.concatenate([jnp.exp(s0 - m0), jnp.exp(s1 - m1)], axis=1).astype(BF16)
                ones0 = head0.astype(F32)
                vz = jnp.concatenate([
                    jnp.concatenate([jnp.where(head0, v2, 0.0), ones0], axis=1),
                    jnp.concatenate([jnp.where(head0, 0.0, v2), 1.0 - ones0], axis=1)], axis=0)
                res = _dot(p, vz.astype(BF16))
                den = res[:, LANES:]
                og[gi, rows(base, Q_BLOCK), :] = res[:, :LANES] / den
                m = jnp.where(head0_q, jnp.broadcast_to(m0, den.shape), jnp.broadcast_to(m1, den.shape))
                lg[gi, rows(base, Q_BLOCK), :] = m + jnp.log(den)
                return carry

            lax.fori_loop(0, dil * nblk, body, 0, unroll=4)

    @pl.when(g == len(DILATED_GROUPS) - 1)
    def _():
        rows = 256

        def combine(c, carry):
            r0 = pl.multiple_of(c * rows, rows)
            l0 = lg[0, pl.ds(r0, rows), :]
            l1 = lg[1, pl.ds(r0, rows), :]
            l2 = lg[2, pl.ds(r0, rows), :]
            mx = jnp.maximum(jnp.maximum(l0, l1), l2)
            e0, e1, e2 = jnp.exp(l0 - mx), jnp.exp(l1 - mx), jnp.exp(l2 - mx)
            acc = (e0 * og[0, pl.ds(r0, rows), :] + e1 * og[1, pl.ds(r0, rows), :]
                   + e2 * og[2, pl.ds(r0, rows), :])
            o_ref[pl.ds(r0, rows), :] = (acc / (e0 + e1 + e2)).astype(o_ref.dtype)
            return carry

        lax.fori_loop(0, ATT_TILE // rows, combine, 0)


def _attn_prompt(q, k, v):
    b, s, _ = q.shape
    tiles_per_group = GROUP_COLS // LANES
    cur = pl.BlockSpec((None, ATT_TILE, LANES), lambda bi, t, c, g: (bi, t, g * tiles_per_group + c))
    prev = pl.BlockSpec((None, ATT_TILE, LANES),
                        lambda bi, t, c, g: (bi, jnp.maximum(t - 1, 0), g * tiles_per_group + c))
    max_window = max(w for w, _ in DILATED_GROUPS)
    ngroups = len(DILATED_GROUPS)
    return pl.pallas_call(
        _attn_prompt_kernel,
        grid=(b, s // ATT_TILE, tiles_per_group, ngroups),
        in_specs=[cur, cur, prev, cur, prev],
        out_specs=pl.BlockSpec((None, ATT_TILE, LANES), lambda bi, t, c, g: (bi, t, c)),
        out_shape=jax.ShapeDtypeStruct((b, s, GROUP_COLS), BF16),
        scratch_shapes=[
            pltpu.VMEM((max_window + ATT_TILE, LANES), F32),
            pltpu.VMEM((max_window + ATT_TILE, LANES), F32),
            pltpu.VMEM((ngroups, ATT_TILE, LANES), F32),
            pltpu.VMEM((ngroups, ATT_TILE, LANES), F32),
            pltpu.VMEM((2, Q_BLOCK, HEADS_PER_TILE * 2 * Q_BLOCK), F32),
        ],
        compiler_params=pltpu.CompilerParams(
            dimension_semantics=("parallel", "parallel", "parallel", "arbitrary"),
            vmem_limit_bytes=VMEM_LIMIT),
        name="attn_prompt",
    )(q, k, k, v, v)


GLA_TILE = 512


def _gla_epilogue(o, r, g):
    o = o * lax.rsqrt(jnp.mean(o * o, -1, keepdims=True) + RMS_EPS)
    return o * g * _silu(r.astype(F32))


def _gla_prompt_kernel(q_ref, k_ref, v_ref, cum_ref, r_ref, g_ref, o_ref, st_ref, st_t):
    step = pl.program_id(1)
    c = GLA_CHUNK
    nchunks = q_ref.shape[0] // c

    @pl.when(step == 0)
    def _():
        st_t[...] = jnp.zeros_like(st_t)

    ri = lax.broadcasted_iota(jnp.int32, (c, c), 0)
    ci = lax.broadcasted_iota(jnp.int32, (c, c), 1)
    row_id = lax.broadcasted_iota(jnp.int32, (c, GLA_DK), 0)

    def chunk(ch, factored):
        r0 = ch * c if isinstance(ch, int) else pl.multiple_of(ch * c, c)
        for h in range(GLA_HEADS):
            ks = slice(h * GLA_DK, (h + 1) * GLA_DK)
            vs = slice(h * GLA_DV, (h + 1) * GLA_DV)
            q = q_ref[pl.ds(r0, c), ks]
            k = k_ref[pl.ds(r0, c), ks]
            v = v_ref[pl.ds(r0, c), vs].astype(BF16)
            cum = cum_ref[pl.ds(r0, c), ks]
            last = cum[c - 1:c]
            decay = jnp.exp(last)
            qe = (q * jnp.exp(cum)).astype(BF16)
            if factored:
                ke = k * jnp.exp(-cum)
                a = jnp.where(ci <= ri, _dot_nt(qe, ke.astype(BF16)), 0.0)
                kd = ke * decay
            else:
                def col(j, a):
                    cj = jnp.sum(jnp.where(row_id == j, cum, 0.0), axis=0, keepdims=True)
                    kj = jnp.sum(jnp.where(row_id == j, k, 0.0), axis=0, keepdims=True)
                    e = jnp.exp(jnp.where(row_id >= j, cum - cj, -jnp.inf))
                    sj = jnp.sum(q * kj * e, axis=-1, keepdims=True)
                    return jnp.where(ci == j, sj, a)

                a = lax.fori_loop(0, c, col, jnp.zeros((c, c), F32))
                kd = k * jnp.exp(last - cum)
            s_t = st_t[h]
            o = _dot_nt(qe, s_t.astype(BF16)) + _dot(a.astype(BF16), v)
            o_ref[pl.ds(r0, c), vs] = _gla_epilogue(o, r_ref[pl.ds(r0, c), vs], g_ref[:, vs]).astype(o_ref.dtype)
            st_t[h] = s_t * decay + _dot_tn(v, kd.astype(BF16))

    total = -jnp.min(cum_ref[...])

    @pl.when(total < GLA_FACTORED_RANGE)
    def _():
        for ch in range(nchunks):
            chunk(ch, True)

    @pl.when(total >= GLA_FACTORED_RANGE)
    def _():
        def body(ch, carry):
            chunk(ch, False)
            return carry

        lax.fori_loop(0, nchunks, body, 0)

    @pl.when(step == pl.num_programs(1) - 1)
    def _():
        for h in range(GLA_HEADS):
            st_ref[h] = st_t[h].T


def _gla_prompt(qg, kg, vg, cum, rg, norm_g):
    b, s, _ = qg.shape
    kspec = pl.BlockSpec((None, GLA_TILE, GLA_K), lambda bi, t: (bi, t, 0))
    vspec = pl.BlockSpec((None, GLA_TILE, GLA_V), lambda bi, t: (bi, t, 0))
    return pl.pallas_call(
        _gla_prompt_kernel,
        grid=(b, s // GLA_TILE),
        in_specs=[kspec, kspec, vspec, kspec, vspec, _const_spec(norm_g.shape)],
        out_specs=(vspec, pl.BlockSpec((None, GLA_HEADS, GLA_DK, GLA_DV), lambda bi, t: (bi, 0, 0, 0))),
        out_shape=(jax.ShapeDtypeStruct((b, s, GLA_V), BF16),
                   jax.ShapeDtypeStruct((b, GLA_HEADS, GLA_DK, GLA_DV), F32)),
        scratch_shapes=[pltpu.VMEM((GLA_HEADS, GLA_DV, GLA_DK), F32)],
        compiler_params=pltpu.CompilerParams(
            dimension_semantics=("parallel", "arbitrary"), vmem_limit_bytes=VMEM_LIMIT),
        name="gla_prompt",
    )(qg, kg, vg, cum, rg, norm_g)


def _pick_column(ref, n):
    lane = lax.broadcasted_iota(jnp.int32, ref.shape, 1)
    return jnp.sum(jnp.where(lane == n, ref[...], 0.0), axis=1, keepdims=True)


def _attn_sample_step(n, qt_ref, kt_ref, vt_ref, caches, news, o_cols):
    ngroups = len(DILATED_GROUPS)
    qcol = _pick_column(qt_ref, n)
    kcol = _pick_column(kt_ref, n)
    vcol = _pick_column(vt_ref, n)

    def shifted(x, new):
        lb = x.shape[1]
        lane = lax.broadcasted_iota(jnp.int32, x.shape, 1)
        return jnp.where(lane == lb - 1, new, pltpu.roll(x, lb - 1, 1))

    heads = []
    for h in range(HEADS_PER_GROUP):
        s_old, s_new = [], []
        for gi, (window, dil) in enumerate(DILATED_GROUPS):
            lb = caches[gi].shape[1]
            r0 = gi * GROUP_COLS + h * HEAD_DIM
            q1 = qcol[r0:r0 + HEAD_DIM]
            k1 = kcol[r0:r0 + HEAD_DIM]
            keys = caches[gi][h * HEAD_DIM:(h + 1) * HEAD_DIM, :]
            news[gi][h * HEAD_DIM:(h + 1) * HEAD_DIM, :] = shifted(keys, k1)
            s = jnp.sum(keys * q1, axis=0, keepdims=True)
            pos = lax.broadcasted_iota(jnp.int32, (1, lb), 1)
            s_old.append(jnp.where(pos % dil == 0, s, -jnp.inf))
            s_new.append(jnp.sum(k1 * q1, axis=0, keepdims=True))
        m = s_new[0]
        for gi in range(ngroups):
            m = jnp.maximum(m, jnp.maximum(s_new[gi], jnp.max(s_old[gi], axis=1, keepdims=True)))
        den = jnp.zeros((1, 1), F32)
        acc = jnp.zeros((HEAD_DIM, 1), F32)
        for gi in range(ngroups):
            r0 = gi * GROUP_COLS + h * HEAD_DIM
            v1 = vcol[r0:r0 + HEAD_DIM]
            p_old = jnp.exp(s_old[gi] - m)
            p_new = jnp.exp(s_new[gi] - m)
            den = den + jnp.sum(p_old, axis=1, keepdims=True) + p_new
            rows = slice(GROUP_COLS + h * HEAD_DIM, GROUP_COLS + (h + 1) * HEAD_DIM)
            vals = caches[gi][rows, :]
            news[gi][rows, :] = shifted(vals, v1)
            acc = acc + jnp.sum(vals * p_old, axis=1, keepdims=True) + v1 * p_new
        heads.append(acc / den)
    o_col = jnp.concatenate(heads, axis=0)
    lane = lax.broadcasted_iota(jnp.int32, o_cols.shape, 1)
    o_cols[...] = jnp.where(lane == n, o_col, o_cols[...])


def _post_kernel(x_ref, oa_ref, ob_ref, gates_ref, pe_ref,
                 wa_ref, wb_ref, wo_ref, wfg_ref, wfu_ref, wfd_ref, wpg_ref, wpp_ref,
                 ln1g_ref, ln1b_ref, ln2g_ref, ln2b_ref, *rest, sample_attn):
    if sample_attn:
        qt_ref, kt_ref, vt_ref, c1_ref, c2_ref, c3_ref, y_ref, n1_ref, n2_ref, n3_ref, oas_ref, o_cols = rest
        i = pl.program_id(0)

        @pl.when(i == 0)
        def _():
            o_cols[...] = jnp.zeros_like(o_cols)

        _attn_sample_step(i, qt_ref, kt_ref, vt_ref, (c1_ref, c2_ref, c3_ref),
                          (n1_ref, n2_ref, n3_ref), o_cols)
    else:
        (y_ref,) = rest

    ga = gates_ref[:, 0:D_MODEL].astype(F32)
    gb = gates_ref[:, D_MODEL:2 * D_MODEL].astype(F32)
    merged = (ga * _dot(oa_ref[...].astype(BF16), wa_ref[...])
              + gb * _dot(ob_ref[...].astype(BF16), wb_ref[...]))
    x1 = _layernorm(DN_ALPHA * x_ref[...] + _dot(merged.astype(BF16), wo_ref[...]),
                    ln1g_ref[...], ln1b_ref[...])
    x1b = x1.astype(BF16)
    act = _silu(_dot(x1b, wfg_ref[...])) * _dot(x1b, wfu_ref[...])
    x2 = _layernorm(DN_ALPHA * x1 + _dot(act.astype(BF16), wfd_ref[...]), ln2g_ref[...], ln2b_ref[...])
    gate = _sigmoid(_dot(x2.astype(BF16), wpg_ref[...]))
    y_ref[...] = x2 + gate * _dot(pe_ref[...].astype(BF16), wpp_ref[...])

    if sample_attn:
        @pl.when(i == pl.num_programs(0) - 1)
        def _():
            for c in range(GROUP_COLS // LANES):
                oas_ref[:, c * LANES:(c + 1) * LANES] = o_cols[c * LANES:(c + 1) * LANES, :].T


def _post(x, oa, ob, gates, pe, weights, *, tm, sample_attn=None):
    rows = x.shape[0]
    steps = rows // tm
    row_spec = lambda n: pl.BlockSpec((tm, n), lambda i: (i, 0))
    in_specs = [row_spec(D_MODEL), row_spec(GROUP_COLS), row_spec(GLA_V), row_spec(2 * D_MODEL),
                row_spec(PLE_DIM)] + [_const_spec(w.shape) for w in weights]
    out_specs = [row_spec(D_MODEL)]
    out_shape = [jax.ShapeDtypeStruct((rows, D_MODEL), F32)]
    operands = [x, oa, ob, gates, pe, *weights]
    scratch = []
    if sample_attn is not None:
        qt, kt, vt, caches = sample_attn
        n = qt.shape[1]
        assert n == LANES == steps, "one request per grid step"
        for (window, _), c in zip(DILATED_GROUPS, caches):
            assert c.shape[2] == window, "every key distance must fall inside the cache"
        cache_specs = [pl.BlockSpec((None,) + c.shape[1:], lambda i: (i, 0, 0)) for c in caches]
        in_specs += [_const_spec(qt.shape)] * 3 + cache_specs
        out_specs += cache_specs + [pl.BlockSpec((n, GROUP_COLS), lambda i: (0, 0))]
        out_shape += [jax.ShapeDtypeStruct(c.shape, c.dtype) for c in caches]
        out_shape += [jax.ShapeDtypeStruct((n, GROUP_COLS), F32)]
        operands += [qt, kt, vt, *caches]
        scratch = [pltpu.VMEM((GROUP_COLS, n), F32)]
    out = pl.pallas_call(
        functools.partial(_post_kernel, sample_attn=sample_attn is not None),
        grid=(steps,),
        in_specs=in_specs,
        out_specs=tuple(out_specs),
        out_shape=tuple(out_shape),
        scratch_shapes=scratch,
        compiler_params=pltpu.CompilerParams(
            dimension_semantics=("arbitrary" if sample_attn is not None else "parallel",),
            vmem_limit_bytes=VMEM_LIMIT),
        name="post_with_sample_attn" if sample_attn is not None else "post",
    )(*operands)
    return out if sample_attn is not None else out[0]


SAMPLE_BLOCK = 8


def _gla_sample_kernel(qt_ref, kt_ref, lat_ref, v_ref, r_ref, g_ref, st_ref, o_ref, st_out_ref, o_scr):
    nb = v_ref.shape[0]
    first = pl.program_id(0) * nb

    def one(j, carry):
        n = first + j
        a = jnp.exp(_pick_column(lat_ref, n))
        k = _pick_column(kt_ref, n)
        q = _pick_column(qt_ref, n)
        for h in range(GLA_HEADS):
            ks = slice(h * GLA_DK, (h + 1) * GLA_DK)
            vs = slice(h * GLA_DV, (h + 1) * GLA_DV)
            st = a[ks] * st_ref[j, h] + k[ks] * v_ref[pl.ds(j, 1), vs]
            st_out_ref[j, h] = st
            o_scr[pl.ds(j, 1), vs] = jnp.sum(q[ks] * st, axis=0, keepdims=True)
        return carry

    lax.fori_loop(0, nb, one, 0)
    for h in range(GLA_HEADS):
        vs = slice(h * GLA_DV, (h + 1) * GLA_DV)
        o_ref[:, vs] = _gla_epilogue(o_scr[:, vs], r_ref[:, vs], g_ref[:, vs]).astype(o_ref.dtype)


def _gla_sample(qgt, kgt, lat, vg, rg, norm_g, state):
    n = vg.shape[0]
    nb = SAMPLE_BLOCK
    row = lambda w: pl.BlockSpec((nb, w), lambda i: (i, 0))
    st_spec = pl.BlockSpec((nb, GLA_HEADS, GLA_DK, GLA_DV), lambda i: (i, 0, 0, 0))
    return pl.pallas_call(
        _gla_sample_kernel,
        grid=(n // nb,),
        in_specs=[_const_spec(qgt.shape), _const_spec(kgt.shape), _const_spec(lat.shape),
                  row(GLA_V), row(GLA_V), _const_spec(norm_g.shape), st_spec],
        out_specs=(row(GLA_V), st_spec),
        out_shape=(jax.ShapeDtypeStruct((n, GLA_V), BF16), jax.ShapeDtypeStruct(state.shape, F32)),
        scratch_shapes=[pltpu.VMEM((nb, GLA_V), F32)],
        compiler_params=pltpu.CompilerParams(
            dimension_semantics=("parallel",), vmem_limit_bytes=VMEM_LIMIT),
        name="gla_sample",
    )(qgt, kgt, lat, vg, rg, norm_g, state)


PROMPT_TM = 256


def _layer_weights(w_in, w_gate_up, b_gate, w_a_out, w_b_out, w_o, ln1_g, ln1_b,
                   w_ff_gate, w_ff_up, w_ff_down, ln2_g, ln2_b, w_ple_gate, w_ple_proj):
    c_gla = 3 * A_WIDTH
    c_lr = c_gla + 2 * GLA_K + 2 * GLA_V
    c_gates = c_lr + GATE_RANK
    wb = w_in.astype(BF16)
    proj = (wb[:, :c_gla], wb[:, c_gla:c_lr], wb[:, c_lr:c_gates], wb[:, c_gates:],
            w_gate_up.astype(BF16), b_gate.reshape(1, GLA_K))
    row = lambda p: p.reshape(1, -1)
    post = (w_a_out.astype(BF16), w_b_out.astype(BF16), w_o.astype(BF16), w_ff_gate.astype(BF16),
            w_ff_up.astype(BF16), w_ff_down.astype(BF16), w_ple_gate.astype(BF16),
            w_ple_proj.astype(BF16), row(ln1_g), row(ln1_b), row(ln2_g), row(ln2_b))
    return proj, post


def _kv_tail(k, v, gi, keep):
    b, s, _ = k.shape
    cols = slice(gi * GROUP_COLS, (gi + 1) * GROUP_COLS)
    kt = k[:, s - keep:, cols].reshape(b, keep, HEADS_PER_GROUP, HEAD_DIM)
    vt = v[:, s - keep:, cols].reshape(b, keep, HEADS_PER_GROUP, HEAD_DIM)
    return jnp.stack([kt, vt], axis=2)


def _positions_minor(cache):
    n, lb = cache.shape[:2]
    return jnp.transpose(cache, (0, 2, 3, 4, 1)).reshape(n, 2 * GROUP_COLS, lb)


def _positions_major(cache):
    n, _, lb = cache.shape
    return jnp.transpose(cache.reshape(n, 2, HEADS_PER_GROUP, HEAD_DIM, lb), (0, 4, 1, 2, 3))


def kernel(x_prompt, x_sample, cache_a1_kv, cache_a2_kv, cache_a3_kv, state_gla, p_prompt, p_sample, w_in, w_gate_up, b_gate, gla_norm_g, w_a_out, w_b_out, w_o, ln1_g, ln1_b, w_ff_gate, w_ff_up, w_ff_down, ln2_g, ln2_b, w_ple_gate, w_ple_proj):
    depth = w_in.shape[0]
    assert depth == 1 and x_sample.shape[1] == 1
    b, s, d = x_prompt.shape
    n = x_sample.shape[0]
    proj_w, post_w = _layer_weights(
        w_in[0], w_gate_up[0], b_gate[0], w_a_out[0], w_b_out[0], w_o[0], ln1_g[0], ln1_b[0],
        w_ff_gate[0], w_ff_up[0], w_ff_down[0], ln2_g[0], ln2_b[0], w_ple_gate[0], w_ple_proj[0])
    norm_g = gla_norm_g[0].reshape(1, GLA_V)

    xp = x_prompt.reshape(b * s, d)
    rope_p = _rope_table(jnp.arange(s))
    q, k, v, qg, kg, vg, rg, cum, gates = _in_proj(
        xp, rope_p, proj_w, tm=PROMPT_TM, rope_period_tiles=s // PROMPT_TM, sample=False)
    three = lambda t: t.reshape(b, s, t.shape[-1])
    k3, v3 = three(k), three(v)
    oa = _attn_prompt(three(q), k3, v3)
    ob, st_p = _gla_prompt(three(qg), three(kg), three(vg), three(cum), three(rg), norm_g)
    kv_p = [_kv_tail(k3, v3, gi, min(w, s))[None] for gi, (w, _) in enumerate(DILATED_GROUPS)]

    xs = x_sample.reshape(n, d)
    rope_s = _rope_table(jnp.full((n,), PAST_LEN, jnp.int32))
    qt, kt, vt, qgt, kgt, vg_s, rg_s, lat, gates_s = _in_proj(
        xs, rope_s, proj_w, tm=n, rope_period_tiles=1, sample=True)
    caches = [_positions_minor(c[0]) for c in (cache_a1_kv, cache_a2_kv, cache_a3_kv)]
    y_p, *new_caches, oa_s = _post(
        xp, oa.reshape(b * s, GROUP_COLS), ob.reshape(b * s, GLA_V), gates,
        p_prompt[0].reshape(b * s, PLE_DIM), post_w, tm=b * s // n, sample_attn=(qt, kt, vt, caches))

    ob_s, st_s = _gla_sample(qgt, kgt, lat, vg_s, rg_s, norm_g, state_gla[0])
    y_s = _post(xs, oa_s, ob_s, gates_s, p_sample[0].reshape(n, PLE_DIM), post_w, tm=n)
    kv_s = [_positions_major(c)[None] for c in new_caches]

    return (y_p.reshape(b, s, d), y_s.reshape(n, 1, d), kv_p[0], kv_p[1], kv_p[2], st_p[None],
            kv_s[0], kv_s[1], kv_s[2], st_s[None])
```

```python
import functools

import numpy as np
import jax
import jax.numpy as jnp
from jax import lax
from jax.experimental import pallas as pl
from jax.experimental.pallas import tpu as pltpu

F32 = jnp.float32
BF16 = jnp.bfloat16

D_MODEL = 1024
HEAD_DIM = 64
HEADS_PER_GROUP = 4
DILATED_GROUPS = ((128, 1), (512, 4), (2048, 16))
GROUP_COLS = HEADS_PER_GROUP * HEAD_DIM
A_WIDTH = len(DILATED_GROUPS) * GROUP_COLS
ROT_DIM = HEAD_DIM // 4
ROPE_THETA = 500000.0
KEYS_PER_WINDOW = 128
GLA_HEADS = 4
GLA_DK = 128
GLA_DV = 256
GLA_K = GLA_HEADS * GLA_DK
GLA_V = GLA_HEADS * GLA_DV
GATE_RANK = 16
GATE_NORMALIZER = 16.0
GLA_CHUNK = 128
D_FF = 2816
PLE_DIM = 256
DN_ALPHA = 2.0 ** 0.25
LN_EPS = 1e-5
RMS_EPS = 1e-6
PAST_LEN = 8192

LANES = 128
SUBLANES = 8
VMEM_LIMIT = 56 * 1024 * 1024

GLA_FACTORED_RANGE = 40.0


def _dot(a, b):
    return jnp.dot(a, b, preferred_element_type=F32)


def _dot_nt(a, b):
    return lax.dot_general(a, b, (((1,), (1,)), ((), ())), preferred_element_type=F32)


def _dot_tn(a, b):
    return lax.dot_general(a, b, (((0,), (0,)), ((), ())), preferred_element_type=F32)


def _const_spec(shape):
    return pl.BlockSpec(shape, lambda *_: (0,) * len(shape), pipeline_mode=pl.Buffered(1))


def _layernorm(x, g, b):
    xc = x - jnp.mean(x, -1, keepdims=True)
    var = jnp.mean(xc * xc, -1, keepdims=True)
    return xc * lax.rsqrt(var + LN_EPS) * g + b


def _sigmoid(x):
    return 1.0 / (1.0 + jnp.exp(-x))


def _silu(x):
    return x * _sigmoid(x)


def _split3_bf16(x):
    hi = x.astype(BF16)
    r1 = x - hi.astype(F32)
    mid = r1.astype(BF16)
    lo = (r1 - mid.astype(F32)).astype(BF16)
    return hi, mid, lo


def _gla_sample_step(n, j, qt_ref, kt_ref, lat_ref, v_ref, st_ref, st_out_ref, o_scr):
    a = jnp.exp(_pick_column(lat_ref, n))
    k = _pick_column(kt_ref, n)
    q = _pick_column(qt_ref, n)
    for h in range(GLA_HEADS):
        ks = slice(h * GLA_DK, (h + 1) * GLA_DK)
        vs = slice(h * GLA_DV, (h + 1) * GLA_DV)
        st = a[ks] * st_ref[j, h] + k[ks] * v_ref[pl.ds(n, 1), vs]
        st_out_ref[j, h] = st
        o_scr[pl.ds(n, 1), vs] = jnp.sum(q[ks] * st, axis=0, keepdims=True)


def _in_proj_kernel(x_ref, rope_ref, wa_ref, wg_ref, wlr_ref, wgates_ref, wup_ref, bg_ref, *rest,
                    sample, gla_sample):
    if gla_sample:
        (qts_ref, kts_ref, lats_ref, vs_ref, rs_ref, gs_ref, st_ref,
         q_ref, k_ref, v_ref, qg_ref, kg_ref, vg_ref, rg_ref, la_ref, gates_ref,
         obs_ref, st_out_ref, o_scr) = rest
        step = pl.program_id(0)
        per_step = st_ref.shape[0]
        for j in range(per_step):
            _gla_sample_step(step * per_step + j, j, qts_ref, kts_ref, lats_ref, vs_ref,
                             st_ref, st_out_ref, o_scr)
    else:
        q_ref, k_ref, v_ref, qg_ref, kg_ref, vg_ref, rg_ref, la_ref, gates_ref = rest
    tm = x_ref.shape[0]
    xb = x_ref[...].astype(BF16)
    cos = rope_ref[:, 0:LANES]
    sin_lo = rope_ref[:, LANES:2 * LANES]
    sin_hi = rope_ref[:, 2 * LANES:3 * LANES]

    def rope(t):
        return (t * cos + pltpu.roll(t, LANES - ROT_DIM // 2, 1) * sin_lo
                + pltpu.roll(t, ROT_DIM // 2, 1) * sin_hi)

    def put(ref, col, val):
        if sample:
            ref[col * LANES:(col + 1) * LANES, :] = val.T
        else:
            ref[:, col * LANES:(col + 1) * LANES] = val

    tiles = GROUP_COLS // LANES
    for c in range(A_WIDTH // GROUP_COLS):
        lo, hi = c * GROUP_COLS, (c + 1) * GROUP_COLS
        qc = _dot(xb, wa_ref[:, lo:hi])
        kc = _dot(xb, wa_ref[:, A_WIDTH + lo:A_WIDTH + hi])
        vc = _dot(xb, wa_ref[:, 2 * A_WIDTH + lo:2 * A_WIDTH + hi])
        for s in range(tiles):
            sl = slice(s * LANES, (s + 1) * LANES)
            put(q_ref, c * tiles + s, rope(qc[:, sl]) * (HEAD_DIM ** -0.5))
            put(k_ref, c * tiles + s, rope(kc[:, sl]))
            put(v_ref, c * tiles + s, vc[:, sl])

    glr = _dot(xb, wlr_ref[...])
    z = _dot(glr.astype(BF16), wup_ref[...]) + bg_ref[...]
    loga = -(jnp.maximum(-z, 0.0) + jnp.log1p(jnp.exp(-jnp.abs(z)))) * (1.0 / GATE_NORMALIZER)
    if not sample:
        row = lax.broadcasted_iota(jnp.int32, (tm, tm), 0)
        col = lax.broadcasted_iota(jnp.int32, (tm, tm), 1)
        tri = ((row // GLA_CHUNK == col // GLA_CHUNK) & (col <= row)).astype(BF16)
        hi, mid, lo = _split3_bf16(loga)
        loga = _dot(tri, hi) + _dot(tri, mid) + _dot(tri, lo)
    qg = _dot(xb, wg_ref[:, 0:GLA_K]) * (GLA_DK ** -0.5)
    kg = _dot(xb, wg_ref[:, GLA_K:2 * GLA_K])
    for s in range(GLA_K // LANES):
        sl = slice(s * LANES, (s + 1) * LANES)
        put(qg_ref, s, qg[:, sl])
        put(kg_ref, s, kg[:, sl])
        put(la_ref, s, loga[:, sl])
    vg_ref[...] = _dot(xb, wg_ref[:, 2 * GLA_K:2 * GLA_K + GLA_V])
    rg_ref[...] = _dot(xb, wg_ref[:, 2 * GLA_K + GLA_V:2 * GLA_K + 2 * GLA_V]).astype(BF16)

    gates_ref[...] = _sigmoid(_dot(xb, wgates_ref[...])).astype(BF16)

    if gla_sample:
        @pl.when(step == pl.num_programs(0) - 1)
        def _():
            for h in range(GLA_HEADS):
                vs = slice(h * GLA_DV, (h + 1) * GLA_DV)
                obs_ref[:, vs] = _gla_epilogue(o_scr[:, vs], rs_ref[:, vs], gs_ref[:, vs]).astype(obs_ref.dtype)


def _in_proj(x, rope_tab, weights, *, tm, rope_period_tiles, sample, gla_sample=None):
    rows = x.shape[0]
    steps = rows // tm
    assert not sample or tm == rows == LANES
    wa, wg, wlr, wgates, wup, bg = weights

    def out(width, dtype, transposed=False):
        if transposed and sample:
            return jax.ShapeDtypeStruct((width, rows), dtype), pl.BlockSpec((width, rows), lambda i: (0, 0))
        return jax.ShapeDtypeStruct((rows, width), dtype), pl.BlockSpec((tm, width), lambda i: (i, 0))

    outs = [
        out(A_WIDTH, F32, True),
        out(A_WIDTH, F32, True),
        out(A_WIDTH, F32, True),
        out(GLA_K, F32, True),
        out(GLA_K, F32, True),
        out(GLA_V, F32),
        out(GLA_V, BF16),
        out(GLA_K, F32, True),
        out(2 * D_MODEL, BF16),
    ]
    in_specs = [
        pl.BlockSpec((tm, D_MODEL), lambda i: (i, 0)),
        pl.BlockSpec((tm, 3 * LANES), lambda i: (i % rope_period_tiles, 0)),
        _const_spec(wa.shape), _const_spec(wg.shape), _const_spec(wlr.shape),
        _const_spec(wgates.shape), _const_spec(wup.shape), _const_spec(bg.shape),
    ]
    operands = [x, rope_tab, wa, wg, wlr, wgates, wup, bg]
    scratch = []
    if gla_sample is not None:
        qgt, kgt, lat, vg, rg, norm_g, state = gla_sample
        n = state.shape[0]
        assert n % steps == 0, "the same number of requests per grid step"
        st_spec = pl.BlockSpec((n // steps,) + state.shape[1:], lambda i: (i, 0, 0, 0))
        in_specs += [_const_spec(t.shape) for t in (qgt, kgt, lat, vg, rg, norm_g)] + [st_spec]
        operands += [qgt, kgt, lat, vg, rg, norm_g, state]
        outs += [(jax.ShapeDtypeStruct((n, GLA_V), BF16), pl.BlockSpec((n, GLA_V), lambda i: (0, 0))),
                 (jax.ShapeDtypeStruct(state.shape, F32), st_spec)]
        scratch = [pltpu.VMEM((n, GLA_V), F32)]
    return pl.pallas_call(
        functools.partial(_in_proj_kernel, sample=sample, gla_sample=gla_sample is not None),
        grid=(steps,),
        in_specs=in_specs,
        out_specs=tuple(o[1] for o in outs),
        out_shape=tuple(o[0] for o in outs),
        scratch_shapes=scratch,
        compiler_params=pltpu.CompilerParams(
            dimension_semantics=("arbitrary" if gla_sample is not None else "parallel",),
            vmem_limit_bytes=VMEM_LIMIT),
        name="in_proj_sample" if sample else "in_proj_prompt",
    )(*operands)


def _rope_table(pos):
    half = ROT_DIM // 2
    inv_freq = ROPE_THETA ** (-jnp.arange(0, ROT_DIM, 2, dtype=F32) / ROT_DIM)
    ang = pos.astype(F32)[:, None] * inv_freq[None, :]
    cos, sin = jnp.cos(ang), jnp.sin(ang)
    lane = np.arange(LANES) % HEAD_DIM
    fidx = lane % half
    first = lane < half
    second = (lane >= half) & (lane < ROT_DIM)
    cos_t = jnp.where((first | second)[None, :], cos[:, fidx], 1.0)
    sin_lo = jnp.where(first[None, :], -sin[:, fidx], 0.0)
    sin_hi = jnp.where(second[None, :], sin[:, fidx], 0.0)
    return jnp.concatenate([cos_t, sin_lo, sin_hi], axis=1)


ATT_TILE = 2048
Q_BLOCK = 128
HEADS_PER_TILE = LANES // HEAD_DIM


def _attn_prompt_kernel(q_ref, kc_ref, kp_ref, vc_ref, vp_ref, o_ref, kk, vv, og, lg, bias):
    t = pl.program_id(1)
    g = pl.program_id(3)
    band_keys = 2 * Q_BLOCK
    qi = lax.broadcasted_iota(jnp.int32, (Q_BLOCK, HEADS_PER_TILE * band_keys), 0)
    kc = lax.broadcasted_iota(jnp.int32, (Q_BLOCK, HEADS_PER_TILE * band_keys), 1) % band_keys
    back = qi + Q_BLOCK - kc
    band = (back >= 0) & (back <= KEYS_PER_WINDOW)
    bias[0] = jnp.where(band, 0.0, -jnp.inf)
    bias[1] = jnp.where(band & (kc >= Q_BLOCK), 0.0, -jnp.inf)
    head0 = lax.broadcasted_iota(jnp.int32, (band_keys, LANES), 1) < HEAD_DIM
    head0_q = lax.broadcasted_iota(jnp.int32, (Q_BLOCK, LANES), 1) < HEAD_DIM

    for gi, (window, dil) in enumerate(DILATED_GROUPS):

        @pl.when(g == gi)
        def _(gi=gi, window=window, dil=dil):
            kk[0:window, :] = kp_ref[ATT_TILE - window:ATT_TILE, :]
            kk[window:window + ATT_TILE, :] = kc_ref[...]
            vv[0:window, :] = vp_ref[ATT_TILE - window:ATT_TILE, :]
            vv[window:window + ATT_TILE, :] = vc_ref[...]
            nblk = ATT_TILE // dil // Q_BLOCK

            def rows(start, size):
                if dil == 1:
                    return pl.ds(pl.multiple_of(start, Q_BLOCK), size)
                return pl.ds(start, size, stride=dil)

            def body(idx, carry):
                r = idx // nblk
                n = idx % nblk
                base = r + dil * Q_BLOCK * n
                q2 = q_ref[rows(base, Q_BLOCK), :].astype(BF16)
                k2 = kk[rows(base, band_keys), :]
                v2 = vv[rows(base, band_keys), :]
                kz = jnp.concatenate([jnp.where(head0, k2, 0.0), jnp.where(head0, 0.0, k2)], axis=0)
                no_prev = jnp.logical_and(t == 0, n == 0).astype(jnp.int32)
                s = _dot_nt(q2, kz.astype(BF16)) + bias[no_prev]
                s0, s1 = s[:, :band_keys], s[:, band_keys:]
                m0 = jnp.max(s0, axis=-1, keepdims=True)
                m1 = jnp.max(s1, axis=-1, keepdims=True)
                p = jnp.concatenate([jnp.exp(s0 - m0), jnp.exp(s1 - m1)], axis=1).astype(BF16)
                ones0 = head0.astype(F32)
                vz = jnp.concatenate([
                    jnp.concatenate([jnp.where(head0, v2, 0.0), ones0], axis=1),
                    jnp.concatenate([jnp.where(head0, 0.0, v2), 1.0 - ones0], axis=1)], axis=0)
                res = _dot(p, vz.astype(BF16))
                den = res[:, LANES:]
                og[gi, rows(base, Q_BLOCK), :] = res[:, :LANES] / den
                m = jnp.where(head0_q, jnp.broadcast_to(m0, den.shape), jnp.broadcast_to(m1, den.shape))
                lg[gi, rows(base, Q_BLOCK), :] = m + jnp.log(den)
                return carry

            lax.fori_loop(0, dil * nblk, body, 0, unroll=8)

    @pl.when(g == len(DILATED_GROUPS) - 1)
    def _():
        rows = 256

        def combine(c, carry):
            r0 = pl.multiple_of(c * rows, rows)
            l0 = lg[0, pl.ds(r0, rows), :]
            l1 = lg[1, pl.ds(r0, rows), :]
            l2 = lg[2, pl.ds(r0, rows), :]
            mx = jnp.maximum(jnp.maximum(l0, l1), l2)
            e0, e1, e2 = jnp.exp(l0 - mx), jnp.exp(l1 - mx), jnp.exp(l2 - mx)
            acc = (e0 * og[0, pl.ds(r0, rows), :] + e1 * og[1, pl.ds(r0, rows), :]
                   + e2 * og[2, pl.ds(r0, rows), :])
            o_ref[pl.ds(r0, rows), :] = (acc / (e0 + e1 + e2)).astype(o_ref.dtype)
            return carry

        lax.fori_loop(0, ATT_TILE // rows, combine, 0)


def _attn_prompt(q, k, v):
    b, s, _ = q.shape
    tiles_per_group = GROUP_COLS // LANES
    cur = pl.BlockSpec((None, ATT_TILE, LANES), lambda bi, t, c, g: (bi, t, g * tiles_per_group + c))
    prev = pl.BlockSpec((None, ATT_TILE, LANES),
                        lambda bi, t, c, g: (bi, jnp.maximum(t - 1, 0), g * tiles_per_group + c))
    max_window = max(w for w, _ in DILATED_GROUPS)
    ngroups = len(DILATED_GROUPS)
    return pl.pallas_call(
        _attn_prompt_kernel,
        grid=(b, s // ATT_TILE, tiles_per_group, ngroups),
        in_specs=[cur, cur, prev, cur, prev],
        out_specs=pl.BlockSpec((None, ATT_TILE, LANES), lambda bi, t, c, g: (bi, t, c)),
        out_shape=jax.ShapeDtypeStruct((b, s, GROUP_COLS), BF16),
        scratch_shapes=[
            pltpu.VMEM((max_window + ATT_TILE, LANES), F32),
            pltpu.VMEM((max_window + ATT_TILE, LANES), F32),
            pltpu.VMEM((ngroups, ATT_TILE, LANES), F32),
            pltpu.VMEM((ngroups, ATT_TILE, LANES), F32),
            pltpu.VMEM((2, Q_BLOCK, HEADS_PER_TILE * 2 * Q_BLOCK), F32),
        ],
        compiler_params=pltpu.CompilerParams(
            dimension_semantics=("parallel", "parallel", "parallel", "arbitrary"),
            vmem_limit_bytes=VMEM_LIMIT),
        name="attn_prompt",
    )(q, k, k, v, v)


GLA_TILE = 512


def _gla_epilogue(o, r, g):
    o = o * lax.rsqrt(jnp.mean(o * o, -1, keepdims=True) + RMS_EPS)
    return o * g * _silu(r.astype(F32))


def _gla_prompt_kernel(q_ref, k_ref, v_ref, cum_ref, r_ref, g_ref, o_ref, st_ref, st_t):
    step = pl.program_id(1)
    c = GLA_CHUNK
    nchunks = q_ref.shape[0] // c

    @pl.when(step == 0)
    def _():
        st_t[...] = jnp.zeros_like(st_t)

    ri = lax.broadcasted_iota(jnp.int32, (c, c), 0)
    ci = lax.broadcasted_iota(jnp.int32, (c, c), 1)
    row_id = lax.broadcasted_iota(jnp.int32, (c, GLA_DK), 0)

    def chunk(ch, factored):
        r0 = ch * c if isinstance(ch, int) else pl.multiple_of(ch * c, c)
        for h in range(GLA_HEADS):
            ks = slice(h * GLA_DK, (h + 1) * GLA_DK)
            vs = slice(h * GLA_DV, (h + 1) * GLA_DV)
            q = q_ref[pl.ds(r0, c), ks]
            k = k_ref[pl.ds(r0, c), ks]
            v = v_ref[pl.ds(r0, c), vs].astype(BF16)
            cum = cum_ref[pl.ds(r0, c), ks]
            last = cum[c - 1:c]
            decay = jnp.exp(last)
            qe = (q * jnp.exp(cum)).astype(BF16)
            if factored:
                ke = k * jnp.exp(-cum)
                a = jnp.where(ci <= ri, _dot_nt(qe, ke.astype(BF16)), 0.0)
                kd = ke * decay
            else:
                def col(j, a):
                    cj = jnp.sum(jnp.where(row_id == j, cum, 0.0), axis=0, keepdims=True)
                    kj = jnp.sum(jnp.where(row_id == j, k, 0.0), axis=0, keepdims=True)
                    e = jnp.exp(jnp.where(row_id >= j, cum - cj, -jnp.inf))
                    sj = jnp.sum(q * kj * e, axis=-1, keepdims=True)
                    return jnp.where(ci == j, sj, a)

                a = lax.fori_loop(0, c, col, jnp.zeros((c, c), F32))
                kd = k * jnp.exp(last - cum)
            s_t = st_t[h]
            o = _dot_nt(qe, s_t.astype(BF16)) + _dot(a.astype(BF16), v)
            o_ref[pl.ds(r0, c), vs] = _gla_epilogue(o, r_ref[pl.ds(r0, c), vs], g_ref[:, vs]).astype(o_ref.dtype)
            st_t[h] = s_t * decay + _dot_tn(v, kd.astype(BF16))

    total = -jnp.min(cum_ref[...])

    @pl.when(total < GLA_FACTORED_RANGE)
    def _():
        for ch in range(nchunks):
            chunk(ch, True)

    @pl.when(total >= GLA_FACTORED_RANGE)
    def _():
        def body(ch, carry):
            chunk(ch, False)
            return carry

        lax.fori_loop(0, nchunks, body, 0)

    @pl.when(step == pl.num_programs(1) - 1)
    def _():
        for h in range(GLA_HEADS):
            st_ref[h] = st_t[h].T


def _gla_prompt(qg, kg, vg, cum, rg, norm_g):
    b, s, _ = qg.shape
    kspec = pl.BlockSpec((None, GLA_TILE, GLA_K), lambda bi, t: (bi, t, 0))
    vspec = pl.BlockSpec((None, GLA_TILE, GLA_V), lambda bi, t: (bi, t, 0))
    return pl.pallas_call(
        _gla_prompt_kernel,
        grid=(b, s // GLA_TILE),
        in_specs=[kspec, kspec, vspec, kspec, vspec, _const_spec(norm_g.shape)],
        out_specs=(vspec, pl.BlockSpec((None, GLA_HEADS, GLA_DK, GLA_DV), lambda bi, t: (bi, 0, 0, 0))),
        out_shape=(jax.ShapeDtypeStruct((b, s, GLA_V), BF16),
                   jax.ShapeDtypeStruct((b, GLA_HEADS, GLA_DK, GLA_DV), F32)),
        scratch_shapes=[pltpu.VMEM((GLA_HEADS, GLA_DV, GLA_DK), F32)],
        compiler_params=pltpu.CompilerParams(
            dimension_semantics=("parallel", "arbitrary"), vmem_limit_bytes=VMEM_LIMIT),
        name="gla_prompt",
    )(qg, kg, vg, cum, rg, norm_g)


def _pick_column(ref, n):
    lane = lax.broadcasted_iota(jnp.int32, ref.shape, 1)
    return jnp.sum(jnp.where(lane == n, ref[...], 0.0), axis=1, keepdims=True)


def _attn_sample_step(n, qt_ref, kt_ref, vt_ref, caches, news, o_cols):
    ngroups = len(DILATED_GROUPS)
    qcol = _pick_column(qt_ref, n)
    kcol = _pick_column(kt_ref, n)
    vcol = _pick_column(vt_ref, n)

    def shifted(x, new):
        lb = x.shape[1]
        lane = lax.broadcasted_iota(jnp.int32, x.shape, 1)
        return jnp.where(lane == lb - 1, new, pltpu.roll(x, lb - 1, 1))

    heads = []
    for h in range(HEADS_PER_GROUP):
        s_old, s_new = [], []
        for gi, (window, dil) in enumerate(DILATED_GROUPS):
            lb = caches[gi].shape[1]
            r0 = gi * GROUP_COLS + h * HEAD_DIM
            q1 = qcol[r0:r0 + HEAD_DIM]
            k1 = kcol[r0:r0 + HEAD_DIM]
            keys = caches[gi][h * HEAD_DIM:(h + 1) * HEAD_DIM, :]
            news[gi][h * HEAD_DIM:(h + 1) * HEAD_DIM, :] = shifted(keys, k1)
            s = jnp.sum(keys * q1, axis=0, keepdims=True)
            pos = lax.broadcasted_iota(jnp.int32, (1, lb), 1)
            s_old.append(jnp.where(pos % dil == 0, s, -jnp.inf))
            s_new.append(jnp.sum(k1 * q1, axis=0, keepdims=True))
        m = s_new[0]
        for gi in range(ngroups):
            m = jnp.maximum(m, jnp.maximum(s_new[gi], jnp.max(s_old[gi], axis=1, keepdims=True)))
        den = jnp.zeros((1, 1), F32)
        acc = jnp.zeros((HEAD_DIM, 1), F32)
        for gi in range(ngroups):
            r0 = gi * GROUP_COLS + h * HEAD_DIM
            v1 = vcol[r0:r0 + HEAD_DIM]
            p_old = jnp.exp(s_old[gi] - m)
            p_new = jnp.exp(s_new[gi] - m)
            den = den + jnp.sum(p_old, axis=1, keepdims=True) + p_new
            rows = slice(GROUP_COLS + h * HEAD_DIM, GROUP_COLS + (h + 1) * HEAD_DIM)
            vals = caches[gi][rows, :]
            news[gi][rows, :] = shifted(vals, v1)
            acc = acc + jnp.sum(vals * p_old, axis=1, keepdims=True) + v1 * p_new
        heads.append(acc / den)
    o_col = jnp.concatenate(heads, axis=0)
    lane = lax.broadcasted_iota(jnp.int32, o_cols.shape, 1)
    o_cols[...] = jnp.where(lane == n, o_col, o_cols[...])


def _post_kernel(x_ref, oa_ref, ob_ref, gates_ref, pe_ref,
                 wa_ref, wb_ref, wo_ref, wfg_ref, wfu_ref, wfd_ref, wpg_ref, wpp_ref,
                 ln1g_ref, ln1b_ref, ln2g_ref, ln2b_ref, *rest, sample_attn):
    if sample_attn:
        qt_ref, kt_ref, vt_ref, c1_ref, c2_ref, c3_ref, y_ref, n1_ref, n2_ref, n3_ref, oas_ref, o_cols = rest
        i = pl.program_id(0)

        @pl.when(i == 0)
        def _():
            o_cols[...] = jnp.zeros_like(o_cols)

        _attn_sample_step(i, qt_ref, kt_ref, vt_ref, (c1_ref, c2_ref, c3_ref),
                          (n1_ref, n2_ref, n3_ref), o_cols)
    else:
        (y_ref,) = rest

    ga = gates_ref[:, 0:D_MODEL].astype(F32)
    gb = gates_ref[:, D_MODEL:2 * D_MODEL].astype(F32)
    merged = (ga * _dot(oa_ref[...].astype(BF16), wa_ref[...])
              + gb * _dot(ob_ref[...].astype(BF16), wb_ref[...]))
    x1 = _layernorm(DN_ALPHA * x_ref[...] + _dot(merged.astype(BF16), wo_ref[...]),
                    ln1g_ref[...], ln1b_ref[...])
    x1b = x1.astype(BF16)
    act = _silu(_dot(x1b, wfg_ref[...])) * _dot(x1b, wfu_ref[...])
    x2 = _layernorm(DN_ALPHA * x1 + _dot(act.astype(BF16), wfd_ref[...]), ln2g_ref[...], ln2b_ref[...])
    gate = _sigmoid(_dot(x2.astype(BF16), wpg_ref[...]))
    y_ref[...] = x2 + gate * _dot(pe_ref[...].astype(BF16), wpp_ref[...])

    if sample_attn:
        @pl.when(i == pl.num_programs(0) - 1)
        def _():
            for c in range(GROUP_COLS // LANES):
                oas_ref[:, c * LANES:(c + 1) * LANES] = o_cols[c * LANES:(c + 1) * LANES, :].T


def _post(x, oa, ob, gates, pe, weights, *, tm, sample_attn=None):
    rows = x.shape[0]
    steps = rows // tm
    row_spec = lambda n: pl.BlockSpec((tm, n), lambda i: (i, 0))
    in_specs = [row_spec(D_MODEL), row_spec(GROUP_COLS), row_spec(GLA_V), row_spec(2 * D_MODEL),
                row_spec(PLE_DIM)] + [_const_spec(w.shape) for w in weights]
    out_specs = [row_spec(D_MODEL)]
    out_shape = [jax.ShapeDtypeStruct((rows, D_MODEL), F32)]
    operands = [x, oa, ob, gates, pe, *weights]
    scratch = []
    if sample_attn is not None:
        qt, kt, vt, caches = sample_attn
        n = qt.shape[1]
        assert n == LANES == steps, "one request per grid step"
        for (window, _), c in zip(DILATED_GROUPS, caches):
            assert c.shape[2] == window, "every key distance must fall inside the cache"
        cache_specs = [pl.BlockSpec((None,) + c.shape[1:], lambda i: (i, 0, 0)) for c in caches]
        in_specs += [_const_spec(qt.shape)] * 3 + cache_specs
        out_specs += cache_specs + [pl.BlockSpec((n, GROUP_COLS), lambda i: (0, 0))]
        out_shape += [jax.ShapeDtypeStruct(c.shape, c.dtype) for c in caches]
        out_shape += [jax.ShapeDtypeStruct((n, GROUP_COLS), F32)]
        operands += [qt, kt, vt, *caches]
        scratch = [pltpu.VMEM((GROUP_COLS, n), F32)]
    out = pl.pallas_call(
        functools.partial(_post_kernel, sample_attn=sample_attn is not None),
        grid=(steps,),
        in_specs=in_specs,
        out_specs=tuple(out_specs),
        out_shape=tuple(out_shape),
        scratch_shapes=scratch,
        compiler_params=pltpu.CompilerParams(
            dimension_semantics=("arbitrary" if sample_attn is not None else "parallel",),
            vmem_limit_bytes=VMEM_LIMIT),
        name="post_with_sample_attn" if sample_attn is not None else "post",
    )(*operands)
    return out if sample_attn is not None else out[0]


PROMPT_TM = 256


def _layer_weights(w_in, w_gate_up, b_gate, w_a_out, w_b_out, w_o, ln1_g, ln1_b,
                   w_ff_gate, w_ff_up, w_ff_down, ln2_g, ln2_b, w_ple_gate, w_ple_proj):
    c_gla = 3 * A_WIDTH
    c_lr = c_gla + 2 * GLA_K + 2 * GLA_V
    c_gates = c_lr + GATE_RANK
    cast = lambda w: w.astype(BF16)
    proj = (cast(w_in[:, :c_gla]), cast(w_in[:, c_gla:c_lr]), cast(w_in[:, c_lr:c_gates]),
            cast(w_in[:, c_gates:]), cast(w_gate_up), b_gate.reshape(1, GLA_K))
    row = lambda p: p.reshape(1, -1)
    post = (w_a_out.astype(BF16), w_b_out.astype(BF16), w_o.astype(BF16), w_ff_gate.astype(BF16),
            w_ff_up.astype(BF16), w_ff_down.astype(BF16), w_ple_gate.astype(BF16),
            w_ple_proj.astype(BF16), row(ln1_g), row(ln1_b), row(ln2_g), row(ln2_b))
    return proj, post


def _kv_tail(k, v, gi, keep):
    b, s, _ = k.shape
    cols = slice(gi * GROUP_COLS, (gi + 1) * GROUP_COLS)
    kt = k[:, s - keep:, cols].reshape(b, keep, HEADS_PER_GROUP, HEAD_DIM)
    vt = v[:, s - keep:, cols].reshape(b, keep, HEADS_PER_GROUP, HEAD_DIM)
    return jnp.stack([kt, vt], axis=2)


def _positions_minor(cache):
    n, lb = cache.shape[:2]
    return jnp.transpose(cache, (0, 2, 3, 4, 1)).reshape(n, 2 * GROUP_COLS, lb)


def _positions_major(cache):
    n, _, lb = cache.shape
    return jnp.transpose(cache.reshape(n, 2, HEADS_PER_GROUP, HEAD_DIM, lb), (0, 4, 1, 2, 3))


def kernel(x_prompt, x_sample, cache_a1_kv, cache_a2_kv, cache_a3_kv, state_gla, p_prompt, p_sample, w_in, w_gate_up, b_gate, gla_norm_g, w_a_out, w_b_out, w_o, ln1_g, ln1_b, w_ff_gate, w_ff_up, w_ff_down, ln2_g, ln2_b, w_ple_gate, w_ple_proj):
    depth = w_in.shape[0]
    assert depth == 1 and x_sample.shape[1] == 1
    b, s, d = x_prompt.shape
    n = x_sample.shape[0]
    proj_w, post_w = _layer_weights(
        w_in[0], w_gate_up[0], b_gate[0], w_a_out[0], w_b_out[0], w_o[0], ln1_g[0], ln1_b[0],
        w_ff_gate[0], w_ff_up[0], w_ff_down[0], ln2_g[0], ln2_b[0], w_ple_gate[0], w_ple_proj[0])
    norm_g = gla_norm_g[0].reshape(1, GLA_V)

    xs = x_sample.reshape(n, d)
    rope_s = _rope_table(jnp.full((n,), PAST_LEN, jnp.int32))
    qt, kt, vt, qgt, kgt, vg_s, rg_s, lat, gates_s = _in_proj(
        xs, rope_s, proj_w, tm=n, rope_period_tiles=1, sample=True)

    xp = x_prompt.reshape(b * s, d)
    rope_p = _rope_table(jnp.arange(s))
    q, k, v, qg, kg, vg, rg, cum, gates, ob_s, st_s = _in_proj(
        xp, rope_p, proj_w, tm=PROMPT_TM, rope_period_tiles=s // PROMPT_TM, sample=False,
        gla_sample=(qgt, kgt, lat, vg_s, rg_s, norm_g, state_gla[0]))
    three = lambda t: t.reshape(b, s, t.shape[-1])
    k3, v3 = three(k), three(v)
    oa = _attn_prompt(three(q), k3, v3)
    ob, st_p = _gla_prompt(three(qg), three(kg), three(vg), three(cum), three(rg), norm_g)
    kv_p = [_kv_tail(k3, v3, gi, min(w, s))[None] for gi, (w, _) in enumerate(DILATED_GROUPS)]

    caches = [_positions_minor(c[0]) for c in (cache_a1_kv, cache_a2_kv, cache_a3_kv)]
    y_p, *new_caches, oa_s = _post(
        xp, oa.reshape(b * s, GROUP_COLS), ob.reshape(b * s, GLA_V), gates,
        p_prompt[0].reshape(b * s, PLE_DIM), post_w, tm=b * s // n, sample_attn=(qt, kt, vt, caches))
    y_s = _post(xs, oa_s, ob_s, gates_s, p_sample[0].reshape(n, PLE_DIM), post_w, tm=n)
    kv_s = [_positions_major(c)[None] for c in new_caches]

    return (y_p.reshape(b, s, d), y_s.reshape(n, 1, d), kv_p[0], kv_p[1], kv_p[2], st_p[None],
            kv_s[0], kv_s[1], kv_s[2], st_s[None])
```

```python
import functools

import numpy as np
import jax
import jax.numpy as jnp
from jax import lax
from jax.experimental import pallas as pl
from jax.experimental.pallas import tpu as pltpu

F32 = jnp.float32
BF16 = jnp.bfloat16

D_MODEL = 1024
HEAD_DIM = 64
HEADS_PER_GROUP = 4
DILATED_GROUPS = ((128, 1), (512, 4), (2048, 16))
GROUP_COLS = HEADS_PER_GROUP * HEAD_DIM
A_WIDTH = len(DILATED_GROUPS) * GROUP_COLS
ROT_DIM = HEAD_DIM // 4
ROPE_THETA = 500000.0
KEYS_PER_WINDOW = 128
GLA_HEADS = 4
GLA_DK = 128
GLA_DV = 256
GLA_K = GLA_HEADS * GLA_DK
GLA_V = GLA_HEADS * GLA_DV
GATE_RANK = 16
GATE_NORMALIZER = 16.0
GLA_CHUNK = 128
D_FF = 2816
PLE_DIM = 256
DN_ALPHA = 2.0 ** 0.25
LN_EPS = 1e-5
RMS_EPS = 1e-6
PAST_LEN = 8192
COL_GLA = 3 * A_WIDTH
COL_LOWRANK = COL_GLA + 2 * GLA_K + 2 * GLA_V
COL_GATES = COL_LOWRANK + GATE_RANK
IN_COLS = COL_GATES + 2 * D_MODEL

LANES = 128
SUBLANES = 8
VMEM_LIMIT = 56 * 1024 * 1024

GLA_FACTORED_RANGE = 40.0


def _dot(a, b):
    return jnp.dot(a, b, preferred_element_type=F32)


def _dot_nt(a, b):
    return lax.dot_general(a, b, (((1,), (1,)), ((), ())), preferred_element_type=F32)


def _dot_tn(a, b):
    return lax.dot_general(a, b, (((0,), (0,)), ((), ())), preferred_element_type=F32)


def _const_spec(shape):
    return pl.BlockSpec(shape, lambda *_: (0,) * len(shape), pipeline_mode=pl.Buffered(1))


def _layernorm(x, g, b):
    xc = x - jnp.mean(x, -1, keepdims=True)
    var = jnp.mean(xc * xc, -1, keepdims=True)
    return xc * lax.rsqrt(var + LN_EPS) * g + b


def _sigmoid(x):
    return 1.0 / (1.0 + jnp.exp(-x))


def _silu(x):
    return x * _sigmoid(x)


def _split3_bf16(x):
    hi = x.astype(BF16)
    r1 = x - hi.astype(F32)
    mid = r1.astype(BF16)
    lo = (r1 - mid.astype(F32)).astype(BF16)
    return hi, mid, lo


def _gla_sample_step(n, j, qt_ref, kt_ref, lat_ref, v_ref, st_ref, st_out_ref, o_scr):
    a = jnp.exp(_pick_column(lat_ref, n))
    k = _pick_column(kt_ref, n)
    q = _pick_column(qt_ref, n)
    for h in range(GLA_HEADS):
        ks = slice(h * GLA_DK, (h + 1) * GLA_DK)
        vs = slice(h * GLA_DV, (h + 1) * GLA_DV)
        st = a[ks] * st_ref[j, h] + k[ks] * v_ref[pl.ds(n, 1), vs]
        st_out_ref[j, h] = st
        o_scr[pl.ds(n, 1), vs] = jnp.sum(q[ks] * st, axis=0, keepdims=True)


def _in_proj_kernel(x_ref, rope_ref, w_ref, wup_ref, bg_ref, *rest,
                    sample, gla_sample):
    if gla_sample:
        (qts_ref, kts_ref, lats_ref, vs_ref, rs_ref, gs_ref, st_ref,
         q_ref, k_ref, v_ref, qg_ref, kg_ref, vg_ref, rg_ref, la_ref, gates_ref,
         obs_ref, st_out_ref, o_scr) = rest
        step = pl.program_id(0)
        per_step = st_ref.shape[0]
        for j in range(per_step):
            _gla_sample_step(step * per_step + j, j, qts_ref, kts_ref, lats_ref, vs_ref,
                             st_ref, st_out_ref, o_scr)
    else:
        q_ref, k_ref, v_ref, qg_ref, kg_ref, vg_ref, rg_ref, la_ref, gates_ref = rest
    tm = x_ref.shape[0]
    wa_ref = w_ref.at[0:COL_GLA]
    wg_ref = w_ref.at[COL_GLA:COL_LOWRANK]
    wlr_ref = w_ref.at[COL_LOWRANK:COL_GATES]
    wgates_ref = w_ref.at[COL_GATES:IN_COLS]
    xb = x_ref[...].astype(BF16)
    cos = rope_ref[:, 0:LANES]
    sin_lo = rope_ref[:, LANES:2 * LANES]
    sin_hi = rope_ref[:, 2 * LANES:3 * LANES]

    def rope(t):
        return (t * cos + pltpu.roll(t, LANES - ROT_DIM // 2, 1) * sin_lo
                + pltpu.roll(t, ROT_DIM // 2, 1) * sin_hi)

    def put(ref, col, val):
        if sample:
            ref[col * LANES:(col + 1) * LANES, :] = val.T
        else:
            ref[:, col * LANES:(col + 1) * LANES] = val

    tiles = GROUP_COLS // LANES
    for c in range(A_WIDTH // GROUP_COLS):
        lo, hi = c * GROUP_COLS, (c + 1) * GROUP_COLS
        qc = _dot_nt(xb, wa_ref[lo:hi, :])
        kc = _dot_nt(xb, wa_ref[A_WIDTH + lo:A_WIDTH + hi, :])
        vc = _dot_nt(xb, wa_ref[2 * A_WIDTH + lo:2 * A_WIDTH + hi, :])
        for s in range(tiles):
            sl = slice(s * LANES, (s + 1) * LANES)
            put(q_ref, c * tiles + s, rope(qc[:, sl]) * (HEAD_DIM ** -0.5))
            put(k_ref, c * tiles + s, rope(kc[:, sl]))
            put(v_ref, c * tiles + s, vc[:, sl])

    glr = _dot_nt(xb, wlr_ref[...])
    z = _dot(glr.astype(BF16), wup_ref[...]) + bg_ref[...]
    loga = -(jnp.maximum(-z, 0.0) + jnp.log1p(jnp.exp(-jnp.abs(z)))) * (1.0 / GATE_NORMALIZER)
    if not sample:
        row = lax.broadcasted_iota(jnp.int32, (tm, tm), 0)
        col = lax.broadcasted_iota(jnp.int32, (tm, tm), 1)
        tri = ((row // GLA_CHUNK == col // GLA_CHUNK) & (col <= row)).astype(BF16)
        hi, mid, lo = _split3_bf16(loga)
        loga = _dot(tri, hi) + _dot(tri, mid) + _dot(tri, lo)
    qg = _dot_nt(xb, wg_ref[0:GLA_K, :]) * (GLA_DK ** -0.5)
    kg = _dot_nt(xb, wg_ref[GLA_K:2 * GLA_K, :])
    for s in range(GLA_K // LANES):
        sl = slice(s * LANES, (s + 1) * LANES)
        put(qg_ref, s, qg[:, sl])
        put(kg_ref, s, kg[:, sl])
        put(la_ref, s, loga[:, sl])
    vg_ref[...] = _dot_nt(xb, wg_ref[2 * GLA_K:2 * GLA_K + GLA_V, :])
    rg_ref[...] = _dot_nt(xb, wg_ref[2 * GLA_K + GLA_V:2 * GLA_K + 2 * GLA_V, :]).astype(BF16)

    gates_ref[...] = _sigmoid(_dot_nt(xb, wgates_ref[...])).astype(BF16)

    if gla_sample:
        @pl.when(step == pl.num_programs(0) - 1)
        def _():
            for h in range(GLA_HEADS):
                vs = slice(h * GLA_DV, (h + 1) * GLA_DV)
                obs_ref[:, vs] = _gla_epilogue(o_scr[:, vs], rs_ref[:, vs], gs_ref[:, vs]).astype(obs_ref.dtype)


def _in_proj(x, rope_tab, weights, *, tm, rope_period_tiles, sample, gla_sample=None):
    rows = x.shape[0]
    steps = rows // tm
    assert not sample or tm == rows == LANES
    w, wup, bg = weights

    def out(width, dtype, transposed=False):
        if transposed and sample:
            return jax.ShapeDtypeStruct((width, rows), dtype), pl.BlockSpec((width, rows), lambda i: (0, 0))
        return jax.ShapeDtypeStruct((rows, width), dtype), pl.BlockSpec((tm, width), lambda i: (i, 0))

    outs = [
        out(A_WIDTH, F32, True),
        out(A_WIDTH, F32, True),
        out(A_WIDTH, F32, True),
        out(GLA_K, F32, True),
        out(GLA_K, F32, True),
        out(GLA_V, F32),
        out(GLA_V, BF16),
        out(GLA_K, F32, True),
        out(2 * D_MODEL, BF16),
    ]
    in_specs = [
        pl.BlockSpec((tm, D_MODEL), lambda i: (i, 0)),
        pl.BlockSpec((tm, 3 * LANES), lambda i: (i % rope_period_tiles, 0)),
        _const_spec(w.shape), _const_spec(wup.shape), _const_spec(bg.shape),
    ]
    operands = [x, rope_tab, w, wup, bg]
    scratch = []
    if gla_sample is not None:
        qgt, kgt, lat, vg, rg, norm_g, state = gla_sample
        n = state.shape[0]
        assert n % steps == 0, "the same number of requests per grid step"
        st_spec = pl.BlockSpec((n // steps,) + state.shape[1:], lambda i: (i, 0, 0, 0))
        in_specs += [_const_spec(t.shape) for t in (qgt, kgt, lat, vg, rg, norm_g)] + [st_spec]
        operands += [qgt, kgt, lat, vg, rg, norm_g, state]
        outs += [(jax.ShapeDtypeStruct((n, GLA_V), BF16), pl.BlockSpec((n, GLA_V), lambda i: (0, 0))),
                 (jax.ShapeDtypeStruct(state.shape, F32), st_spec)]
        scratch = [pltpu.VMEM((n, GLA_V), F32)]
    return pl.pallas_call(
        functools.partial(_in_proj_kernel, sample=sample, gla_sample=gla_sample is not None),
        grid=(steps,),
        in_specs=in_specs,
        out_specs=tuple(o[1] for o in outs),
        out_shape=tuple(o[0] for o in outs),
        scratch_shapes=scratch,
        compiler_params=pltpu.CompilerParams(
            dimension_semantics=("arbitrary" if gla_sample is not None else "parallel",),
            vmem_limit_bytes=VMEM_LIMIT),
        name="in_proj_sample" if sample else "in_proj_prompt",
    )(*operands)


def _rope_table(pos):
    half = ROT_DIM // 2
    inv_freq = ROPE_THETA ** (-jnp.arange(0, ROT_DIM, 2, dtype=F32) / ROT_DIM)
    ang = pos.astype(F32)[:, None] * inv_freq[None, :]
    cos, sin = jnp.cos(ang), jnp.sin(ang)
    lane = np.arange(LANES) % HEAD_DIM
    fidx = lane % half
    first = lane < half
    second = (lane >= half) & (lane < ROT_DIM)
    cos_t = jnp.where((first | second)[None, :], cos[:, fidx], 1.0)
    sin_lo = jnp.where(first[None, :], -sin[:, fidx], 0.0)
    sin_hi = jnp.where(second[None, :], sin[:, fidx], 0.0)
    return jnp.concatenate([cos_t, sin_lo, sin_hi], axis=1)


ATT_TILE = 2048
Q_BLOCK = 128
HEADS_PER_TILE = LANES // HEAD_DIM


def _attn_prompt_kernel(q_ref, kc_ref, kp_ref, vc_ref, vp_ref, o_ref, og, lg, bias):
    t = pl.program_id(1)
    g = pl.program_id(3)
    band_keys = 2 * Q_BLOCK
    qi = lax.broadcasted_iota(jnp.int32, (Q_BLOCK, HEADS_PER_TILE * band_keys), 0)
    kc = lax.broadcasted_iota(jnp.int32, (Q_BLOCK, HEADS_PER_TILE * band_keys), 1) % band_keys
    back = qi + Q_BLOCK - kc
    band = (back >= 0) & (back <= KEYS_PER_WINDOW)
    bias[0] = jnp.where(band, 0.0, -jnp.inf)
    bias[1] = jnp.where(band & (kc >= Q_BLOCK), 0.0, -jnp.inf)
    head0 = lax.broadcasted_iota(jnp.int32, (band_keys, LANES), 1) < HEAD_DIM
    head0_q = lax.broadcasted_iota(jnp.int32, (Q_BLOCK, LANES), 1) < HEAD_DIM

    for gi, (window, dil) in enumerate(DILATED_GROUPS):

        @pl.when(g == gi)
        def _(gi=gi, window=window, dil=dil):
            nblk = ATT_TILE // dil // Q_BLOCK

            def rows(start):
                return pl.ds(start, Q_BLOCK) if dil == 1 else pl.ds(start, Q_BLOCK, stride=dil)

            def block(r, n):
                base = r + dil * Q_BLOCK * n
                q2 = q_ref[rows(base), :].astype(BF16)
                if n == 0:
                    prev = rows(ATT_TILE - window + r)
                    k_prev, v_prev = kp_ref[prev, :], vp_ref[prev, :]
                    mask = bias[(t == 0).astype(jnp.int32)]
                else:
                    prev = rows(base - dil * Q_BLOCK)
                    k_prev, v_prev = kc_ref[prev, :], vc_ref[prev, :]
                    mask = bias[0]
                k2 = jnp.concatenate([k_prev, kc_ref[rows(base), :]], axis=0)
                v2 = jnp.concatenate([v_prev, vc_ref[rows(base), :]], axis=0)
                kz = jnp.concatenate([jnp.where(head0, k2, 0.0), jnp.where(head0, 0.0, k2)], axis=0)
                s = _dot_nt(q2, kz.astype(BF16)) + mask
                s0, s1 = s[:, :band_keys], s[:, band_keys:]
                m0 = jnp.max(s0, axis=-1, keepdims=True)
                m1 = jnp.max(s1, axis=-1, keepdims=True)
                p = jnp.concatenate([jnp.exp(s0 - m0), jnp.exp(s1 - m1)], axis=1).astype(BF16)
                ones0 = head0.astype(F32)
                vz = jnp.concatenate([
                    jnp.concatenate([jnp.where(head0, v2, 0.0), ones0], axis=1),
                    jnp.concatenate([jnp.where(head0, 0.0, v2), 1.0 - ones0], axis=1)], axis=0)
                res = _dot(p, vz.astype(BF16))
                den = res[:, LANES:]
                og[gi, rows(base), :] = res[:, :LANES] / den
                m = jnp.where(head0_q, jnp.broadcast_to(m0, den.shape), jnp.broadcast_to(m1, den.shape))
                lg[gi, rows(base), :] = m + jnp.log(den)

            for r in range(dil):
                for n in range(nblk):
                    block(r, n)

    @pl.when(g == len(DILATED_GROUPS) - 1)
    def _():
        rows = 256

        def combine(c, carry):
            r0 = pl.multiple_of(c * rows, rows)
            l0 = lg[0, pl.ds(r0, rows), :]
            l1 = lg[1, pl.ds(r0, rows), :]
            l2 = lg[2, pl.ds(r0, rows), :]
            mx = jnp.maximum(jnp.maximum(l0, l1), l2)
            e0, e1, e2 = jnp.exp(l0 - mx), jnp.exp(l1 - mx), jnp.exp(l2 - mx)
            acc = (e0 * og[0, pl.ds(r0, rows), :] + e1 * og[1, pl.ds(r0, rows), :]
                   + e2 * og[2, pl.ds(r0, rows), :])
            o_ref[pl.ds(r0, rows), :] = (acc / (e0 + e1 + e2)).astype(o_ref.dtype)
            return carry

        lax.fori_loop(0, ATT_TILE // rows, combine, 0)


def _attn_prompt(q, k, v):
    b, s, _ = q.shape
    tiles_per_group = GROUP_COLS // LANES
    cur = pl.BlockSpec((None, ATT_TILE, LANES), lambda bi, t, c, g: (bi, t, g * tiles_per_group + c))
    prev = pl.BlockSpec((None, ATT_TILE, LANES),
                        lambda bi, t, c, g: (bi, jnp.maximum(t - 1, 0), g * tiles_per_group + c))
    assert all(w <= ATT_TILE for w, _ in DILATED_GROUPS), "the key halo must fit in the previous tile"
    ngroups = len(DILATED_GROUPS)
    return pl.pallas_call(
        _attn_prompt_kernel,
        grid=(b, s // ATT_TILE, tiles_per_group, ngroups),
        in_specs=[cur, cur, prev, cur, prev],
        out_specs=pl.BlockSpec((None, ATT_TILE, LANES), lambda bi, t, c, g: (bi, t, c)),
        out_shape=jax.ShapeDtypeStruct((b, s, GROUP_COLS), BF16),
        scratch_shapes=[
            pltpu.VMEM((ngroups, ATT_TILE, LANES), F32),
            pltpu.VMEM((ngroups, ATT_TILE, LANES), F32),
            pltpu.VMEM((2, Q_BLOCK, HEADS_PER_TILE * 2 * Q_BLOCK), F32),
        ],
        compiler_params=pltpu.CompilerParams(
            dimension_semantics=("parallel", "parallel", "parallel", "arbitrary"),
            vmem_limit_bytes=VMEM_LIMIT),
        name="attn_prompt",
    )(q, k, k, v, v)


GLA_TILE = 512


def _gla_epilogue(o, r, g):
    o = o * lax.rsqrt(jnp.mean(o * o, -1, keepdims=True) + RMS_EPS)
    return o * g * _silu(r.astype(F32))


def _gla_prompt_kernel(q_ref, k_ref, v_ref, cum_ref, r_ref, g_ref, o_ref, st_ref, st_t):
    step = pl.program_id(1)
    c = GLA_CHUNK
    nchunks = q_ref.shape[0] // c

    @pl.when(step == 0)
    def _():
        st_t[...] = jnp.zeros_like(st_t)

    ri = lax.broadcasted_iota(jnp.int32, (c, c), 0)
    ci = lax.broadcasted_iota(jnp.int32, (c, c), 1)
    row_id = lax.broadcasted_iota(jnp.int32, (c, GLA_DK), 0)

    def chunk(ch, factored):
        r0 = ch * c if isinstance(ch, int) else pl.multiple_of(ch * c, c)
        for h in range(GLA_HEADS):
            ks = slice(h * GLA_DK, (h + 1) * GLA_DK)
            vs = slice(h * GLA_DV, (h + 1) * GLA_DV)
            q = q_ref[pl.ds(r0, c), ks]
            k = k_ref[pl.ds(r0, c), ks]
            v = v_ref[pl.ds(r0, c), vs].astype(BF16)
            cum = cum_ref[pl.ds(r0, c), ks]
            last = cum[c - 1:c]
            decay = jnp.exp(last)
            qe = (q * jnp.exp(cum)).astype(BF16)
            if factored:
                ke = k * jnp.exp(-cum)
                a = jnp.where(ci <= ri, _dot_nt(qe, ke.astype(BF16)), 0.0)
                kd = ke * decay
            else:
                def col(j, a):
                    cj = jnp.sum(jnp.where(row_id == j, cum, 0.0), axis=0, keepdims=True)
                    kj = jnp.sum(jnp.where(row_id == j, k, 0.0), axis=0, keepdims=True)
                    e = jnp.exp(jnp.where(row_id >= j, cum - cj, -jnp.inf))
                    sj = jnp.sum(q * kj * e, axis=-1, keepdims=True)
                    return jnp.where(ci == j, sj, a)

                a = lax.fori_loop(0, c, col, jnp.zeros((c, c), F32))
                kd = k * jnp.exp(last - cum)
            s_t = st_t[h]
            o = _dot_nt(qe, s_t.astype(BF16)) + _dot(a.astype(BF16), v)
            o_ref[pl.ds(r0, c), vs] = _gla_epilogue(o, r_ref[pl.ds(r0, c), vs], g_ref[:, vs]).astype(o_ref.dtype)
            st_t[h] = s_t * decay + _dot_tn(v, kd.astype(BF16))

    total = -jnp.min(cum_ref[...])

    @pl.when(total < GLA_FACTORED_RANGE)
    def _():
        for ch in range(nchunks):
            chunk(ch, True)

    @pl.when(total >= GLA_FACTORED_RANGE)
    def _():
        def body(ch, carry):
            chunk(ch, False)
            return carry

        lax.fori_loop(0, nchunks, body, 0)

    @pl.when(step == pl.num_programs(1) - 1)
    def _():
        for h in range(GLA_HEADS):
            st_ref[h] = st_t[h].T


def _gla_prompt(qg, kg, vg, cum, rg, norm_g):
    b, s, _ = qg.shape
    kspec = pl.BlockSpec((None, GLA_TILE, GLA_K), lambda bi, t: (bi, t, 0))
    vspec = pl.BlockSpec((None, GLA_TILE, GLA_V), lambda bi, t: (bi, t, 0))
    return pl.pallas_call(
        _gla_prompt_kernel,
        grid=(b, s // GLA_TILE),
        in_specs=[kspec, kspec, vspec, kspec, vspec, _const_spec(norm_g.shape)],
        out_specs=(vspec, pl.BlockSpec((None, GLA_HEADS, GLA_DK, GLA_DV), lambda bi, t: (bi, 0, 0, 0))),
        out_shape=(jax.ShapeDtypeStruct((b, s, GLA_V), BF16),
                   jax.ShapeDtypeStruct((b, GLA_HEADS, GLA_DK, GLA_DV), F32)),
        scratch_shapes=[pltpu.VMEM((GLA_HEADS, GLA_DV, GLA_DK), F32)],
        compiler_params=pltpu.CompilerParams(
            dimension_semantics=("parallel", "arbitrary"), vmem_limit_bytes=VMEM_LIMIT),
        name="gla_prompt",
    )(qg, kg, vg, cum, rg, norm_g)


def _pick_column(ref, n):
    lane = lax.broadcasted_iota(jnp.int32, ref.shape, 1)
    return jnp.sum(jnp.where(lane == n, ref[...], 0.0), axis=1, keepdims=True)


def _attn_sample_step(n, qt_ref, kt_ref, vt_ref, caches, news, o_cols):
    ngroups = len(DILATED_GROUPS)
    qcol = _pick_column(qt_ref, n)
    kcol = _pick_column(kt_ref, n)
    vcol = _pick_column(vt_ref, n)

    def shifted(x, new):
        lb = x.shape[1]
        lane = lax.broadcasted_iota(jnp.int32, x.shape, 1)
        return jnp.where(lane == lb - 1, new, pltpu.roll(x, lb - 1, 1))

    heads = []
    for h in range(HEADS_PER_GROUP):
        s_old, s_new = [], []
        for gi, (window, dil) in enumerate(DILATED_GROUPS):
            lb = caches[gi].shape[1]
            r0 = gi * GROUP_COLS + h * HEAD_DIM
            q1 = qcol[r0:r0 + HEAD_DIM]
            k1 = kcol[r0:r0 + HEAD_DIM]
            keys = caches[gi][h * HEAD_DIM:(h + 1) * HEAD_DIM, :]
            news[gi][h * HEAD_DIM:(h + 1) * HEAD_DIM, :] = shifted(keys, k1)
            s = jnp.sum(keys * q1, axis=0, keepdims=True)
            pos = lax.broadcasted_iota(jnp.int32, (1, lb), 1)
            s_old.append(jnp.where(pos % dil == 0, s, -jnp.inf))
            s_new.append(jnp.sum(k1 * q1, axis=0, keepdims=True))
        m = s_new[0]
        for gi in range(ngroups):
            m = jnp.maximum(m, jnp.maximum(s_new[gi], jnp.max(s_old[gi], axis=1, keepdims=True)))
        den = jnp.zeros((1, 1), F32)
        acc = jnp.zeros((HEAD_DIM, 1), F32)
        for gi in range(ngroups):
            r0 = gi * GROUP_COLS + h * HEAD_DIM
            v1 = vcol[r0:r0 + HEAD_DIM]
            p_old = jnp.exp(s_old[gi] - m)
            p_new = jnp.exp(s_new[gi] - m)
            den = den + jnp.sum(p_old, axis=1, keepdims=True) + p_new
            rows = slice(GROUP_COLS + h * HEAD_DIM, GROUP_COLS + (h + 1) * HEAD_DIM)
            vals = caches[gi][rows, :]
            news[gi][rows, :] = shifted(vals, v1)
            acc = acc + jnp.sum(vals * p_old, axis=1, keepdims=True) + v1 * p_new
        heads.append(acc / den)
    o_col = jnp.concatenate(heads, axis=0)
    lane = lax.broadcasted_iota(jnp.int32, o_cols.shape, 1)
    o_cols[...] = jnp.where(lane == n, o_col, o_cols[...])


def _post_kernel(x_ref, oa_ref, ob_ref, gates_ref, pe_ref,
                 wa_ref, wb_ref, wo_ref, wfg_ref, wfu_ref, wfd_ref, wpg_ref, wpp_ref,
                 ln1g_ref, ln1b_ref, ln2g_ref, ln2b_ref, *rest, sample_attn):
    if sample_attn:
        qt_ref, kt_ref, vt_ref, c1_ref, c2_ref, c3_ref, y_ref, n1_ref, n2_ref, n3_ref, oas_ref, o_cols = rest
        i = pl.program_id(0)

        @pl.when(i == 0)
        def _():
            o_cols[...] = jnp.zeros_like(o_cols)

        _attn_sample_step(i, qt_ref, kt_ref, vt_ref, (c1_ref, c2_ref, c3_ref),
                          (n1_ref, n2_ref, n3_ref), o_cols)
    else:
        (y_ref,) = rest

    ga = gates_ref[:, 0:D_MODEL].astype(F32)
    gb = gates_ref[:, D_MODEL:2 * D_MODEL].astype(F32)
    merged = (ga * _dot(oa_ref[...].astype(BF16), wa_ref[...])
              + gb * _dot(ob_ref[...].astype(BF16), wb_ref[...]))
    x1 = _layernorm(DN_ALPHA * x_ref[...] + _dot(merged.astype(BF16), wo_ref[...]),
                    ln1g_ref[...], ln1b_ref[...])
    x1b = x1.astype(BF16)
    act = _silu(_dot(x1b, wfg_ref[...])) * _dot(x1b, wfu_ref[...])
    x2 = _layernorm(DN_ALPHA * x1 + _dot(act.astype(BF16), wfd_ref[...]), ln2g_ref[...], ln2b_ref[...])
    gate = _sigmoid(_dot(x2.astype(BF16), wpg_ref[...]))
    y_ref[...] = x2 + gate * _dot(pe_ref[...].astype(BF16), wpp_ref[...])

    if sample_attn:
        @pl.when(i == pl.num_programs(0) - 1)
        def _():
            for c in range(GROUP_COLS // LANES):
                oas_ref[:, c * LANES:(c + 1) * LANES] = o_cols[c * LANES:(c + 1) * LANES, :].T


def _post(x, oa, ob, gates, pe, weights, *, tm, sample_attn=None):
    rows = x.shape[0]
    steps = rows // tm
    row_spec = lambda n: pl.BlockSpec((tm, n), lambda i: (i, 0))
    in_specs = [row_spec(D_MODEL), row_spec(GROUP_COLS), row_spec(GLA_V), row_spec(2 * D_MODEL),
                row_spec(PLE_DIM)] + [_const_spec(w.shape) for w in weights]
    out_specs = [row_spec(D_MODEL)]
    out_shape = [jax.ShapeDtypeStruct((rows, D_MODEL), F32)]
    operands = [x, oa, ob, gates, pe, *weights]
    scratch = []
    if sample_attn is not None:
        qt, kt, vt, caches = sample_attn
        n = qt.shape[1]
        assert n == LANES == steps, "one request per grid step"
        for (window, _), c in zip(DILATED_GROUPS, caches):
            assert c.shape[2] == window, "every key distance must fall inside the cache"
        cache_specs = [pl.BlockSpec((None,) + c.shape[1:], lambda i: (i, 0, 0)) for c in caches]
        in_specs += [_const_spec(qt.shape)] * 3 + cache_specs
        out_specs += cache_specs + [pl.BlockSpec((n, GROUP_COLS), lambda i: (0, 0))]
        out_shape += [jax.ShapeDtypeStruct(c.shape, c.dtype) for c in caches]
        out_shape += [jax.ShapeDtypeStruct((n, GROUP_COLS), F32)]
        operands += [qt, kt, vt, *caches]
        scratch = [pltpu.VMEM((GROUP_COLS, n), F32)]
    out = pl.pallas_call(
        functools.partial(_post_kernel, sample_attn=sample_attn is not None),
        grid=(steps,),
        in_specs=in_specs,
        out_specs=tuple(out_specs),
        out_shape=tuple(out_shape),
        scratch_shapes=scratch,
        compiler_params=pltpu.CompilerParams(
            dimension_semantics=("arbitrary" if sample_attn is not None else "parallel",),
            vmem_limit_bytes=VMEM_LIMIT),
        name="post_with_sample_attn" if sample_attn is not None else "post",
    )(*operands)
    return out if sample_attn is not None else out[0]


PROMPT_TM = 256


def _layer_weights(w_in, w_gate_up, b_gate, w_a_out, w_b_out, w_o, ln1_g, ln1_b,
                   w_ff_gate, w_ff_up, w_ff_down, ln2_g, ln2_b, w_ple_gate, w_ple_proj):
    proj = (w_in.T.astype(BF16), w_gate_up.astype(BF16), b_gate.reshape(1, GLA_K))
    row = lambda p: p.reshape(1, -1)
    post = (w_a_out.astype(BF16), w_b_out.astype(BF16), w_o.astype(BF16), w_ff_gate.astype(BF16),
            w_ff_up.astype(BF16), w_ff_down.astype(BF16), w_ple_gate.astype(BF16),
            w_ple_proj.astype(BF16), row(ln1_g), row(ln1_b), row(ln2_g), row(ln2_b))
    return proj, post


def _kv_tail(k, v, gi, keep):
    b, s, _ = k.shape
    cols = slice(gi * GROUP_COLS, (gi + 1) * GROUP_COLS)
    kt = k[:, s - keep:, cols].reshape(b, keep, HEADS_PER_GROUP, HEAD_DIM)
    vt = v[:, s - keep:, cols].reshape(b, keep, HEADS_PER_GROUP, HEAD_DIM)
    return jnp.stack([kt, vt], axis=2)


def _positions_minor(cache):
    n, lb = cache.shape[:2]
    return jnp.transpose(cache, (0, 2, 3, 4, 1)).reshape(n, 2 * GROUP_COLS, lb)


def _positions_major(cache):
    n, _, lb = cache.shape
    return jnp.transpose(cache.reshape(n, 2, HEADS_PER_GROUP, HEAD_DIM, lb), (0, 4, 1, 2, 3))


def kernel(x_prompt, x_sample, cache_a1_kv, cache_a2_kv, cache_a3_kv, state_gla, p_prompt, p_sample, w_in, w_gate_up, b_gate, gla_norm_g, w_a_out, w_b_out, w_o, ln1_g, ln1_b, w_ff_gate, w_ff_up, w_ff_down, ln2_g, ln2_b, w_ple_gate, w_ple_proj):
    depth = w_in.shape[0]
    assert depth == 1 and x_sample.shape[1] == 1
    b, s, d = x_prompt.shape
    n = x_sample.shape[0]
    proj_w, post_w = _layer_weights(
        w_in[0], w_gate_up[0], b_gate[0], w_a_out[0], w_b_out[0], w_o[0], ln1_g[0], ln1_b[0],
        w_ff_gate[0], w_ff_up[0], w_ff_down[0], ln2_g[0], ln2_b[0], w_ple_gate[0], w_ple_proj[0])
    norm_g = gla_norm_g[0].reshape(1, GLA_V)

    xs = x_sample.reshape(n, d)
    rope_s = _rope_table(jnp.full((n,), PAST_LEN, jnp.int32))
    qt, kt, vt, qgt, kgt, vg_s, rg_s, lat, gates_s = _in_proj(
        xs, rope_s, proj_w, tm=n, rope_period_tiles=1, sample=True)

    xp = x_prompt.reshape(b * s, d)
    rope_p = _rope_table(jnp.arange(s))
    q, k, v, qg, kg, vg, rg, cum, gates, ob_s, st_s = _in_proj(
        xp, rope_p, proj_w, tm=PROMPT_TM, rope_period_tiles=s // PROMPT_TM, sample=False,
        gla_sample=(qgt, kgt, lat, vg_s, rg_s, norm_g, state_gla[0]))
    three = lambda t: t.reshape(b, s, t.shape[-1])
    k3, v3 = three(k), three(v)
    oa = _attn_prompt(three(q), k3, v3)
    ob, st_p = _gla_prompt(three(qg), three(kg), three(vg), three(cum), three(rg), norm_g)
    kv_p = [_kv_tail(k3, v3, gi, min(w, s))[None] for gi, (w, _) in enumerate(DILATED_GROUPS)]

    caches = [_positions_minor(c[0]) for c in (cache_a1_kv, cache_a2_kv, cache_a3_kv)]
    y_p, *new_caches, oa_s = _post(
        xp, oa.reshape(b * s, GROUP_COLS), ob.reshape(b * s, GLA_V), gates,
        p_prompt[0].reshape(b * s, PLE_DIM), post_w, tm=b * s // n, sample_attn=(qt, kt, vt, caches))
    y_s = _post(xs, oa_s, ob_s, gates_s, p_sample[0].reshape(n, PLE_DIM), post_w, tm=n)
    kv_s = [_positions_major(c)[None] for c in new_caches]

    return (y_p.reshape(b, s, d), y_s.reshape(n, 1, d), kv_p[0], kv_p[1], kv_p[2], st_p[None],
            kv_s[0], kv_s[1], kv_s[2], st_s[None])
```

```python
import functools

import numpy as np
import jax
import jax.numpy as jnp
from jax import lax
from jax.experimental import pallas as pl
from jax.experimental.pallas import tpu as pltpu

F32 = jnp.float32
BF16 = jnp.bfloat16

D_MODEL = 1024
HEAD_DIM = 64
HEADS_PER_GROUP = 4
DILATED_GROUPS = ((128, 1), (512, 4), (2048, 16))
GROUP_COLS = HEADS_PER_GROUP * HEAD_DIM
A_WIDTH = len(DILATED_GROUPS) * GROUP_COLS
ROT_DIM = HEAD_DIM // 4
ROPE_THETA = 500000.0
KEYS_PER_WINDOW = 128
GLA_HEADS = 4
GLA_DK = 128
GLA_DV = 256
GLA_K = GLA_HEADS * GLA_DK
GLA_V = GLA_HEADS * GLA_DV
GATE_RANK = 16
GATE_NORMALIZER = 16.0
GLA_CHUNK = 128
D_FF = 2816
PLE_DIM = 256
DN_ALPHA = 2.0 ** 0.25
LN_EPS = 1e-5
RMS_EPS = 1e-6
PAST_LEN = 8192
COL_GLA = 3 * A_WIDTH
COL_LOWRANK = COL_GLA + 2 * GLA_K + 2 * GLA_V
COL_GATES = COL_LOWRANK + GATE_RANK
IN_COLS = COL_GATES + 2 * D_MODEL

LANES = 128
SUBLANES = 8
VMEM_LIMIT = 56 * 1024 * 1024

GLA_FACTORED_RANGE = 40.0


def _dot(a, b):
    return jnp.dot(a, b, preferred_element_type=F32)


def _dot_nt(a, b):
    return lax.dot_general(a, b, (((1,), (1,)), ((), ())), preferred_element_type=F32)


def _dot_tn(a, b):
    return lax.dot_general(a, b, (((0,), (0,)), ((), ())), preferred_element_type=F32)


def _const_spec(shape):
    return pl.BlockSpec(shape, lambda *_: (0,) * len(shape), pipeline_mode=pl.Buffered(1))


def _layernorm(x, g, b):
    xc = x - jnp.mean(x, -1, keepdims=True)
    var = jnp.mean(xc * xc, -1, keepdims=True)
    return xc * lax.rsqrt(var + LN_EPS) * g + b


def _sigmoid(x):
    return 1.0 / (1.0 + jnp.exp(-x))


def _silu(x):
    return x * _sigmoid(x)


def _split3_bf16(x):
    hi = x.astype(BF16)
    r1 = x - hi.astype(F32)
    mid = r1.astype(BF16)
    lo = (r1 - mid.astype(F32)).astype(BF16)
    return hi, mid, lo


def _gla_sample_step(n, j, qt_ref, kt_ref, lat_ref, v_ref, st_ref, st_out_ref, o_scr):
    a = jnp.exp(_pick_column(lat_ref, n))
    k = _pick_column(kt_ref, n)
    q = _pick_column(qt_ref, n)
    for h in range(GLA_HEADS):
        ks = slice(h * GLA_DK, (h + 1) * GLA_DK)
        vs = slice(h * GLA_DV, (h + 1) * GLA_DV)
        st = a[ks] * st_ref[j, h] + k[ks] * v_ref[pl.ds(n, 1), vs]
        st_out_ref[j, h] = st
        o_scr[pl.ds(n, 1), vs] = jnp.sum(q[ks] * st, axis=0, keepdims=True)


def _gla_chunk(q, k, v, cum, s_t, factored):
    c = q.shape[0]
    ri = lax.broadcasted_iota(jnp.int32, (c, c), 0)
    ci = lax.broadcasted_iota(jnp.int32, (c, c), 1)
    last = cum[c - 1:c]
    decay = jnp.exp(last)
    qe = (q * jnp.exp(cum)).astype(BF16)
    vb = v.astype(BF16)
    if factored:
        ke = k * jnp.exp(-cum)
        a = jnp.where(ci <= ri, _dot_nt(qe, ke.astype(BF16)), 0.0)
        kd = ke * decay
    else:
        row_id = lax.broadcasted_iota(jnp.int32, cum.shape, 0)

        def col(j, a):
            cj = jnp.sum(jnp.where(row_id == j, cum, 0.0), axis=0, keepdims=True)
            kj = jnp.sum(jnp.where(row_id == j, k, 0.0), axis=0, keepdims=True)
            e = jnp.exp(jnp.where(row_id >= j, cum - cj, -jnp.inf))
            sj = jnp.sum(q * kj * e, axis=-1, keepdims=True)
            return jnp.where(ci == j, sj, a)

        a = lax.fori_loop(0, c, col, jnp.zeros((c, c), F32))
        kd = k * jnp.exp(last - cum)
    o = _dot_nt(qe, s_t.astype(BF16)) + _dot(a.astype(BF16), vb)
    return o, s_t * decay + _dot_tn(vb, kd.astype(BF16))


def _in_proj_kernel(x_ref, rope_ref, w_ref, wup_ref, bg_ref, *rest, sample, tiles_per_seq):
    tm = x_ref.shape[0]
    wa_ref = w_ref.at[0:COL_GLA]
    wg_ref = w_ref.at[COL_GLA:COL_LOWRANK]
    wlr_ref = w_ref.at[COL_LOWRANK:COL_GATES]
    wgates_ref = w_ref.at[COL_GATES:IN_COLS]
    xb = x_ref[...].astype(BF16)
    cos = rope_ref[:, 0:LANES]
    sin_lo = rope_ref[:, LANES:2 * LANES]
    sin_hi = rope_ref[:, 2 * LANES:3 * LANES]

    def rope(t):
        return (t * cos + pltpu.roll(t, LANES - ROT_DIM // 2, 1) * sin_lo
                + pltpu.roll(t, ROT_DIM // 2, 1) * sin_hi)

    def put(ref, col, val):
        if sample:
            ref[col * LANES:(col + 1) * LANES, :] = val.T
        else:
            ref[:, col * LANES:(col + 1) * LANES] = val

    def branch_a_and_gates(q_ref, k_ref, v_ref, gates_ref):
        tiles = GROUP_COLS // LANES
        for c in range(A_WIDTH // GROUP_COLS):
            lo, hi = c * GROUP_COLS, (c + 1) * GROUP_COLS
            qc = _dot_nt(xb, wa_ref[lo:hi, :])
            kc = _dot_nt(xb, wa_ref[A_WIDTH + lo:A_WIDTH + hi, :])
            vc = _dot_nt(xb, wa_ref[2 * A_WIDTH + lo:2 * A_WIDTH + hi, :])
            for s in range(tiles):
                sl = slice(s * LANES, (s + 1) * LANES)
                put(q_ref, c * tiles + s, rope(qc[:, sl]) * (HEAD_DIM ** -0.5))
                put(k_ref, c * tiles + s, rope(kc[:, sl]))
                put(v_ref, c * tiles + s, vc[:, sl])
        gates_ref[...] = _sigmoid(_dot_nt(xb, wgates_ref[...])).astype(BF16)

    def gla_projections():
        qg = _dot_nt(xb, wg_ref[0:GLA_K, :]) * (GLA_DK ** -0.5)
        kg = _dot_nt(xb, wg_ref[GLA_K:2 * GLA_K, :])
        vg = _dot_nt(xb, wg_ref[2 * GLA_K:2 * GLA_K + GLA_V, :])
        rg = _dot_nt(xb, wg_ref[2 * GLA_K + GLA_V:2 * GLA_K + 2 * GLA_V, :])
        return qg, kg, vg, rg

    def log_decay(rows_bf16):
        glr = _dot_nt(rows_bf16, wlr_ref[...])
        z = _dot(glr.astype(BF16), wup_ref[...]) + bg_ref[...]
        return -(jnp.maximum(-z, 0.0) + jnp.log1p(jnp.exp(-jnp.abs(z)))) * (1.0 / GATE_NORMALIZER)

    if sample:
        loga = log_decay(xb)
        q_ref, k_ref, v_ref, qg_ref, kg_ref, vg_ref, rg_ref, la_ref, gates_ref = rest
        branch_a_and_gates(q_ref, k_ref, v_ref, gates_ref)
        qg, kg, vg, rg = gla_projections()
        for s in range(GLA_K // LANES):
            sl = slice(s * LANES, (s + 1) * LANES)
            put(qg_ref, s, qg[:, sl])
            put(kg_ref, s, kg[:, sl])
            put(la_ref, s, loga[:, sl])
        vg_ref[...] = vg
        rg_ref[...] = rg.astype(BF16)
        return

    (x_next_ref, gn_ref, qts_ref, kts_ref, lats_ref, vs_ref, rs_ref, st_ref,
     q_ref, k_ref, v_ref, gates_ref, ob_ref, stp_ref, obs_ref, st_out_ref,
     st_t, o_scr, cum_scr, range_scr) = rest
    step = pl.program_id(0)
    tile_in_seq = step % tiles_per_seq

    @pl.when(tile_in_seq == 0)
    def _():
        st_t[...] = jnp.zeros_like(st_t)

    def stage_decay(rows_bf16):
        row = lax.broadcasted_iota(jnp.int32, (tm, tm), 0)
        col = lax.broadcasted_iota(jnp.int32, (tm, tm), 1)
        tri = ((row // GLA_CHUNK == col // GLA_CHUNK) & (col <= row)).astype(BF16)
        hi, mid, lo = _split3_bf16(log_decay(rows_bf16))
        cum = _dot(tri, hi) + _dot(tri, mid) + _dot(tri, lo)
        cum_scr[...] = cum
        range_scr[0] = -jnp.min(cum)

    @pl.when(step == 0)
    def _():
        stage_decay(xb)

    def rest_of_tile(factored):
        cum = cum_scr[...]
        per_step = st_ref.shape[0]
        for j in range(per_step):
            _gla_sample_step(step * per_step + j, j, qts_ref, kts_ref, lats_ref, vs_ref,
                             st_ref, st_out_ref, o_scr)
        qg, kg, vg, rg = gla_projections()
        for ch in range(tm // GLA_CHUNK):
            rows = slice(ch * GLA_CHUNK, (ch + 1) * GLA_CHUNK)
            for h in range(GLA_HEADS):
                ks = slice(h * GLA_DK, (h + 1) * GLA_DK)
                vs = slice(h * GLA_DV, (h + 1) * GLA_DV)
                o, st_t[h] = _gla_chunk(qg[rows, ks], kg[rows, ks], vg[rows, vs], cum[rows, ks],
                                        st_t[h], factored)
                ob_ref[rows, vs] = _gla_epilogue(o, rg[rows, vs], gn_ref[:, vs]).astype(ob_ref.dtype)
        branch_a_and_gates(q_ref, k_ref, v_ref, gates_ref)
        stage_decay(x_next_ref[...].astype(BF16))

    total = range_scr[0]
    pl.when(total < GLA_FACTORED_RANGE)(functools.partial(rest_of_tile, True))
    pl.when(total >= GLA_FACTORED_RANGE)(functools.partial(rest_of_tile, False))

    @pl.when(tile_in_seq == tiles_per_seq - 1)
    def _():
        for h in range(GLA_HEADS):
            stp_ref[h] = st_t[h].T

    @pl.when(step == pl.num_programs(0) - 1)
    def _():
        for h in range(GLA_HEADS):
            vs = slice(h * GLA_DV, (h + 1) * GLA_DV)
            obs_ref[:, vs] = _gla_epilogue(o_scr[:, vs], rs_ref[:, vs], gn_ref[:, vs]).astype(obs_ref.dtype)


def _in_proj(x, rope_tab, weights, *, tm, rope_period_tiles, sample, norm_g=None, gla_sample=None):
    rows = x.shape[0]
    steps = rows // tm
    w, wup, bg = weights
    row = lambda width: pl.BlockSpec((tm, width), lambda i: (i, 0))
    whole = lambda shape: pl.BlockSpec(shape, lambda i: (0,) * len(shape))
    in_specs = [row(D_MODEL), pl.BlockSpec((tm, 3 * LANES), lambda i: (i % rope_period_tiles, 0)),
                _const_spec(w.shape), _const_spec(wup.shape), _const_spec(bg.shape)]
    next_tile = pl.BlockSpec((tm, D_MODEL), lambda i: (jnp.minimum(i + 1, steps - 1), 0))
    operands = [x, rope_tab, w, wup, bg]
    scratch = []
    if sample:
        assert tm == rows == LANES
        t = lambda width: (jax.ShapeDtypeStruct((width, rows), F32), whole((width, rows)))
        outs = [t(A_WIDTH), t(A_WIDTH), t(A_WIDTH), t(GLA_K), t(GLA_K),
                (jax.ShapeDtypeStruct((rows, GLA_V), F32), row(GLA_V)),
                (jax.ShapeDtypeStruct((rows, GLA_V), BF16), row(GLA_V)),
                t(GLA_K),
                (jax.ShapeDtypeStruct((rows, 2 * D_MODEL), BF16), row(2 * D_MODEL))]
    else:
        qgt, kgt, lat, vg, rg, state = gla_sample
        n = state.shape[0]
        seqs = steps // rope_period_tiles
        assert n % steps == 0, "the same number of requests per grid step"
        assert tm % GLA_CHUNK == 0
        st_spec = pl.BlockSpec((n // steps,) + state.shape[1:], lambda i: (i, 0, 0, 0))
        in_specs += [next_tile] + [_const_spec(a.shape) for a in (norm_g, qgt, kgt, lat, vg, rg)] + [st_spec]
        operands += [x, norm_g, qgt, kgt, lat, vg, rg, state]
        outs = [(jax.ShapeDtypeStruct((rows, A_WIDTH), F32), row(A_WIDTH)),
                (jax.ShapeDtypeStruct((rows, A_WIDTH), F32), row(A_WIDTH)),
                (jax.ShapeDtypeStruct((rows, A_WIDTH), F32), row(A_WIDTH)),
                (jax.ShapeDtypeStruct((rows, 2 * D_MODEL), BF16), row(2 * D_MODEL)),
                (jax.ShapeDtypeStruct((rows, GLA_V), BF16), row(GLA_V)),
                (jax.ShapeDtypeStruct((seqs, GLA_HEADS, GLA_DK, GLA_DV), F32),
                 pl.BlockSpec((None, GLA_HEADS, GLA_DK, GLA_DV), lambda i: (i // rope_period_tiles, 0, 0, 0))),
                (jax.ShapeDtypeStruct((n, GLA_V), BF16), whole((n, GLA_V))),
                (jax.ShapeDtypeStruct(state.shape, F32), st_spec)]
        scratch = [pltpu.VMEM((GLA_HEADS, GLA_DV, GLA_DK), F32), pltpu.VMEM((n, GLA_V), F32),
                   pltpu.VMEM((tm, GLA_K), F32), pltpu.SMEM((1,), F32)]
    return pl.pallas_call(
        functools.partial(_in_proj_kernel, sample=sample, tiles_per_seq=rope_period_tiles),
        grid=(steps,),
        in_specs=in_specs,
        out_specs=tuple(o[1] for o in outs),
        out_shape=tuple(o[0] for o in outs),
        scratch_shapes=scratch,
        compiler_params=pltpu.CompilerParams(
            dimension_semantics=("parallel" if sample else "arbitrary",), vmem_limit_bytes=VMEM_LIMIT),
        name="in_proj_sample" if sample else "in_proj_prompt",
    )(*operands)


def _rope_table(pos):
    half = ROT_DIM // 2
    inv_freq = ROPE_THETA ** (-jnp.arange(0, ROT_DIM, 2, dtype=F32) / ROT_DIM)
    ang = pos.astype(F32)[:, None] * inv_freq[None, :]
    cos, sin = jnp.cos(ang), jnp.sin(ang)
    lane = np.arange(LANES) % HEAD_DIM
    fidx = lane % half
    first = lane < half
    second = (lane >= half) & (lane < ROT_DIM)
    cos_t = jnp.where((first | second)[None, :], cos[:, fidx], 1.0)
    sin_lo = jnp.where(first[None, :], -sin[:, fidx], 0.0)
    sin_hi = jnp.where(second[None, :], sin[:, fidx], 0.0)
    return jnp.concatenate([cos_t, sin_lo, sin_hi], axis=1)


ATT_TILE = 2048
Q_BLOCK = 128
HEADS_PER_TILE = LANES // HEAD_DIM


def _attn_prompt_kernel(q_ref, kc_ref, kp_ref, vc_ref, vp_ref, o_ref, og, lg, bias):
    t = pl.program_id(1)
    g = pl.program_id(3)
    band_keys = 2 * Q_BLOCK
    qi = lax.broadcasted_iota(jnp.int32, (Q_BLOCK, HEADS_PER_TILE * band_keys), 0)
    kc = lax.broadcasted_iota(jnp.int32, (Q_BLOCK, HEADS_PER_TILE * band_keys), 1) % band_keys
    back = qi + Q_BLOCK - kc
    band = (back >= 0) & (back <= KEYS_PER_WINDOW)
    bias[0] = jnp.where(band, 0.0, -jnp.inf)
    bias[1] = jnp.where(band & (kc >= Q_BLOCK), 0.0, -jnp.inf)
    head0 = lax.broadcasted_iota(jnp.int32, (band_keys, LANES), 1) < HEAD_DIM
    head0_q = lax.broadcasted_iota(jnp.int32, (Q_BLOCK, LANES), 1) < HEAD_DIM

    for gi, (window, dil) in enumerate(DILATED_GROUPS):

        @pl.when(g == gi)
        def _(gi=gi, window=window, dil=dil):
            nblk = ATT_TILE // dil // Q_BLOCK

            def rows(start):
                return pl.ds(start, Q_BLOCK) if dil == 1 else pl.ds(start, Q_BLOCK, stride=dil)

            def block(r, n):
                base = r + dil * Q_BLOCK * n
                q2 = q_ref[rows(base), :].astype(BF16)
                if n == 0:
                    prev = rows(ATT_TILE - window + r)
                    k_prev, v_prev = kp_ref[prev, :], vp_ref[prev, :]
                    mask = bias[(t == 0).astype(jnp.int32)]
                else:
                    prev = rows(base - dil * Q_BLOCK)
                    k_prev, v_prev = kc_ref[prev, :], vc_ref[prev, :]
                    mask = bias[0]
                k2 = jnp.concatenate([k_prev, kc_ref[rows(base), :]], axis=0)
                v2 = jnp.concatenate([v_prev, vc_ref[rows(base), :]], axis=0)
                kz = jnp.concatenate([jnp.where(head0, k2, 0.0), jnp.where(head0, 0.0, k2)], axis=0)
                s = _dot_nt(q2, kz.astype(BF16)) + mask
                s0, s1 = s[:, :band_keys], s[:, band_keys:]
                m0 = jnp.max(s0, axis=-1, keepdims=True)
                m1 = jnp.max(s1, axis=-1, keepdims=True)
                p = jnp.concatenate([jnp.exp(s0 - m0), jnp.exp(s1 - m1)], axis=1).astype(BF16)
                ones0 = head0.astype(F32)
                vz = jnp.concatenate([
                    jnp.concatenate([jnp.where(head0, v2, 0.0), ones0], axis=1),
                    jnp.concatenate([jnp.where(head0, 0.0, v2), 1.0 - ones0], axis=1)], axis=0)
                res = _dot(p, vz.astype(BF16))
                den = res[:, LANES:]
                og[gi, rows(base), :] = res[:, :LANES] / den
                m = jnp.where(head0_q, jnp.broadcast_to(m0, den.shape), jnp.broadcast_to(m1, den.shape))
                lg[gi, rows(base), :] = m + jnp.log(den)

            for r in range(dil):
                for n in range(nblk):
                    block(r, n)

    @pl.when(g == len(DILATED_GROUPS) - 1)
    def _():
        rows = 256

        def combine(c, carry):
            r0 = pl.multiple_of(c * rows, rows)
            l0 = lg[0, pl.ds(r0, rows), :]
            l1 = lg[1, pl.ds(r0, rows), :]
            l2 = lg[2, pl.ds(r0, rows), :]
            mx = jnp.maximum(jnp.maximum(l0, l1), l2)
            e0, e1, e2 = jnp.exp(l0 - mx), jnp.exp(l1 - mx), jnp.exp(l2 - mx)
            acc = (e0 * og[0, pl.ds(r0, rows), :] + e1 * og[1, pl.ds(r0, rows), :]
                   + e2 * og[2, pl.ds(r0, rows), :])
            o_ref[pl.ds(r0, rows), :] = (acc / (e0 + e1 + e2)).astype(o_ref.dtype)
            return carry

        lax.fori_loop(0, ATT_TILE // rows, combine, 0)


def _attn_prompt(q, k, v):
    b, s, _ = q.shape
    tiles_per_group = GROUP_COLS // LANES
    cur = pl.BlockSpec((None, ATT_TILE, LANES), lambda bi, t, c, g: (bi, t, g * tiles_per_group + c))
    prev = pl.BlockSpec((None, ATT_TILE, LANES),
                        lambda bi, t, c, g: (bi, jnp.maximum(t - 1, 0), g * tiles_per_group + c))
    assert all(w <= ATT_TILE for w, _ in DILATED_GROUPS), "the key halo must fit in the previous tile"
    ngroups = len(DILATED_GROUPS)
    return pl.pallas_call(
        _attn_prompt_kernel,
        grid=(b, s // ATT_TILE, tiles_per_group, ngroups),
        in_specs=[cur, cur, prev, cur, prev],
        out_specs=pl.BlockSpec((None, ATT_TILE, LANES), lambda bi, t, c, g: (bi, t, c)),
        out_shape=jax.ShapeDtypeStruct((b, s, GROUP_COLS), BF16),
        scratch_shapes=[
            pltpu.VMEM((ngroups, ATT_TILE, LANES), F32),
            pltpu.VMEM((ngroups, ATT_TILE, LANES), F32),
            pltpu.VMEM((2, Q_BLOCK, HEADS_PER_TILE * 2 * Q_BLOCK), F32),
        ],
        compiler_params=pltpu.CompilerParams(
            dimension_semantics=("parallel", "parallel", "parallel", "arbitrary"),
            vmem_limit_bytes=VMEM_LIMIT),
        name="attn_prompt",
    )(q, k, k, v, v)


def _gla_epilogue(o, r, g):
    o = o * lax.rsqrt(jnp.mean(o * o, -1, keepdims=True) + RMS_EPS)
    return o * g * _silu(r.astype(F32))


def _pick_column(ref, n):
    lane = lax.broadcasted_iota(jnp.int32, ref.shape, 1)
    return jnp.sum(jnp.where(lane == n, ref[...], 0.0), axis=1, keepdims=True)


def _attn_sample_step(n, qt_ref, kt_ref, vt_ref, caches, news, o_cols):
    ngroups = len(DILATED_GROUPS)
    qcol = _pick_column(qt_ref, n)
    kcol = _pick_column(kt_ref, n)
    vcol = _pick_column(vt_ref, n)

    def shifted(x, new):
        lb = x.shape[1]
        lane = lax.broadcasted_iota(jnp.int32, x.shape, 1)
        return jnp.where(lane == lb - 1, new, pltpu.roll(x, lb - 1, 1))

    heads = []
    for h in range(HEADS_PER_GROUP):
        s_old, s_new = [], []
        for gi, (window, dil) in enumerate(DILATED_GROUPS):
            lb = caches[gi].shape[1]
            r0 = gi * GROUP_COLS + h * HEAD_DIM
            q1 = qcol[r0:r0 + HEAD_DIM]
            k1 = kcol[r0:r0 + HEAD_DIM]
            keys = caches[gi][h * HEAD_DIM:(h + 1) * HEAD_DIM, :]
            news[gi][h * HEAD_DIM:(h + 1) * HEAD_DIM, :] = shifted(keys, k1)
            s = jnp.sum(keys * q1, axis=0, keepdims=True)
            pos = lax.broadcasted_iota(jnp.int32, (1, lb), 1)
            s_old.append(jnp.where(pos % dil == 0, s, -jnp.inf))
            s_new.append(jnp.sum(k1 * q1, axis=0, keepdims=True))
        m = s_new[0]
        for gi in range(ngroups):
            m = jnp.maximum(m, jnp.maximum(s_new[gi], jnp.max(s_old[gi], axis=1, keepdims=True)))
        den = jnp.zeros((1, 1), F32)
        acc = jnp.zeros((HEAD_DIM, 1), F32)
        for gi in range(ngroups):
            r0 = gi * GROUP_COLS + h * HEAD_DIM
            v1 = vcol[r0:r0 + HEAD_DIM]
            p_old = jnp.exp(s_old[gi] - m)
            p_new = jnp.exp(s_new[gi] - m)
            den = den + jnp.sum(p_old, axis=1, keepdims=True) + p_new
            rows = slice(GROUP_COLS + h * HEAD_DIM, GROUP_COLS + (h + 1) * HEAD_DIM)
            vals = caches[gi][rows, :]
            news[gi][rows, :] = shifted(vals, v1)
            acc = acc + jnp.sum(vals * p_old, axis=1, keepdims=True) + v1 * p_new
        heads.append(acc / den)
    o_col = jnp.concatenate(heads, axis=0)
    lane = lax.broadcasted_iota(jnp.int32, o_cols.shape, 1)
    o_cols[...] = jnp.where(lane == n, o_col, o_cols[...])


def _post_kernel(x_ref, oa_ref, ob_ref, gates_ref, pe_ref,
                 wa_ref, wb_ref, wo_ref, wfg_ref, wfu_ref, wfd_ref, wpg_ref, wpp_ref,
                 ln1g_ref, ln1b_ref, ln2g_ref, ln2b_ref, *rest, sample_attn, skewed):
    rest = list(rest)
    if sample_attn:
        qt_ref, kt_ref, vt_ref, c1_ref, c2_ref, c3_ref = rest[:6]
        y_ref, n1_ref, n2_ref, n3_ref, oas_ref = rest[6:11]
        o_cols = rest[11]
        rest = rest[12:]
    else:
        y_ref = rest.pop(0)
    h_carry = rest.pop(0) if skewed else None
    i = pl.program_id(0)

    if sample_attn or skewed:
        @pl.when(i == 0)
        def _():
            if sample_attn:
                o_cols[...] = jnp.zeros_like(o_cols)
            if skewed:
                h_carry[...] = jnp.zeros_like(h_carry)

    if sample_attn:
        request = jnp.minimum(i, qt_ref.shape[1] - 1) if skewed else i
        _attn_sample_step(request, qt_ref, kt_ref, vt_ref, (c1_ref, c2_ref, c3_ref),
                          (n1_ref, n2_ref, n3_ref), o_cols)

    ga = gates_ref[:, 0:D_MODEL].astype(F32)
    gb = gates_ref[:, D_MODEL:2 * D_MODEL].astype(F32)
    merged = (ga * _dot(oa_ref[...].astype(BF16), wa_ref[...])
              + gb * _dot(ob_ref[...].astype(BF16), wb_ref[...]))
    h_new = DN_ALPHA * x_ref[...] + _dot(merged.astype(BF16), wo_ref[...])

    h = h_carry[...] if skewed else h_new
    x1 = _layernorm(h, ln1g_ref[...], ln1b_ref[...])
    x1b = x1.astype(BF16)
    act = _silu(_dot(x1b, wfg_ref[...])) * _dot(x1b, wfu_ref[...])
    x2 = _layernorm(DN_ALPHA * x1 + _dot(act.astype(BF16), wfd_ref[...]), ln2g_ref[...], ln2b_ref[...])
    gate = _sigmoid(_dot(x2.astype(BF16), wpg_ref[...]))
    y_ref[...] = x2 + gate * _dot(pe_ref[...].astype(BF16), wpp_ref[...])
    if skewed:
        h_carry[...] = h_new

    if sample_attn:
        @pl.when(i == pl.num_programs(0) - 1)
        def _():
            for c in range(GROUP_COLS // LANES):
                oas_ref[:, c * LANES:(c + 1) * LANES] = o_cols[c * LANES:(c + 1) * LANES, :].T


def _post(x, oa, ob, gates, pe, weights, *, tm, sample_attn=None):
    rows = x.shape[0]
    tiles = rows // tm
    skewed = tiles > 1
    steps = tiles + 1 if skewed else tiles
    stage1 = lambda w: pl.BlockSpec((tm, w), lambda i: (jnp.minimum(i, tiles - 1), 0))
    stage2 = lambda w: pl.BlockSpec((tm, w), lambda i: (jnp.maximum(i - 1, 0) if skewed else i, 0))
    in_specs = [stage1(D_MODEL), stage1(GROUP_COLS), stage1(GLA_V), stage1(2 * D_MODEL),
                stage2(PLE_DIM)] + [_const_spec(w.shape) for w in weights]
    out_specs = [stage2(D_MODEL)]
    out_shape = [jax.ShapeDtypeStruct((rows, D_MODEL), F32)]
    operands = [x, oa, ob, gates, pe, *weights]
    scratch = []
    if sample_attn is not None:
        qt, kt, vt, caches = sample_attn
        n = qt.shape[1]
        assert n == LANES == tiles, "one request per row tile"
        for (window, _), c in zip(DILATED_GROUPS, caches):
            assert c.shape[2] == window, "every key distance must fall inside the cache"
        cache_specs = [pl.BlockSpec((None,) + c.shape[1:], lambda i: (jnp.minimum(i, n - 1), 0, 0))
                       for c in caches]
        in_specs += [_const_spec(qt.shape)] * 3 + cache_specs
        out_specs += cache_specs + [pl.BlockSpec((n, GROUP_COLS), lambda i: (0, 0))]
        out_shape += [jax.ShapeDtypeStruct(c.shape, c.dtype) for c in caches]
        out_shape += [jax.ShapeDtypeStruct((n, GROUP_COLS), F32)]
        operands += [qt, kt, vt, *caches]
        scratch.append(pltpu.VMEM((GROUP_COLS, n), F32))
    if skewed:
        scratch.append(pltpu.VMEM((tm, D_MODEL), F32))
    carried = sample_attn is not None or skewed
    out = pl.pallas_call(
        functools.partial(_post_kernel, sample_attn=sample_attn is not None, skewed=skewed),
        grid=(steps,),
        in_specs=in_specs,
        out_specs=tuple(out_specs),
        out_shape=tuple(out_shape),
        scratch_shapes=scratch,
        compiler_params=pltpu.CompilerParams(
            dimension_semantics=("arbitrary" if carried else "parallel",),
            vmem_limit_bytes=VMEM_LIMIT),
        name="post_with_sample_attn" if sample_attn is not None else "post",
    )(*operands)
    return out if sample_attn is not None else out[0]


PROMPT_TM = 256


def _layer_weights(w_in, w_gate_up, b_gate, w_a_out, w_b_out, w_o, ln1_g, ln1_b,
                   w_ff_gate, w_ff_up, w_ff_down, ln2_g, ln2_b, w_ple_gate, w_ple_proj):
    proj = (w_in.T.astype(BF16), w_gate_up.astype(BF16), b_gate.reshape(1, GLA_K))
    row = lambda p: p.reshape(1, -1)
    post = (w_a_out.astype(BF16), w_b_out.astype(BF16), w_o.astype(BF16), w_ff_gate.astype(BF16),
            w_ff_up.astype(BF16), w_ff_down.astype(BF16), w_ple_gate.astype(BF16),
            w_ple_proj.astype(BF16), row(ln1_g), row(ln1_b), row(ln2_g), row(ln2_b))
    return proj, post


def _kv_tail(k, v, gi, keep):
    b, s, _ = k.shape
    cols = slice(gi * GROUP_COLS, (gi + 1) * GROUP_COLS)
    kt = k[:, s - keep:, cols].reshape(b, keep, HEADS_PER_GROUP, HEAD_DIM)
    vt = v[:, s - keep:, cols].reshape(b, keep, HEADS_PER_GROUP, HEAD_DIM)
    return jnp.stack([kt, vt], axis=2)


def _positions_minor(cache):
    n, lb = cache.shape[:2]
    return jnp.transpose(cache, (0, 2, 3, 4, 1)).reshape(n, 2 * GROUP_COLS, lb)


def _positions_major(cache):
    n, _, lb = cache.shape
    return jnp.transpose(cache.reshape(n, 2, HEADS_PER_GROUP, HEAD_DIM, lb), (0, 4, 1, 2, 3))


def kernel(x_prompt, x_sample, cache_a1_kv, cache_a2_kv, cache_a3_kv, state_gla, p_prompt, p_sample, w_in, w_gate_up, b_gate, gla_norm_g, w_a_out, w_b_out, w_o, ln1_g, ln1_b, w_ff_gate, w_ff_up, w_ff_down, ln2_g, ln2_b, w_ple_gate, w_ple_proj):
    depth = w_in.shape[0]
    assert depth == 1 and x_sample.shape[1] == 1
    b, s, d = x_prompt.shape
    n = x_sample.shape[0]
    proj_w, post_w = _layer_weights(
        w_in[0], w_gate_up[0], b_gate[0], w_a_out[0], w_b_out[0], w_o[0], ln1_g[0], ln1_b[0],
        w_ff_gate[0], w_ff_up[0], w_ff_down[0], ln2_g[0], ln2_b[0], w_ple_gate[0], w_ple_proj[0])
    norm_g = gla_norm_g[0].reshape(1, GLA_V)

    xs = x_sample.reshape(n, d)
    rope_s = _rope_table(jnp.full((n,), PAST_LEN, jnp.int32))
    qt, kt, vt, qgt, kgt, vg_s, rg_s, lat, gates_s = _in_proj(
        xs, rope_s, proj_w, tm=n, rope_period_tiles=1, sample=True)

    xp = x_prompt.reshape(b * s, d)
    rope_p = _rope_table(jnp.arange(s))
    q, k, v, gates, ob, st_p, ob_s, st_s = _in_proj(
        xp, rope_p, proj_w, tm=PROMPT_TM, rope_period_tiles=s // PROMPT_TM, sample=False,
        norm_g=norm_g, gla_sample=(qgt, kgt, lat, vg_s, rg_s, state_gla[0]))
    three = lambda t: t.reshape(b, s, t.shape[-1])
    k3, v3 = three(k), three(v)
    oa = _attn_prompt(three(q), k3, v3)
    kv_p = [_kv_tail(k3, v3, gi, min(w, s))[None] for gi, (w, _) in enumerate(DILATED_GROUPS)]

    caches = [_positions_minor(c[0]) for c in (cache_a1_kv, cache_a2_kv, cache_a3_kv)]
    y_p, *new_caches, oa_s = _post(
        xp, oa.reshape(b * s, GROUP_COLS), ob, gates,
        p_prompt[0].reshape(b * s, PLE_DIM), post_w, tm=b * s // n, sample_attn=(qt, kt, vt, caches))
    y_s = _post(xs, oa_s, ob_s, gates_s, p_sample[0].reshape(n, PLE_DIM), post_w, tm=n)
    kv_s = [_positions_major(c)[None] for c in new_caches]

    return (y_p.reshape(b, s, d), y_s.reshape(n, 1, d), kv_p[0], kv_p[1], kv_p[2], st_p[None],
            kv_s[0], kv_s[1], kv_s[2], st_s[None])
```

```python
import functools

import numpy as np
import jax
import jax.numpy as jnp
from jax import lax
from jax.experimental import pallas as pl
from jax.experimental.pallas import tpu as pltpu

F32 = jnp.float32
BF16 = jnp.bfloat16

D_MODEL = 1024
HEAD_DIM = 64
HEADS_PER_GROUP = 4
DILATED_GROUPS = ((128, 1), (512, 4), (2048, 16))
GROUP_COLS = HEADS_PER_GROUP * HEAD_DIM
A_WIDTH = len(DILATED_GROUPS) * GROUP_COLS
ROT_DIM = HEAD_DIM // 4
ROPE_THETA = 500000.0
KEYS_PER_WINDOW = 128
GLA_HEADS = 4
GLA_DK = 128
GLA_DV = 256
GLA_K = GLA_HEADS * GLA_DK
GLA_V = GLA_HEADS * GLA_DV
GATE_RANK = 16
GATE_NORMALIZER = 16.0
GLA_CHUNK = 128
D_FF = 2816
PLE_DIM = 256
DN_ALPHA = 2.0 ** 0.25
LN_EPS = 1e-5
RMS_EPS = 1e-6
PAST_LEN = 8192
COL_GLA = 3 * A_WIDTH
COL_LOWRANK = COL_GLA + 2 * GLA_K + 2 * GLA_V
COL_GATES = COL_LOWRANK + GATE_RANK
IN_COLS = COL_GATES + 2 * D_MODEL

LANES = 128
SUBLANES = 8
VMEM_LIMIT = 56 * 1024 * 1024

GLA_FACTORED_RANGE = 40.0


def _dot(a, b):
    return jnp.dot(a, b, preferred_element_type=F32)


def _dot_nt(a, b):
    return lax.dot_general(a, b, (((1,), (1,)), ((), ())), preferred_element_type=F32)


def _dot_tn(a, b):
    return lax.dot_general(a, b, (((0,), (0,)), ((), ())), preferred_element_type=F32)


def _const_spec(shape):
    return pl.BlockSpec(shape, lambda *_: (0,) * len(shape), pipeline_mode=pl.Buffered(1))


def _layernorm(x, g, b):
    xc = x - jnp.mean(x, -1, keepdims=True)
    var = jnp.mean(xc * xc, -1, keepdims=True)
    return xc * lax.rsqrt(var + LN_EPS) * g + b


def _sigmoid(x):
    return 1.0 / (1.0 + jnp.exp(-x))


def _silu(x):
    return x * _sigmoid(x)


def _split3_bf16(x):
    hi = x.astype(BF16)
    r1 = x - hi.astype(F32)
    mid = r1.astype(BF16)
    lo = (r1 - mid.astype(F32)).astype(BF16)
    return hi, mid, lo


def _gla_sample_step(n, j, qt_ref, kt_ref, lat_ref, v_ref, st_ref, st_out_ref, o_scr):
    a = jnp.exp(_pick_column(lat_ref, n))
    k = _pick_column(kt_ref, n)
    q = _pick_column(qt_ref, n)
    for h in range(GLA_HEADS):
        ks = slice(h * GLA_DK, (h + 1) * GLA_DK)
        vs = slice(h * GLA_DV, (h + 1) * GLA_DV)
        st = a[ks] * st_ref[j, h] + k[ks] * v_ref[pl.ds(n, 1), vs]
        st_out_ref[j, h] = st
        o_scr[pl.ds(n, 1), vs] = jnp.sum(q[ks] * st, axis=0, keepdims=True)


def _gla_chunk(q, k, v, cum, s_t, factored):
    c = q.shape[0]
    ri = lax.broadcasted_iota(jnp.int32, (c, c), 0)
    ci = lax.broadcasted_iota(jnp.int32, (c, c), 1)
    last = cum[c - 1:c]
    decay = jnp.exp(last)
    qe = (q * jnp.exp(cum)).astype(BF16)
    vb = v.astype(BF16)
    if factored:
        ke = k * jnp.exp(-cum)
        a = jnp.where(ci <= ri, _dot_nt(qe, ke.astype(BF16)), 0.0)
        kd = ke * decay
    else:
        row_id = lax.broadcasted_iota(jnp.int32, cum.shape, 0)

        def col(j, a):
            cj = jnp.sum(jnp.where(row_id == j, cum, 0.0), axis=0, keepdims=True)
            kj = jnp.sum(jnp.where(row_id == j, k, 0.0), axis=0, keepdims=True)
            e = jnp.exp(jnp.where(row_id >= j, cum - cj, -jnp.inf))
            sj = jnp.sum(q * kj * e, axis=-1, keepdims=True)
            return jnp.where(ci == j, sj, a)

        a = lax.fori_loop(0, c, col, jnp.zeros((c, c), F32))
        kd = k * jnp.exp(last - cum)
    o = _dot_nt(qe, s_t.astype(BF16)) + _dot(a.astype(BF16), vb)
    return o, s_t * decay + _dot_tn(vb, kd.astype(BF16))


def _in_proj_kernel(x_ref, rope_ref, w_ref, wup_ref, bg_ref, *rest, sample, tiles_per_seq):
    tm = x_ref.shape[0]
    wa_ref = w_ref.at[0:COL_GLA]
    wg_ref = w_ref.at[COL_GLA:COL_LOWRANK]
    wlr_ref = w_ref.at[COL_LOWRANK:COL_GATES]
    wgates_ref = w_ref.at[COL_GATES:IN_COLS]
    xb = x_ref[...].astype(BF16)
    cos = rope_ref[:, 0:LANES]
    sin_lo = rope_ref[:, LANES:2 * LANES]
    sin_hi = rope_ref[:, 2 * LANES:3 * LANES]

    def rope(t):
        return (t * cos + pltpu.roll(t, LANES - ROT_DIM // 2, 1) * sin_lo
                + pltpu.roll(t, ROT_DIM // 2, 1) * sin_hi)

    def put(ref, col, val):
        if sample:
            ref[col * LANES:(col + 1) * LANES, :] = val.T
        else:
            ref[:, col * LANES:(col + 1) * LANES] = val

    def branch_a_and_gates(q_ref, k_ref, v_ref, gates_ref, tails=None):
        tiles = GROUP_COLS // LANES
        for c in range(A_WIDTH // GROUP_COLS):
            lo, hi = c * GROUP_COLS, (c + 1) * GROUP_COLS
            qc = _dot_nt(xb, wa_ref[lo:hi, :])
            kc = _dot_nt(xb, wa_ref[A_WIDTH + lo:A_WIDTH + hi, :])
            vc = _dot_nt(xb, wa_ref[2 * A_WIDTH + lo:2 * A_WIDTH + hi, :])
            for s in range(tiles):
                sl = slice(s * LANES, (s + 1) * LANES)
                k_rot = rope(kc[:, sl])
                put(q_ref, c * tiles + s, rope(qc[:, sl]) * (HEAD_DIM ** -0.5))
                put(k_ref, c * tiles + s, k_rot)
                put(v_ref, c * tiles + s, vc[:, sl])
                if tails is not None:
                    width = tails[c].shape[-1]
                    tails[c][0, sl, :] = k_rot[tm - width:, :].T
                    tails[c][1, sl, :] = vc[tm - width:, sl].T
        gates_ref[...] = _sigmoid(_dot_nt(xb, wgates_ref[...])).astype(BF16)

    def gla_projections():
        qg = _dot_nt(xb, wg_ref[0:GLA_K, :]) * (GLA_DK ** -0.5)
        kg = _dot_nt(xb, wg_ref[GLA_K:2 * GLA_K, :])
        vg = _dot_nt(xb, wg_ref[2 * GLA_K:2 * GLA_K + GLA_V, :])
        rg = _dot_nt(xb, wg_ref[2 * GLA_K + GLA_V:2 * GLA_K + 2 * GLA_V, :])
        return qg, kg, vg, rg

    def log_decay(rows_bf16):
        glr = _dot_nt(rows_bf16, wlr_ref[...])
        z = _dot(glr.astype(BF16), wup_ref[...]) + bg_ref[...]
        return -(jnp.maximum(-z, 0.0) + jnp.log1p(jnp.exp(-jnp.abs(z)))) * (1.0 / GATE_NORMALIZER)

    if sample:
        loga = log_decay(xb)
        q_ref, k_ref, v_ref, qg_ref, kg_ref, vg_ref, rg_ref, la_ref, gates_ref = rest
        branch_a_and_gates(q_ref, k_ref, v_ref, gates_ref)
        qg, kg, vg, rg = gla_projections()
        for s in range(GLA_K // LANES):
            sl = slice(s * LANES, (s + 1) * LANES)
            put(qg_ref, s, qg[:, sl])
            put(kg_ref, s, kg[:, sl])
            put(la_ref, s, loga[:, sl])
        vg_ref[...] = vg
        rg_ref[...] = rg.astype(BF16)
        return

    (x_next_ref, gn_ref, qts_ref, kts_ref, lats_ref, vs_ref, rs_ref, st_ref,
     q_ref, k_ref, v_ref, gates_ref, ob_ref, stp_ref, obs_ref, st_out_ref, *tails,
     st_t, o_scr, cum_scr, range_scr) = rest
    step = pl.program_id(0)
    tile_in_seq = step % tiles_per_seq

    @pl.when(tile_in_seq == 0)
    def _():
        st_t[...] = jnp.zeros_like(st_t)

    def stage_decay(rows_bf16):
        row = lax.broadcasted_iota(jnp.int32, (tm, tm), 0)
        col = lax.broadcasted_iota(jnp.int32, (tm, tm), 1)
        tri = ((row // GLA_CHUNK == col // GLA_CHUNK) & (col <= row)).astype(BF16)
        hi, mid, lo = _split3_bf16(log_decay(rows_bf16))
        cum = _dot(tri, hi) + _dot(tri, mid) + _dot(tri, lo)
        cum_scr[...] = cum
        range_scr[0] = -jnp.min(cum)

    @pl.when(step == 0)
    def _():
        stage_decay(xb)

    def rest_of_tile(factored):
        cum = cum_scr[...]
        stage_decay(x_next_ref[...].astype(BF16))
        qg, kg, vg, rg = gla_projections()
        per_step = st_ref.shape[0]
        for j in range(per_step):
            _gla_sample_step(step * per_step + j, j, qts_ref, kts_ref, lats_ref, vs_ref,
                             st_ref, st_out_ref, o_scr)
        for ch in range(tm // GLA_CHUNK):
            rows = slice(ch * GLA_CHUNK, (ch + 1) * GLA_CHUNK)
            for h in range(GLA_HEADS):
                ks = slice(h * GLA_DK, (h + 1) * GLA_DK)
                vs = slice(h * GLA_DV, (h + 1) * GLA_DV)
                o, st_t[h] = _gla_chunk(qg[rows, ks], kg[rows, ks], vg[rows, vs], cum[rows, ks],
                                        st_t[h], factored)
                ob_ref[rows, vs] = _gla_epilogue(o, rg[rows, vs], gn_ref[:, vs]).astype(ob_ref.dtype)
        branch_a_and_gates(q_ref, k_ref, v_ref, gates_ref, tails)

    total = range_scr[0]
    pl.when(total < GLA_FACTORED_RANGE)(functools.partial(rest_of_tile, True))
    pl.when(total >= GLA_FACTORED_RANGE)(functools.partial(rest_of_tile, False))

    @pl.when(tile_in_seq == tiles_per_seq - 1)
    def _():
        for h in range(GLA_HEADS):
            stp_ref[h] = st_t[h].T

    @pl.when(step == pl.num_programs(0) - 1)
    def _():
        for h in range(GLA_HEADS):
            vs = slice(h * GLA_DV, (h + 1) * GLA_DV)
            obs_ref[:, vs] = _gla_epilogue(o_scr[:, vs], rs_ref[:, vs], gn_ref[:, vs]).astype(obs_ref.dtype)


def _in_proj(x, rope_tab, weights, *, tm, rope_period_tiles, sample, norm_g=None, gla_sample=None):
    rows = x.shape[0]
    steps = rows // tm
    w, wup, bg = weights
    row = lambda width: pl.BlockSpec((tm, width), lambda i: (i, 0))
    whole = lambda shape: pl.BlockSpec(shape, lambda i: (0,) * len(shape))
    in_specs = [row(D_MODEL), pl.BlockSpec((tm, 3 * LANES), lambda i: (i % rope_period_tiles, 0)),
                _const_spec(w.shape), _const_spec(wup.shape), _const_spec(bg.shape)]
    next_tile = pl.BlockSpec((tm, D_MODEL), lambda i: (jnp.minimum(i + 1, steps - 1), 0))
    operands = [x, rope_tab, w, wup, bg]
    scratch = []
    if sample:
        assert tm == rows == LANES
        t = lambda width: (jax.ShapeDtypeStruct((width, rows), F32), whole((width, rows)))
        outs = [t(A_WIDTH), t(A_WIDTH), t(A_WIDTH), t(GLA_K), t(GLA_K),
                (jax.ShapeDtypeStruct((rows, GLA_V), F32), row(GLA_V)),
                (jax.ShapeDtypeStruct((rows, GLA_V), BF16), row(GLA_V)),
                t(GLA_K),
                (jax.ShapeDtypeStruct((rows, 2 * D_MODEL), BF16), row(2 * D_MODEL))]
    else:
        qgt, kgt, lat, vg, rg, state = gla_sample
        n = state.shape[0]
        seqs = steps // rope_period_tiles
        assert n % steps == 0, "the same number of requests per grid step"
        assert tm % GLA_CHUNK == 0
        st_spec = pl.BlockSpec((n // steps,) + state.shape[1:], lambda i: (i, 0, 0, 0))
        in_specs += [next_tile] + [_const_spec(a.shape) for a in (norm_g, qgt, kgt, lat, vg, rg)] + [st_spec]
        operands += [x, norm_g, qgt, kgt, lat, vg, rg, state]
        outs = [(jax.ShapeDtypeStruct((rows, A_WIDTH), F32), row(A_WIDTH)),
                (jax.ShapeDtypeStruct((rows, A_WIDTH), F32), row(A_WIDTH)),
                (jax.ShapeDtypeStruct((rows, A_WIDTH), F32), row(A_WIDTH)),
                (jax.ShapeDtypeStruct((rows, 2 * D_MODEL), BF16), row(2 * D_MODEL)),
                (jax.ShapeDtypeStruct((rows, GLA_V), BF16), row(GLA_V)),
                (jax.ShapeDtypeStruct((seqs, GLA_HEADS, GLA_DK, GLA_DV), F32),
                 pl.BlockSpec((None, GLA_HEADS, GLA_DK, GLA_DV), lambda i: (i // rope_period_tiles, 0, 0, 0))),
                (jax.ShapeDtypeStruct((n, GLA_V), BF16), whole((n, GLA_V))),
                (jax.ShapeDtypeStruct(state.shape, F32), st_spec)]
        seq_len = rope_period_tiles * tm
        for window, _ in DILATED_GROUPS:
            keep = min(window, seq_len)
            width = min(keep, tm)
            first_kept = rope_period_tiles - keep // width
            outs.append((
                jax.ShapeDtypeStruct((seqs, 2, GROUP_COLS, keep), F32),
                pl.BlockSpec((None, 2, GROUP_COLS, width),
                             lambda i, first_kept=first_kept: (
                                 i // rope_period_tiles, 0, 0,
                                 jnp.maximum(i % rope_period_tiles - first_kept, 0)))))
        scratch = [pltpu.VMEM((GLA_HEADS, GLA_DV, GLA_DK), F32), pltpu.VMEM((n, GLA_V), F32),
                   pltpu.VMEM((tm, GLA_K), F32), pltpu.SMEM((1,), F32)]
    return pl.pallas_call(
        functools.partial(_in_proj_kernel, sample=sample, tiles_per_seq=rope_period_tiles),
        grid=(steps,),
        in_specs=in_specs,
        out_specs=tuple(o[1] for o in outs),
        out_shape=tuple(o[0] for o in outs),
        scratch_shapes=scratch,
        compiler_params=pltpu.CompilerParams(
            dimension_semantics=("parallel" if sample else "arbitrary",), vmem_limit_bytes=VMEM_LIMIT),
        name="in_proj_sample" if sample else "in_proj_prompt",
    )(*operands)


def _rope_table(pos):
    half = ROT_DIM // 2
    inv_freq = ROPE_THETA ** (-jnp.arange(0, ROT_DIM, 2, dtype=F32) / ROT_DIM)
    ang = pos.astype(F32)[:, None] * inv_freq[None, :]
    cos, sin = jnp.cos(ang), jnp.sin(ang)
    lane = np.arange(LANES) % HEAD_DIM
    fidx = lane % half
    first = lane < half
    second = (lane >= half) & (lane < ROT_DIM)
    cos_t = jnp.where((first | second)[None, :], cos[:, fidx], 1.0)
    sin_lo = jnp.where(first[None, :], -sin[:, fidx], 0.0)
    sin_hi = jnp.where(second[None, :], sin[:, fidx], 0.0)
    return jnp.concatenate([cos_t, sin_lo, sin_hi], axis=1)


ATT_TILE = 2048
Q_BLOCK = 128
HEADS_PER_TILE = LANES // HEAD_DIM


def _attn_prompt_kernel(q_ref, kc_ref, kp_ref, vc_ref, vp_ref, o_ref, og, lg, bias):
    t = pl.program_id(1)
    g = pl.program_id(3)
    band_keys = 2 * Q_BLOCK
    qi = lax.broadcasted_iota(jnp.int32, (Q_BLOCK, HEADS_PER_TILE * band_keys), 0)
    kc = lax.broadcasted_iota(jnp.int32, (Q_BLOCK, HEADS_PER_TILE * band_keys), 1) % band_keys
    back = qi + Q_BLOCK - kc
    band = (back >= 0) & (back <= KEYS_PER_WINDOW)
    bias[0] = jnp.where(band, 0.0, -jnp.inf)
    bias[1] = jnp.where(band & (kc >= Q_BLOCK), 0.0, -jnp.inf)
    head0 = lax.broadcasted_iota(jnp.int32, (band_keys, LANES), 1) < HEAD_DIM
    head0_q = lax.broadcasted_iota(jnp.int32, (Q_BLOCK, LANES), 1) < HEAD_DIM

    for gi, (window, dil) in enumerate(DILATED_GROUPS):

        @pl.when(g == gi)
        def _(gi=gi, window=window, dil=dil):
            nblk = ATT_TILE // dil // Q_BLOCK

            def rows(start):
                return pl.ds(start, Q_BLOCK) if dil == 1 else pl.ds(start, Q_BLOCK, stride=dil)

            def block(r, n):
                base = r + dil * Q_BLOCK * n
                q2 = q_ref[rows(base), :].astype(BF16)
                if n == 0:
                    prev = rows(ATT_TILE - window + r)
                    k_prev, v_prev = kp_ref[prev, :], vp_ref[prev, :]
                    mask = bias[(t == 0).astype(jnp.int32)]
                else:
                    prev = rows(base - dil * Q_BLOCK)
                    k_prev, v_prev = kc_ref[prev, :], vc_ref[prev, :]
                    mask = bias[0]
                k2 = jnp.concatenate([k_prev, kc_ref[rows(base), :]], axis=0)
                v2 = jnp.concatenate([v_prev, vc_ref[rows(base), :]], axis=0)
                kz = jnp.concatenate([jnp.where(head0, k2, 0.0), jnp.where(head0, 0.0, k2)], axis=0)
                s = _dot_nt(q2, kz.astype(BF16)) + mask
                s0, s1 = s[:, :band_keys], s[:, band_keys:]
                m0 = jnp.max(s0, axis=-1, keepdims=True)
                m1 = jnp.max(s1, axis=-1, keepdims=True)
                p = jnp.concatenate([jnp.exp(s0 - m0), jnp.exp(s1 - m1)], axis=1).astype(BF16)
                ones0 = head0.astype(F32)
                vz = jnp.concatenate([
                    jnp.concatenate([jnp.where(head0, v2, 0.0), ones0], axis=1),
                    jnp.concatenate([jnp.where(head0, 0.0, v2), 1.0 - ones0], axis=1)], axis=0)
                res = _dot(p, vz.astype(BF16))
                den = res[:, LANES:]
                og[gi, rows(base), :] = res[:, :LANES] / den
                m = jnp.where(head0_q, jnp.broadcast_to(m0, den.shape), jnp.broadcast_to(m1, den.shape))
                lg[gi, rows(base), :] = m + jnp.log(den)

            for r in range(dil):
                for n in range(nblk):
                    block(r, n)

    @pl.when(g == len(DILATED_GROUPS) - 1)
    def _():
        rows = 256

        def combine(c, carry):
            r0 = pl.multiple_of(c * rows, rows)
            l0 = lg[0, pl.ds(r0, rows), :]
            l1 = lg[1, pl.ds(r0, rows), :]
            l2 = lg[2, pl.ds(r0, rows), :]
            mx = jnp.maximum(jnp.maximum(l0, l1), l2)
            e0, e1, e2 = jnp.exp(l0 - mx), jnp.exp(l1 - mx), jnp.exp(l2 - mx)
            acc = (e0 * og[0, pl.ds(r0, rows), :] + e1 * og[1, pl.ds(r0, rows), :]
                   + e2 * og[2, pl.ds(r0, rows), :])
            o_ref[pl.ds(r0, rows), :] = (acc / (e0 + e1 + e2)).astype(o_ref.dtype)
            return carry

        lax.fori_loop(0, ATT_TILE // rows, combine, 0)


def _attn_prompt(q, k, v):
    b, s, _ = q.shape
    tiles_per_group = GROUP_COLS // LANES
    cur = pl.BlockSpec((None, ATT_TILE, LANES), lambda bi, t, c, g: (bi, t, g * tiles_per_group + c))
    prev = pl.BlockSpec((None, ATT_TILE, LANES),
                        lambda bi, t, c, g: (bi, jnp.maximum(t - 1, 0), g * tiles_per_group + c))
    assert all(w <= ATT_TILE for w, _ in DILATED_GROUPS), "the key halo must fit in the previous tile"
    ngroups = len(DILATED_GROUPS)
    return pl.pallas_call(
        _attn_prompt_kernel,
        grid=(b, s // ATT_TILE, tiles_per_group, ngroups),
        in_specs=[cur, cur, prev, cur, prev],
        out_specs=pl.BlockSpec((None, ATT_TILE, LANES), lambda bi, t, c, g: (bi, t, c)),
        out_shape=jax.ShapeDtypeStruct((b, s, GROUP_COLS), BF16),
        scratch_shapes=[
            pltpu.VMEM((ngroups, ATT_TILE, LANES), F32),
            pltpu.VMEM((ngroups, ATT_TILE, LANES), F32),
            pltpu.VMEM((2, Q_BLOCK, HEADS_PER_TILE * 2 * Q_BLOCK), F32),
        ],
        compiler_params=pltpu.CompilerParams(
            dimension_semantics=("parallel", "parallel", "parallel", "arbitrary"),
            vmem_limit_bytes=VMEM_LIMIT),
        name="attn_prompt",
    )(q, k, k, v, v)


def _gla_epilogue(o, r, g):
    o = o * lax.rsqrt(jnp.mean(o * o, -1, keepdims=True) + RMS_EPS)
    return o * g * _silu(r.astype(F32))


def _pick_column(ref, n):
    lane = lax.broadcasted_iota(jnp.int32, ref.shape, 1)
    return jnp.sum(jnp.where(lane == n, ref[...], 0.0), axis=1, keepdims=True)


def _attn_sample_step(n, qt_ref, kt_ref, vt_ref, caches, news, o_cols):
    ngroups = len(DILATED_GROUPS)
    qcol = _pick_column(qt_ref, n)
    kcol = _pick_column(kt_ref, n)
    vcol = _pick_column(vt_ref, n)

    def shifted(x, new):
        lb = x.shape[1]
        lane = lax.broadcasted_iota(jnp.int32, x.shape, 1)
        return jnp.where(lane == lb - 1, new, pltpu.roll(x, lb - 1, 1))

    heads = []
    for h in range(HEADS_PER_GROUP):
        s_old, s_new = [], []
        for gi, (window, dil) in enumerate(DILATED_GROUPS):
            lb = caches[gi].shape[1]
            r0 = gi * GROUP_COLS + h * HEAD_DIM
            q1 = qcol[r0:r0 + HEAD_DIM]
            k1 = kcol[r0:r0 + HEAD_DIM]
            keys = caches[gi][h * HEAD_DIM:(h + 1) * HEAD_DIM, :]
            news[gi][h * HEAD_DIM:(h + 1) * HEAD_DIM, :] = shifted(keys, k1)
            s = jnp.sum(keys * q1, axis=0, keepdims=True)
            pos = lax.broadcasted_iota(jnp.int32, (1, lb), 1)
            s_old.append(jnp.where(pos % dil == 0, s, -jnp.inf))
            s_new.append(jnp.sum(k1 * q1, axis=0, keepdims=True))
        m = s_new[0]
        for gi in range(ngroups):
            m = jnp.maximum(m, jnp.maximum(s_new[gi], jnp.max(s_old[gi], axis=1, keepdims=True)))
        den = jnp.zeros((1, 1), F32)
        acc = jnp.zeros((HEAD_DIM, 1), F32)
        for gi in range(ngroups):
            r0 = gi * GROUP_COLS + h * HEAD_DIM
            v1 = vcol[r0:r0 + HEAD_DIM]
            p_old = jnp.exp(s_old[gi] - m)
            p_new = jnp.exp(s_new[gi] - m)
            den = den + jnp.sum(p_old, axis=1, keepdims=True) + p_new
            rows = slice(GROUP_COLS + h * HEAD_DIM, GROUP_COLS + (h + 1) * HEAD_DIM)
            vals = caches[gi][rows, :]
            news[gi][rows, :] = shifted(vals, v1)
            acc = acc + jnp.sum(vals * p_old, axis=1, keepdims=True) + v1 * p_new
        heads.append(acc / den)
    o_col = jnp.concatenate(heads, axis=0)
    lane = lax.broadcasted_iota(jnp.int32, o_cols.shape, 1)
    o_cols[...] = jnp.where(lane == n, o_col, o_cols[...])


def _post_kernel(x_ref, oa_ref, ob_ref, gates_ref, pe_ref,
                 wa_ref, wb_ref, wo_ref, wfg_ref, wfu_ref, wfd_ref, wpg_ref, wpp_ref,
                 ln1g_ref, ln1b_ref, ln2g_ref, ln2b_ref, *rest, sample_attn, skewed):
    rest = list(rest)
    if sample_attn:
        qt_ref, kt_ref, vt_ref, c1_ref, c2_ref, c3_ref = rest[:6]
        y_ref, n1_ref, n2_ref, n3_ref, oas_ref = rest[6:11]
        o_cols = rest[11]
        rest = rest[12:]
    else:
        y_ref = rest.pop(0)
    h_carry = rest.pop(0) if skewed else None
    i = pl.program_id(0)

    if sample_attn or skewed:
        @pl.when(i == 0)
        def _():
            if sample_attn:
                o_cols[...] = jnp.zeros_like(o_cols)
            if skewed:
                h_carry[...] = jnp.zeros_like(h_carry)

    if sample_attn:
        request = jnp.minimum(i, qt_ref.shape[1] - 1) if skewed else i
        _attn_sample_step(request, qt_ref, kt_ref, vt_ref, (c1_ref, c2_ref, c3_ref),
                          (n1_ref, n2_ref, n3_ref), o_cols)

    ga = gates_ref[:, 0:D_MODEL].astype(F32)
    gb = gates_ref[:, D_MODEL:2 * D_MODEL].astype(F32)
    merged = (ga * _dot(oa_ref[...].astype(BF16), wa_ref[...])
              + gb * _dot(ob_ref[...].astype(BF16), wb_ref[...]))
    h_new = DN_ALPHA * x_ref[...] + _dot(merged.astype(BF16), wo_ref[...])

    h = h_carry[...] if skewed else h_new
    x1 = _layernorm(h, ln1g_ref[...], ln1b_ref[...])
    x1b = x1.astype(BF16)
    act = _silu(_dot(x1b, wfg_ref[...])) * _dot(x1b, wfu_ref[...])
    x2 = _layernorm(DN_ALPHA * x1 + _dot(act.astype(BF16), wfd_ref[...]), ln2g_ref[...], ln2b_ref[...])
    gate = _sigmoid(_dot(x2.astype(BF16), wpg_ref[...]))
    y_ref[...] = x2 + gate * _dot(pe_ref[...].astype(BF16), wpp_ref[...])
    if skewed:
        h_carry[...] = h_new

    if sample_attn:
        @pl.when(i == pl.num_programs(0) - 1)
        def _():
            for c in range(GROUP_COLS // LANES):
                oas_ref[:, c * LANES:(c + 1) * LANES] = o_cols[c * LANES:(c + 1) * LANES, :].T


def _post(x, oa, ob, gates, pe, weights, *, tm, sample_attn=None):
    rows = x.shape[0]
    tiles = rows // tm
    skewed = tiles > 1
    steps = tiles + 1 if skewed else tiles
    stage1 = lambda w: pl.BlockSpec((tm, w), lambda i: (jnp.minimum(i, tiles - 1), 0))
    stage2 = lambda w: pl.BlockSpec((tm, w), lambda i: (jnp.maximum(i - 1, 0) if skewed else i, 0))
    in_specs = [stage1(D_MODEL), stage1(GROUP_COLS), stage1(GLA_V), stage1(2 * D_MODEL),
                stage2(PLE_DIM)] + [_const_spec(w.shape) for w in weights]
    out_specs = [stage2(D_MODEL)]
    out_shape = [jax.ShapeDtypeStruct((rows, D_MODEL), F32)]
    operands = [x, oa, ob, gates, pe, *weights]
    scratch = []
    if sample_attn is not None:
        qt, kt, vt, caches = sample_attn
        n = qt.shape[1]
        assert n == LANES == tiles, "one request per row tile"
        for (window, _), c in zip(DILATED_GROUPS, caches):
            assert c.shape[2] == window, "every key distance must fall inside the cache"
        cache_specs = [pl.BlockSpec((None,) + c.shape[1:], lambda i: (jnp.minimum(i, n - 1), 0, 0))
                       for c in caches]
        in_specs += [_const_spec(qt.shape)] * 3 + cache_specs
        out_specs += cache_specs + [pl.BlockSpec((n, GROUP_COLS), lambda i: (0, 0))]
        out_shape += [jax.ShapeDtypeStruct(c.shape, c.dtype) for c in caches]
        out_shape += [jax.ShapeDtypeStruct((n, GROUP_COLS), F32)]
        operands += [qt, kt, vt, *caches]
        scratch.append(pltpu.VMEM((GROUP_COLS, n), F32))
    if skewed:
        scratch.append(pltpu.VMEM((tm, D_MODEL), F32))
    carried = sample_attn is not None or skewed
    out = pl.pallas_call(
        functools.partial(_post_kernel, sample_attn=sample_attn is not None, skewed=skewed),
        grid=(steps,),
        in_specs=in_specs,
        out_specs=tuple(out_specs),
        out_shape=tuple(out_shape),
        scratch_shapes=scratch,
        compiler_params=pltpu.CompilerParams(
            dimension_semantics=("arbitrary" if carried else "parallel",),
            vmem_limit_bytes=VMEM_LIMIT),
        name="post_with_sample_attn" if sample_attn is not None else "post",
    )(*operands)
    return out if sample_attn is not None else out[0]


PROMPT_TM = 256


def _layer_weights(w_in, w_gate_up, b_gate, w_a_out, w_b_out, w_o, ln1_g, ln1_b,
                   w_ff_gate, w_ff_up, w_ff_down, ln2_g, ln2_b, w_ple_gate, w_ple_proj):
    proj = (w_in.T.astype(BF16), w_gate_up.astype(BF16), b_gate.reshape(1, GLA_K))
    row = lambda p: p.reshape(1, -1)
    post = (w_a_out.astype(BF16), w_b_out.astype(BF16), w_o.astype(BF16), w_ff_gate.astype(BF16),
            w_ff_up.astype(BF16), w_ff_down.astype(BF16), w_ple_gate.astype(BF16),
            w_ple_proj.astype(BF16), row(ln1_g), row(ln1_b), row(ln2_g), row(ln2_b))
    return proj, post


def _positions_minor(cache):
    n, lb = cache.shape[:2]
    return jnp.transpose(cache, (0, 2, 3, 4, 1)).reshape(n, 2 * GROUP_COLS, lb)


def _positions_major(cache):
    n, lb = cache.shape[0], cache.shape[-1]
    return jnp.transpose(cache.reshape(n, 2, HEADS_PER_GROUP, HEAD_DIM, lb), (0, 4, 1, 2, 3))


def kernel(x_prompt, x_sample, cache_a1_kv, cache_a2_kv, cache_a3_kv, state_gla, p_prompt, p_sample, w_in, w_gate_up, b_gate, gla_norm_g, w_a_out, w_b_out, w_o, ln1_g, ln1_b, w_ff_gate, w_ff_up, w_ff_down, ln2_g, ln2_b, w_ple_gate, w_ple_proj):
    depth = w_in.shape[0]
    assert depth == 1 and x_sample.shape[1] == 1
    b, s, d = x_prompt.shape
    n = x_sample.shape[0]
    proj_w, post_w = _layer_weights(
        w_in[0], w_gate_up[0], b_gate[0], w_a_out[0], w_b_out[0], w_o[0], ln1_g[0], ln1_b[0],
        w_ff_gate[0], w_ff_up[0], w_ff_down[0], ln2_g[0], ln2_b[0], w_ple_gate[0], w_ple_proj[0])
    norm_g = gla_norm_g[0].reshape(1, GLA_V)

    xs = x_sample.reshape(n, d)
    rope_s = _rope_table(jnp.full((n,), PAST_LEN, jnp.int32))
    qt, kt, vt, qgt, kgt, vg_s, rg_s, lat, gates_s = _in_proj(
        xs, rope_s, proj_w, tm=n, rope_period_tiles=1, sample=True)

    xp = x_prompt.reshape(b * s, d)
    rope_p = _rope_table(jnp.arange(s))
    q, k, v, gates, ob, st_p, ob_s, st_s, *kv_tails = _in_proj(
        xp, rope_p, proj_w, tm=PROMPT_TM, rope_period_tiles=s // PROMPT_TM, sample=False,
        norm_g=norm_g, gla_sample=(qgt, kgt, lat, vg_s, rg_s, state_gla[0]))
    three = lambda t: t.reshape(b, s, t.shape[-1])
    oa = _attn_prompt(three(q), three(k), three(v))
    kv_p = [_positions_major(t)[None] for t in kv_tails]

    caches = [_positions_minor(c[0]) for c in (cache_a1_kv, cache_a2_kv, cache_a3_kv)]
    y_p, *new_caches, oa_s = _post(
        xp, oa.reshape(b * s, GROUP_COLS), ob, gates,
        p_prompt[0].reshape(b * s, PLE_DIM), post_w, tm=b * s // n, sample_attn=(qt, kt, vt, caches))
    y_s = _post(xs, oa_s, ob_s, gates_s, p_sample[0].reshape(n, PLE_DIM), post_w, tm=n)
    kv_s = [_positions_major(c)[None] for c in new_caches]

    return (y_p.reshape(b, s, d), y_s.reshape(n, 1, d), kv_p[0], kv_p[1], kv_p[2], st_p[None],
            kv_s[0], kv_s[1], kv_s[2], st_s[None])
```

```python
import functools

import numpy as np
import jax
import jax.numpy as jnp
from jax import lax
from jax.experimental import pallas as pl
from jax.experimental.pallas import tpu as pltpu

F32 = jnp.float32
BF16 = jnp.bfloat16

D_MODEL = 1024
HEAD_DIM = 64
HEADS_PER_GROUP = 4
DILATED_GROUPS = ((128, 1), (512, 4), (2048, 16))
GROUP_COLS = HEADS_PER_GROUP * HEAD_DIM
A_WIDTH = len(DILATED_GROUPS) * GROUP_COLS
ROT_DIM = HEAD_DIM // 4
ROPE_THETA = 500000.0
KEYS_PER_WINDOW = 128
GLA_HEADS = 4
GLA_DK = 128
GLA_DV = 256
GLA_K = GLA_HEADS * GLA_DK
GLA_V = GLA_HEADS * GLA_DV
GATE_RANK = 16
GATE_NORMALIZER = 16.0
GLA_CHUNK = 128
D_FF = 2816
PLE_DIM = 256
DN_ALPHA = 2.0 ** 0.25
LN_EPS = 1e-5
RMS_EPS = 1e-6
PAST_LEN = 8192
COL_GLA = 3 * A_WIDTH
COL_LOWRANK = COL_GLA + 2 * GLA_K + 2 * GLA_V
COL_GATES = COL_LOWRANK + GATE_RANK
IN_COLS = COL_GATES + 2 * D_MODEL

LANES = 128
SUBLANES = 8
VMEM_LIMIT = 56 * 1024 * 1024

GLA_FACTORED_RANGE = 40.0


def _dot(a, b):
    return jnp.dot(a, b, preferred_element_type=F32)


def _dot_nt(a, b):
    return lax.dot_general(a, b, (((1,), (1,)), ((), ())), preferred_element_type=F32)


def _dot_tn(a, b):
    return lax.dot_general(a, b, (((0,), (0,)), ((), ())), preferred_element_type=F32)


def _const_spec(shape):
    return pl.BlockSpec(shape, lambda *_: (0,) * len(shape), pipeline_mode=pl.Buffered(1))


def _layernorm(x, g, b):
    xc = x - jnp.mean(x, -1, keepdims=True)
    var = jnp.mean(xc * xc, -1, keepdims=True)
    return xc * lax.rsqrt(var + LN_EPS) * g + b


def _sigmoid(x):
    return 1.0 / (1.0 + jnp.exp(-x))


def _silu(x):
    return x * _sigmoid(x)


def _split3_bf16(x):
    hi = x.astype(BF16)
    r1 = x - hi.astype(F32)
    mid = r1.astype(BF16)
    lo = (r1 - mid.astype(F32)).astype(BF16)
    return hi, mid, lo


def _gla_sample_step(n, j, qt_ref, kt_ref, lat_ref, v_ref, st_ref, st_out_ref, o_scr):
    a = jnp.exp(_pick_column(lat_ref, n))
    k = _pick_column(kt_ref, n)
    q = _pick_column(qt_ref, n)
    for h in range(GLA_HEADS):
        ks = slice(h * GLA_DK, (h + 1) * GLA_DK)
        vs = slice(h * GLA_DV, (h + 1) * GLA_DV)
        st = a[ks] * st_ref[j, h] + k[ks] * v_ref[pl.ds(n, 1), vs]
        st_out_ref[j, h] = st
        o_scr[pl.ds(n, 1), vs] = jnp.sum(q[ks] * st, axis=0, keepdims=True)


def _gla_chunk(q, k, v, cum, s_t, factored):
    c = q.shape[0]
    ri = lax.broadcasted_iota(jnp.int32, (c, c), 0)
    ci = lax.broadcasted_iota(jnp.int32, (c, c), 1)
    last = cum[c - 1:c]
    decay = jnp.exp(last)
    qe = (q * jnp.exp(cum)).astype(BF16)
    vb = v.astype(BF16)
    if factored:
        ke = k * jnp.exp(-cum)
        a = jnp.where(ci <= ri, _dot_nt(qe, ke.astype(BF16)), 0.0)
        kd = ke * decay
    else:
        row_id = lax.broadcasted_iota(jnp.int32, cum.shape, 0)

        def col(j, a):
            cj = jnp.sum(jnp.where(row_id == j, cum, 0.0), axis=0, keepdims=True)
            kj = jnp.sum(jnp.where(row_id == j, k, 0.0), axis=0, keepdims=True)
            e = jnp.exp(jnp.where(row_id >= j, cum - cj, -jnp.inf))
            sj = jnp.sum(q * kj * e, axis=-1, keepdims=True)
            return jnp.where(ci == j, sj, a)

        a = lax.fori_loop(0, c, col, jnp.zeros((c, c), F32))
        kd = k * jnp.exp(last - cum)
    o = _dot_nt(qe, s_t.astype(BF16)) + _dot(a.astype(BF16), vb)
    return o, s_t * decay + _dot_tn(vb, kd.astype(BF16))


def _in_proj_kernel(x_ref, rope_ref, w_ref, wup_ref, bg_ref, *rest, sample, tiles_per_seq):
    tm = x_ref.shape[0]
    wa_ref = w_ref.at[0:COL_GLA]
    wg_ref = w_ref.at[COL_GLA:COL_LOWRANK]
    wlr_ref = w_ref.at[COL_LOWRANK:COL_GATES]
    wgates_ref = w_ref.at[COL_GATES:IN_COLS]
    xb = x_ref[...].astype(BF16)
    cos = rope_ref[:, 0:LANES]
    sin_lo = rope_ref[:, LANES:2 * LANES]
    sin_hi = rope_ref[:, 2 * LANES:3 * LANES]

    def rope(t):
        return (t * cos + pltpu.roll(t, LANES - ROT_DIM // 2, 1) * sin_lo
                + pltpu.roll(t, ROT_DIM // 2, 1) * sin_hi)

    def put(ref, col, val):
        if sample:
            ref[col * LANES:(col + 1) * LANES, :] = val.T
        else:
            ref[:, col * LANES:(col + 1) * LANES] = val

    def branch_a_and_gates(q_ref, k_ref, v_ref, gates_ref, tails=None):
        tiles = GROUP_COLS // LANES
        for c in range(A_WIDTH // GROUP_COLS):
            lo, hi = c * GROUP_COLS, (c + 1) * GROUP_COLS
            qc = _dot_nt(xb, wa_ref[lo:hi, :])
            kc = _dot_nt(xb, wa_ref[A_WIDTH + lo:A_WIDTH + hi, :])
            vc = _dot_nt(xb, wa_ref[2 * A_WIDTH + lo:2 * A_WIDTH + hi, :])
            for s in range(tiles):
                sl = slice(s * LANES, (s + 1) * LANES)
                k_rot = rope(kc[:, sl])
                put(q_ref, c * tiles + s, rope(qc[:, sl]) * (HEAD_DIM ** -0.5))
                put(k_ref, c * tiles + s, k_rot)
                put(v_ref, c * tiles + s, vc[:, sl])
                if tails is not None:
                    width = tails[c].shape[-1]
                    tails[c][0, sl, :] = k_rot[tm - width:, :].T
                    tails[c][1, sl, :] = vc[tm - width:, sl].T
        gates_ref[...] = _sigmoid(_dot_nt(xb, wgates_ref[...])).astype(BF16)

    def gla_projections():
        qg = _dot_nt(xb, wg_ref[0:GLA_K, :]) * (GLA_DK ** -0.5)
        kg = _dot_nt(xb, wg_ref[GLA_K:2 * GLA_K, :])
        vg = _dot_nt(xb, wg_ref[2 * GLA_K:2 * GLA_K + GLA_V, :])
        rg = _dot_nt(xb, wg_ref[2 * GLA_K + GLA_V:2 * GLA_K + 2 * GLA_V, :])
        return qg, kg, vg, rg

    def log_decay(rows_bf16):
        glr = _dot_nt(rows_bf16, wlr_ref[...])
        z = _dot(glr.astype(BF16), wup_ref[...]) + bg_ref[...]
        return -(jnp.maximum(-z, 0.0) + jnp.log1p(jnp.exp(-jnp.abs(z)))) * (1.0 / GATE_NORMALIZER)

    if sample:
        loga = log_decay(xb)
        q_ref, k_ref, v_ref, qg_ref, kg_ref, vg_ref, rg_ref, la_ref, gates_ref = rest
        branch_a_and_gates(q_ref, k_ref, v_ref, gates_ref)
        qg, kg, vg, rg = gla_projections()
        for s in range(GLA_K // LANES):
            sl = slice(s * LANES, (s + 1) * LANES)
            put(qg_ref, s, qg[:, sl])
            put(kg_ref, s, kg[:, sl])
            put(la_ref, s, loga[:, sl])
        vg_ref[...] = vg
        rg_ref[...] = rg.astype(BF16)
        return

    (x_next_ref, gn_ref, qts_ref, kts_ref, lats_ref, vs_ref, rs_ref, st_ref,
     q_ref, k_ref, v_ref, gates_ref, ob_ref, stp_ref, obs_ref, st_out_ref, *tails,
     st_t, o_scr, cum_scr, range_scr) = rest
    step = pl.program_id(0)
    tile_in_seq = step % tiles_per_seq

    @pl.when(tile_in_seq == 0)
    def _():
        st_t[...] = jnp.zeros_like(st_t)

    def stage_decay(rows_bf16):
        row = lax.broadcasted_iota(jnp.int32, (tm, tm), 0)
        col = lax.broadcasted_iota(jnp.int32, (tm, tm), 1)
        tri = ((row // GLA_CHUNK == col // GLA_CHUNK) & (col <= row)).astype(BF16)
        hi, mid, lo = _split3_bf16(log_decay(rows_bf16))
        cum = _dot(tri, hi) + _dot(tri, mid) + _dot(tri, lo)
        cum_scr[...] = cum
        range_scr[0] = -jnp.min(cum)

    @pl.when(step == 0)
    def _():
        stage_decay(xb)

    def rest_of_tile(factored):
        cum = cum_scr[...]
        stage_decay(x_next_ref[...].astype(BF16))
        qg, kg, vg, rg = gla_projections()
        per_step = st_ref.shape[0]
        for j in range(per_step):
            _gla_sample_step(step * per_step + j, j, qts_ref, kts_ref, lats_ref, vs_ref,
                             st_ref, st_out_ref, o_scr)
        for ch in range(tm // GLA_CHUNK):
            rows = slice(ch * GLA_CHUNK, (ch + 1) * GLA_CHUNK)
            for h in range(GLA_HEADS):
                ks = slice(h * GLA_DK, (h + 1) * GLA_DK)
                vs = slice(h * GLA_DV, (h + 1) * GLA_DV)
                o, st_t[h] = _gla_chunk(qg[rows, ks], kg[rows, ks], vg[rows, vs], cum[rows, ks],
                                        st_t[h], factored)
                ob_ref[rows, vs] = _gla_epilogue(o, rg[rows, vs], gn_ref[:, vs]).astype(ob_ref.dtype)
        branch_a_and_gates(q_ref, k_ref, v_ref, gates_ref, tails)

    total = range_scr[0]
    pl.when(total < GLA_FACTORED_RANGE)(functools.partial(rest_of_tile, True))
    pl.when(total >= GLA_FACTORED_RANGE)(functools.partial(rest_of_tile, False))

    @pl.when(tile_in_seq == tiles_per_seq - 1)
    def _():
        for h in range(GLA_HEADS):
            stp_ref[h] = st_t[h].T

    @pl.when(step == pl.num_programs(0) - 1)
    def _():
        for h in range(GLA_HEADS):
            vs = slice(h * GLA_DV, (h + 1) * GLA_DV)
            obs_ref[:, vs] = _gla_epilogue(o_scr[:, vs], rs_ref[:, vs], gn_ref[:, vs]).astype(obs_ref.dtype)


def _in_proj(x, rope_tab, weights, *, tm, rope_period_tiles, sample, norm_g=None, gla_sample=None):
    rows = x.shape[0]
    steps = rows // tm
    w, wup, bg = weights
    row = lambda width: pl.BlockSpec((tm, width), lambda i: (i, 0))
    whole = lambda shape: pl.BlockSpec(shape, lambda i: (0,) * len(shape))
    in_specs = [row(D_MODEL), pl.BlockSpec((tm, 3 * LANES), lambda i: (i % rope_period_tiles, 0)),
                _const_spec(w.shape), _const_spec(wup.shape), _const_spec(bg.shape)]
    next_tile = pl.BlockSpec((tm, D_MODEL), lambda i: (jnp.minimum(i + 1, steps - 1), 0))
    operands = [x, rope_tab, w, wup, bg]
    scratch = []
    if sample:
        assert tm == rows == LANES
        t = lambda width: (jax.ShapeDtypeStruct((width, rows), F32), whole((width, rows)))
        outs = [t(A_WIDTH), t(A_WIDTH), t(A_WIDTH), t(GLA_K), t(GLA_K),
                (jax.ShapeDtypeStruct((rows, GLA_V), F32), row(GLA_V)),
                (jax.ShapeDtypeStruct((rows, GLA_V), BF16), row(GLA_V)),
                t(GLA_K),
                (jax.ShapeDtypeStruct((rows, 2 * D_MODEL), BF16), row(2 * D_MODEL))]
    else:
        qgt, kgt, lat, vg, rg, state = gla_sample
        n = state.shape[0]
        seqs = steps // rope_period_tiles
        assert n % steps == 0, "the same number of requests per grid step"
        assert tm % GLA_CHUNK == 0
        st_spec = pl.BlockSpec((n // steps,) + state.shape[1:], lambda i: (i, 0, 0, 0))
        in_specs += [next_tile] + [_const_spec(a.shape) for a in (norm_g, qgt, kgt, lat, vg, rg)] + [st_spec]
        operands += [x, norm_g, qgt, kgt, lat, vg, rg, state]
        outs = [(jax.ShapeDtypeStruct((rows, A_WIDTH), F32), row(A_WIDTH)),
                (jax.ShapeDtypeStruct((rows, A_WIDTH), F32), row(A_WIDTH)),
                (jax.ShapeDtypeStruct((rows, A_WIDTH), F32), row(A_WIDTH)),
                (jax.ShapeDtypeStruct((rows, 2 * D_MODEL), BF16), row(2 * D_MODEL)),
                (jax.ShapeDtypeStruct((rows, GLA_V), BF16), row(GLA_V)),
                (jax.ShapeDtypeStruct((seqs, GLA_HEADS, GLA_DK, GLA_DV), F32),
                 pl.BlockSpec((None, GLA_HEADS, GLA_DK, GLA_DV), lambda i: (i // rope_period_tiles, 0, 0, 0))),
                (jax.ShapeDtypeStruct((n, GLA_V), BF16), whole((n, GLA_V))),
                (jax.ShapeDtypeStruct(state.shape, F32), st_spec)]
        seq_len = rope_period_tiles * tm
        for window, _ in DILATED_GROUPS:
            keep = min(window, seq_len)
            width = min(keep, tm)
            first_kept = rope_period_tiles - keep // width
            outs.append((
                jax.ShapeDtypeStruct((seqs, 2, GROUP_COLS, keep), F32),
                pl.BlockSpec((None, 2, GROUP_COLS, width),
                             lambda i, first_kept=first_kept: (
                                 i // rope_period_tiles, 0, 0,
                                 jnp.maximum(i % rope_period_tiles - first_kept, 0)))))
        scratch = [pltpu.VMEM((GLA_HEADS, GLA_DV, GLA_DK), F32), pltpu.VMEM((n, GLA_V), F32),
                   pltpu.VMEM((tm, GLA_K), F32), pltpu.SMEM((1,), F32)]
    return pl.pallas_call(
        functools.partial(_in_proj_kernel, sample=sample, tiles_per_seq=rope_period_tiles),
        grid=(steps,),
        in_specs=in_specs,
        out_specs=tuple(o[1] for o in outs),
        out_shape=tuple(o[0] for o in outs),
        scratch_shapes=scratch,
        compiler_params=pltpu.CompilerParams(
            dimension_semantics=("parallel" if sample else "arbitrary",), vmem_limit_bytes=VMEM_LIMIT),
        name="in_proj_sample" if sample else "in_proj_prompt",
    )(*operands)


def _rope_table(pos):
    half = ROT_DIM // 2
    inv_freq = ROPE_THETA ** (-jnp.arange(0, ROT_DIM, 2, dtype=F32) / ROT_DIM)
    ang = pos.astype(F32)[:, None] * inv_freq[None, :]
    cos, sin = jnp.cos(ang), jnp.sin(ang)
    lane = np.arange(LANES) % HEAD_DIM
    fidx = lane % half
    first = lane < half
    second = (lane >= half) & (lane < ROT_DIM)
    cos_t = jnp.where((first | second)[None, :], cos[:, fidx], 1.0)
    sin_lo = jnp.where(first[None, :], -sin[:, fidx], 0.0)
    sin_hi = jnp.where(second[None, :], sin[:, fidx], 0.0)
    return jnp.concatenate([cos_t, sin_lo, sin_hi], axis=1)


ATT_TILE = 2048
Q_BLOCK = 128
HEADS_PER_TILE = LANES // HEAD_DIM


def _attn_prompt_kernel(q_ref, kc_ref, kp_ref, vc_ref, vp_ref, o_ref, og, lg, bias):
    t = pl.program_id(1)
    g = pl.program_id(3)
    band_keys = 2 * Q_BLOCK
    qi = lax.broadcasted_iota(jnp.int32, (Q_BLOCK, HEADS_PER_TILE * band_keys), 0)
    kc = lax.broadcasted_iota(jnp.int32, (Q_BLOCK, HEADS_PER_TILE * band_keys), 1) % band_keys
    back = qi + Q_BLOCK - kc
    band = (back >= 0) & (back <= KEYS_PER_WINDOW)
    bias[0] = jnp.where(band, 0.0, -jnp.inf)
    bias[1] = jnp.where(band & (kc >= Q_BLOCK), 0.0, -jnp.inf)
    head0 = lax.broadcasted_iota(jnp.int32, (band_keys, LANES), 1) < HEAD_DIM
    head0_q = lax.broadcasted_iota(jnp.int32, (Q_BLOCK, LANES), 1) < HEAD_DIM

    for gi, (window, dil) in enumerate(DILATED_GROUPS):

        @pl.when(g == gi)
        def _(gi=gi, window=window, dil=dil):
            nblk = ATT_TILE // dil // Q_BLOCK

            def rows(start):
                return pl.ds(start, Q_BLOCK) if dil == 1 else pl.ds(start, Q_BLOCK, stride=dil)

            def block(r, n):
                base = r + dil * Q_BLOCK * n
                q2 = q_ref[rows(base), :].astype(BF16)
                if n == 0:
                    prev = rows(ATT_TILE - window + r)
                    k_prev, v_prev = kp_ref[prev, :], vp_ref[prev, :]
                    mask = bias[(t == 0).astype(jnp.int32)]
                else:
                    prev = rows(base - dil * Q_BLOCK)
                    k_prev, v_prev = kc_ref[prev, :], vc_ref[prev, :]
                    mask = bias[0]
                k2 = jnp.concatenate([k_prev, kc_ref[rows(base), :]], axis=0)
                v2 = jnp.concatenate([v_prev, vc_ref[rows(base), :]], axis=0)
                kz = jnp.concatenate([jnp.where(head0, k2, 0.0), jnp.where(head0, 0.0, k2)], axis=0)
                s = _dot_nt(q2, kz.astype(BF16)) + mask
                s0, s1 = s[:, :band_keys], s[:, band_keys:]
                m0 = jnp.max(s0, axis=-1, keepdims=True)
                m1 = jnp.max(s1, axis=-1, keepdims=True)
                p = jnp.concatenate([jnp.exp(s0 - m0), jnp.exp(s1 - m1)], axis=1).astype(BF16)
                ones0 = head0.astype(F32)
                vz = jnp.concatenate([
                    jnp.concatenate([jnp.where(head0, v2, 0.0), ones0], axis=1),
                    jnp.concatenate([jnp.where(head0, 0.0, v2), 1.0 - ones0], axis=1)], axis=0)
                res = _dot(p, vz.astype(BF16))
                den = res[:, LANES:]
                og[gi, rows(base), :] = res[:, :LANES] / den
                m = jnp.where(head0_q, jnp.broadcast_to(m0, den.shape), jnp.broadcast_to(m1, den.shape))
                lg[gi, rows(base), :] = m + jnp.log(den)

            for r in range(dil):
                for n in range(nblk):
                    block(r, n)

    @pl.when(g == len(DILATED_GROUPS) - 1)
    def _():
        rows = 256

        def combine(c, carry):
            r0 = pl.multiple_of(c * rows, rows)
            l0 = lg[0, pl.ds(r0, rows), :]
            l1 = lg[1, pl.ds(r0, rows), :]
            l2 = lg[2, pl.ds(r0, rows), :]
            mx = jnp.maximum(jnp.maximum(l0, l1), l2)
            e0, e1, e2 = jnp.exp(l0 - mx), jnp.exp(l1 - mx), jnp.exp(l2 - mx)
            acc = (e0 * og[0, pl.ds(r0, rows), :] + e1 * og[1, pl.ds(r0, rows), :]
                   + e2 * og[2, pl.ds(r0, rows), :])
            o_ref[pl.ds(r0, rows), :] = (acc / (e0 + e1 + e2)).astype(o_ref.dtype)
            return carry

        lax.fori_loop(0, ATT_TILE // rows, combine, 0)


def _attn_prompt(q, k, v):
    b, s, _ = q.shape
    tiles_per_group = GROUP_COLS // LANES
    cur = pl.BlockSpec((None, ATT_TILE, LANES), lambda bi, t, c, g: (bi, t, g * tiles_per_group + c))
    prev = pl.BlockSpec((None, ATT_TILE, LANES),
                        lambda bi, t, c, g: (bi, jnp.maximum(t - 1, 0), g * tiles_per_group + c))
    assert all(w <= ATT_TILE for w, _ in DILATED_GROUPS), "the key halo must fit in the previous tile"
    ngroups = len(DILATED_GROUPS)
    return pl.pallas_call(
        _attn_prompt_kernel,
        grid=(b, s // ATT_TILE, tiles_per_group, ngroups),
        in_specs=[cur, cur, prev, cur, prev],
        out_specs=pl.BlockSpec((None, ATT_TILE, LANES), lambda bi, t, c, g: (bi, t, c)),
        out_shape=jax.ShapeDtypeStruct((b, s, GROUP_COLS), BF16),
        scratch_shapes=[
            pltpu.VMEM((ngroups, ATT_TILE, LANES), F32),
            pltpu.VMEM((ngroups, ATT_TILE, LANES), F32),
            pltpu.VMEM((2, Q_BLOCK, HEADS_PER_TILE * 2 * Q_BLOCK), F32),
        ],
        compiler_params=pltpu.CompilerParams(
            dimension_semantics=("parallel", "parallel", "parallel", "arbitrary"),
            vmem_limit_bytes=VMEM_LIMIT),
        name="attn_prompt",
    )(q, k, k, v, v)


def _gla_epilogue(o, r, g):
    o = o * lax.rsqrt(jnp.mean(o * o, -1, keepdims=True) + RMS_EPS)
    return o * g * _silu(r.astype(F32))


def _pick_column(ref, n):
    lane = lax.broadcasted_iota(jnp.int32, ref.shape, 1)
    return jnp.sum(jnp.where(lane == n, ref[...], 0.0), axis=1, keepdims=True)


def _attn_sample_step(n, qt_ref, kt_ref, vt_ref, caches, news, o_cols):
    ngroups = len(DILATED_GROUPS)
    qcol = _pick_column(qt_ref, n)
    kcol = _pick_column(kt_ref, n)
    vcol = _pick_column(vt_ref, n)

    def shifted(x, new):
        lb = x.shape[1]
        lane = lax.broadcasted_iota(jnp.int32, x.shape, 1)
        return jnp.where(lane == lb - 1, new, pltpu.roll(x, lb - 1, 1))

    heads = []
    for h in range(HEADS_PER_GROUP):
        s_old, s_new = [], []
        for gi, (window, dil) in enumerate(DILATED_GROUPS):
            lb = caches[gi].shape[1]
            r0 = gi * GROUP_COLS + h * HEAD_DIM
            q1 = qcol[r0:r0 + HEAD_DIM]
            k1 = kcol[r0:r0 + HEAD_DIM]
            keys = caches[gi][h * HEAD_DIM:(h + 1) * HEAD_DIM, :]
            news[gi][h * HEAD_DIM:(h + 1) * HEAD_DIM, :] = shifted(keys, k1)
            s = jnp.sum(keys * q1, axis=0, keepdims=True)
            pos = lax.broadcasted_iota(jnp.int32, (1, lb), 1)
            s_old.append(jnp.where(pos % dil == 0, s, -jnp.inf))
            s_new.append(jnp.sum(k1 * q1, axis=0, keepdims=True))
        m = s_new[0]
        for gi in range(ngroups):
            m = jnp.maximum(m, jnp.maximum(s_new[gi], jnp.max(s_old[gi], axis=1, keepdims=True)))
        den = jnp.zeros((1, 1), F32)
        acc = jnp.zeros((HEAD_DIM, 1), F32)
        for gi in range(ngroups):
            r0 = gi * GROUP_COLS + h * HEAD_DIM
            v1 = vcol[r0:r0 + HEAD_DIM]
            p_old = jnp.exp(s_old[gi] - m)
            p_new = jnp.exp(s_new[gi] - m)
            den = den + jnp.sum(p_old, axis=1, keepdims=True) + p_new
            rows = slice(GROUP_COLS + h * HEAD_DIM, GROUP_COLS + (h + 1) * HEAD_DIM)
            vals = caches[gi][rows, :]
            news[gi][rows, :] = shifted(vals, v1)
            acc = acc + jnp.sum(vals * p_old, axis=1, keepdims=True) + v1 * p_new
        heads.append(acc / den)
    o_col = jnp.concatenate(heads, axis=0)
    lane = lax.broadcasted_iota(jnp.int32, o_cols.shape, 1)
    o_cols[...] = jnp.where(lane == n, o_col, o_cols[...])


def _post_kernel(x_ref, oa_ref, ob_ref, gates_ref, pe_ref,
                 wa_ref, wb_ref, wo_ref, wfg_ref, wfu_ref, wfd_ref, wpg_ref, wpp_ref,
                 ln1g_ref, ln1b_ref, ln2g_ref, ln2b_ref, *rest, sample_attn, skewed):
    rest = list(rest)
    if sample_attn:
        qt_ref, kt_ref, vt_ref, c1_ref, c2_ref, c3_ref = rest[:6]
        y_ref, n1_ref, n2_ref, n3_ref, oas_ref = rest[6:11]
        o_cols = rest[11]
        rest = rest[12:]
    else:
        y_ref = rest.pop(0)
    h_carry = rest.pop(0) if skewed else None
    i = pl.program_id(0)

    if sample_attn or skewed:
        @pl.when(i == 0)
        def _():
            if sample_attn:
                o_cols[...] = jnp.zeros_like(o_cols)
            if skewed:
                h_carry[...] = jnp.zeros_like(h_carry)

    if sample_attn:
        request = jnp.minimum(i, qt_ref.shape[1] - 1) if skewed else i
        _attn_sample_step(request, qt_ref, kt_ref, vt_ref, (c1_ref, c2_ref, c3_ref),
                          (n1_ref, n2_ref, n3_ref), o_cols)

    ga = gates_ref[:, 0:D_MODEL].astype(F32)
    gb = gates_ref[:, D_MODEL:2 * D_MODEL].astype(F32)
    from_a = _dot(oa_ref[...].astype(BF16), wa_ref[...])
    from_b = _dot(ob_ref[...].astype(BF16), wb_ref[...])
    if skewed:
        x1 = _layernorm(h_carry[...], ln1g_ref[...], ln1b_ref[...])
        merged = (ga * from_a + gb * from_b).astype(BF16)
    else:
        merged = (ga * from_a + gb * from_b).astype(BF16)
        x1 = _layernorm(DN_ALPHA * x_ref[...] + _dot(merged, wo_ref[...]), ln1g_ref[...], ln1b_ref[...])
    x1b = x1.astype(BF16)
    act = _silu(_dot(x1b, wfg_ref[...])) * _dot(x1b, wfu_ref[...])
    if skewed:
        h_new = DN_ALPHA * x_ref[...] + _dot(merged, wo_ref[...])
    ff = _dot(act.astype(BF16), wfd_ref[...])
    ple = _dot(pe_ref[...].astype(BF16), wpp_ref[...])
    x2 = _layernorm(DN_ALPHA * x1 + ff, ln2g_ref[...], ln2b_ref[...])
    gate = _sigmoid(_dot(x2.astype(BF16), wpg_ref[...]))
    y_ref[...] = x2 + gate * ple
    if skewed:
        h_carry[...] = h_new

    if sample_attn:
        @pl.when(i == pl.num_programs(0) - 1)
        def _():
            for c in range(GROUP_COLS // LANES):
                oas_ref[:, c * LANES:(c + 1) * LANES] = o_cols[c * LANES:(c + 1) * LANES, :].T


def _post(x, oa, ob, gates, pe, weights, *, tm, sample_attn=None):
    rows = x.shape[0]
    tiles = rows // tm
    skewed = tiles > 1
    steps = tiles + 1 if skewed else tiles
    stage1 = lambda w: pl.BlockSpec((tm, w), lambda i: (jnp.minimum(i, tiles - 1), 0))
    stage2 = lambda w: pl.BlockSpec((tm, w), lambda i: (jnp.maximum(i - 1, 0) if skewed else i, 0))
    in_specs = [stage1(D_MODEL), stage1(GROUP_COLS), stage1(GLA_V), stage1(2 * D_MODEL),
                stage2(PLE_DIM)] + [_const_spec(w.shape) for w in weights]
    out_specs = [stage2(D_MODEL)]
    out_shape = [jax.ShapeDtypeStruct((rows, D_MODEL), F32)]
    operands = [x, oa, ob, gates, pe, *weights]
    scratch = []
    if sample_attn is not None:
        qt, kt, vt, caches = sample_attn
        n = qt.shape[1]
        assert n == LANES == tiles, "one request per row tile"
        for (window, _), c in zip(DILATED_GROUPS, caches):
            assert c.shape[2] == window, "every key distance must fall inside the cache"
        cache_specs = [pl.BlockSpec((None,) + c.shape[1:], lambda i: (jnp.minimum(i, n - 1), 0, 0))
                       for c in caches]
        in_specs += [_const_spec(qt.shape)] * 3 + cache_specs
        out_specs += cache_specs + [pl.BlockSpec((n, GROUP_COLS), lambda i: (0, 0))]
        out_shape += [jax.ShapeDtypeStruct(c.shape, c.dtype) for c in caches]
        out_shape += [jax.ShapeDtypeStruct((n, GROUP_COLS), F32)]
        operands += [qt, kt, vt, *caches]
        scratch.append(pltpu.VMEM((GROUP_COLS, n), F32))
    if skewed:
        scratch.append(pltpu.VMEM((tm, D_MODEL), F32))
    carried = sample_attn is not None or skewed
    out = pl.pallas_call(
        functools.partial(_post_kernel, sample_attn=sample_attn is not None, skewed=skewed),
        grid=(steps,),
        in_specs=in_specs,
        out_specs=tuple(out_specs),
        out_shape=tuple(out_shape),
        scratch_shapes=scratch,
        compiler_params=pltpu.CompilerParams(
            dimension_semantics=("arbitrary" if carried else "parallel",),
            vmem_limit_bytes=VMEM_LIMIT),
        name="post_with_sample_attn" if sample_attn is not None else "post",
    )(*operands)
    return out if sample_attn is not None else out[0]


PROMPT_TM = 256


def _layer_weights(w_in, w_gate_up, b_gate, w_a_out, w_b_out, w_o, ln1_g, ln1_b,
                   w_ff_gate, w_ff_up, w_ff_down, ln2_g, ln2_b, w_ple_gate, w_ple_proj):
    proj = (w_in.T.astype(BF16), w_gate_up.astype(BF16), b_gate.reshape(1, GLA_K))
    row = lambda p: p.reshape(1, -1)
    post = (w_a_out.astype(BF16), w_b_out.astype(BF16), w_o.astype(BF16), w_ff_gate.astype(BF16),
            w_ff_up.astype(BF16), w_ff_down.astype(BF16), w_ple_gate.astype(BF16),
            w_ple_proj.astype(BF16), row(ln1_g), row(ln1_b), row(ln2_g), row(ln2_b))
    return proj, post


def _positions_minor(cache):
    n, lb = cache.shape[:2]
    return jnp.transpose(cache, (0, 2, 3, 4, 1)).reshape(n, 2 * GROUP_COLS, lb)


def _positions_major(cache):
    n, lb = cache.shape[0], cache.shape[-1]
    return jnp.transpose(cache.reshape(n, 2, HEADS_PER_GROUP, HEAD_DIM, lb), (0, 4, 1, 2, 3))


def kernel(x_prompt, x_sample, cache_a1_kv, cache_a2_kv, cache_a3_kv, state_gla, p_prompt, p_sample, w_in, w_gate_up, b_gate, gla_norm_g, w_a_out, w_b_out, w_o, ln1_g, ln1_b, w_ff_gate, w_ff_up, w_ff_down, ln2_g, ln2_b, w_ple_gate, w_ple_proj):
    depth = w_in.shape[0]
    assert depth == 1 and x_sample.shape[1] == 1
    b, s, d = x_prompt.shape
    n = x_sample.shape[0]
    proj_w, post_w = _layer_weights(
        w_in[0], w_gate_up[0], b_gate[0], w_a_out[0], w_b_out[0], w_o[0], ln1_g[0], ln1_b[0],
        w_ff_gate[0], w_ff_up[0], w_ff_down[0], ln2_g[0], ln2_b[0], w_ple_gate[0], w_ple_proj[0])
    norm_g = gla_norm_g[0].reshape(1, GLA_V)

    xs = x_sample.reshape(n, d)
    rope_s = _rope_table(jnp.full((n,), PAST_LEN, jnp.int32))
    qt, kt, vt, qgt, kgt, vg_s, rg_s, lat, gates_s = _in_proj(
        xs, rope_s, proj_w, tm=n, rope_period_tiles=1, sample=True)

    xp = x_prompt.reshape(b * s, d)
    rope_p = _rope_table(jnp.arange(s))
    q, k, v, gates, ob, st_p, ob_s, st_s, *kv_tails = _in_proj(
        xp, rope_p, proj_w, tm=PROMPT_TM, rope_period_tiles=s // PROMPT_TM, sample=False,
        norm_g=norm_g, gla_sample=(qgt, kgt, lat, vg_s, rg_s, state_gla[0]))
    three = lambda t: t.reshape(b, s, t.shape[-1])
    oa = _attn_prompt(three(q), three(k), three(v))
    kv_p = [_positions_major(t)[None] for t in kv_tails]

    caches = [_positions_minor(c[0]) for c in (cache_a1_kv, cache_a2_kv, cache_a3_kv)]
    y_p, *new_caches, oa_s = _post(
        xp, oa.reshape(b * s, GROUP_COLS), ob, gates,
        p_prompt[0].reshape(b * s, PLE_DIM), post_w, tm=b * s // n, sample_attn=(qt, kt, vt, caches))
    y_s = _post(xs, oa_s, ob_s, gates_s, p_sample[0].reshape(n, PLE_DIM), post_w, tm=n)
    kv_s = [_positions_major(c)[None] for c in new_caches]

    return (y_p.reshape(b, s, d), y_s.reshape(n, 1, d), kv_p[0], kv_p[1], kv_p[2], st_p[None],
            kv_s[0], kv_s[1], kv_s[2], st_s[None])
```

```python
import functools

import numpy as np
import jax
import jax.numpy as jnp
from jax import lax
from jax.experimental import pallas as pl
from jax.experimental.pallas import tpu as pltpu

F32 = jnp.float32
BF16 = jnp.bfloat16

D_MODEL = 1024
HEAD_DIM = 64
HEADS_PER_GROUP = 4
DILATED_GROUPS = ((128, 1), (512, 4), (2048, 16))
GROUP_COLS = HEADS_PER_GROUP * HEAD_DIM
A_WIDTH = len(DILATED_GROUPS) * GROUP_COLS
ROT_DIM = HEAD_DIM // 4
ROPE_THETA = 500000.0
KEYS_PER_WINDOW = 128
GLA_HEADS = 4
GLA_DK = 128
GLA_DV = 256
GLA_K = GLA_HEADS * GLA_DK
GLA_V = GLA_HEADS * GLA_DV
GATE_RANK = 16
GATE_NORMALIZER = 16.0
GLA_CHUNK = 128
D_FF = 2816
PLE_DIM = 256
DN_ALPHA = 2.0 ** 0.25
LN_EPS = 1e-5
RMS_EPS = 1e-6
PAST_LEN = 8192
COL_GLA = 3 * A_WIDTH
COL_LOWRANK = COL_GLA + 2 * GLA_K + 2 * GLA_V
COL_GATES = COL_LOWRANK + GATE_RANK
IN_COLS = COL_GATES + 2 * D_MODEL

LANES = 128
SUBLANES = 8
VMEM_LIMIT = 60 * 1024 * 1024

GLA_FACTORED_RANGE = 40.0


def _dot(a, b):
    return jnp.dot(a, b, preferred_element_type=F32)


def _dot_nt(a, b):
    return lax.dot_general(a, b, (((1,), (1,)), ((), ())), preferred_element_type=F32)


def _dot_tn(a, b):
    return lax.dot_general(a, b, (((0,), (0,)), ((), ())), preferred_element_type=F32)


def _const_spec(shape):
    return pl.BlockSpec(shape, lambda *_: (0,) * len(shape), pipeline_mode=pl.Buffered(1))


def _layernorm(x, g, b):
    xc = x - jnp.mean(x, -1, keepdims=True)
    var = jnp.mean(xc * xc, -1, keepdims=True)
    return xc * lax.rsqrt(var + LN_EPS) * g + b


def _sigmoid(x):
    return 1.0 / (1.0 + jnp.exp(-x))


def _silu(x):
    return x * _sigmoid(x)


def _split3_bf16(x):
    hi = x.astype(BF16)
    r1 = x - hi.astype(F32)
    mid = r1.astype(BF16)
    lo = (r1 - mid.astype(F32)).astype(BF16)
    return hi, mid, lo


def _gla_sample_step(n, j, qt_ref, kt_ref, lat_ref, v_ref, st_ref, st_out_ref, o_scr):
    a = jnp.exp(_pick_column(lat_ref, n))
    k = _pick_column(kt_ref, n)
    q = _pick_column(qt_ref, n)
    for h in range(GLA_HEADS):
        ks = slice(h * GLA_DK, (h + 1) * GLA_DK)
        vs = slice(h * GLA_DV, (h + 1) * GLA_DV)
        st = a[ks] * st_ref[j, h] + k[ks] * v_ref[pl.ds(n, 1), vs]
        st_out_ref[j, h] = st
        o_scr[pl.ds(n, 1), vs] = jnp.sum(q[ks] * st, axis=0, keepdims=True)


def _gla_chunk(q, k, v, cum, s_t, factored):
    c = q.shape[0]
    ri = lax.broadcasted_iota(jnp.int32, (c, c), 0)
    ci = lax.broadcasted_iota(jnp.int32, (c, c), 1)
    last = cum[c - 1:c]
    decay = jnp.exp(last)
    qe = (q * jnp.exp(cum)).astype(BF16)
    vb = v.astype(BF16)
    if factored:
        ke = k * jnp.exp(-cum)
        a = jnp.where(ci <= ri, _dot_nt(qe, ke.astype(BF16)), 0.0)
        kd = ke * decay
    else:
        row_id = lax.broadcasted_iota(jnp.int32, cum.shape, 0)

        def col(j, a):
            cj = jnp.sum(jnp.where(row_id == j, cum, 0.0), axis=0, keepdims=True)
            kj = jnp.sum(jnp.where(row_id == j, k, 0.0), axis=0, keepdims=True)
            e = jnp.exp(jnp.where(row_id >= j, cum - cj, -jnp.inf))
            sj = jnp.sum(q * kj * e, axis=-1, keepdims=True)
            return jnp.where(ci == j, sj, a)

        a = lax.fori_loop(0, c, col, jnp.zeros((c, c), F32))
        kd = k * jnp.exp(last - cum)
    o = _dot_nt(qe, s_t.astype(BF16)) + _dot(a.astype(BF16), vb)
    return o, s_t * decay + _dot_tn(vb, kd.astype(BF16))


def _in_proj_kernel(x_ref, rope_ref, w_ref, wup_ref, bg_ref, *rest, sample, tiles_per_seq):
    tm = x_ref.shape[0]
    wa_ref = w_ref.at[0:COL_GLA]
    wg_ref = w_ref.at[COL_GLA:COL_LOWRANK]
    wlr_ref = w_ref.at[COL_LOWRANK:COL_GATES]
    wgates_ref = w_ref.at[COL_GATES:IN_COLS]
    xb = x_ref[...].astype(BF16)
    cos = rope_ref[:, 0:LANES]
    sin_lo = rope_ref[:, LANES:2 * LANES]
    sin_hi = rope_ref[:, 2 * LANES:3 * LANES]

    def rope(t):
        return (t * cos + pltpu.roll(t, LANES - ROT_DIM // 2, 1) * sin_lo
                + pltpu.roll(t, ROT_DIM // 2, 1) * sin_hi)

    def put(ref, col, val):
        if sample:
            ref[col * LANES:(col + 1) * LANES, :] = val.T
        else:
            ref[:, col * LANES:(col + 1) * LANES] = val

    def branch_a_and_gates(q_ref, k_ref, v_ref, gates_ref, tails=None):
        tiles = GROUP_COLS // LANES
        for c in range(A_WIDTH // GROUP_COLS):
            lo, hi = c * GROUP_COLS, (c + 1) * GROUP_COLS
            qc = _dot_nt(xb, wa_ref[lo:hi, :])
            kc = _dot_nt(xb, wa_ref[A_WIDTH + lo:A_WIDTH + hi, :])
            vc = _dot_nt(xb, wa_ref[2 * A_WIDTH + lo:2 * A_WIDTH + hi, :])
            for s in range(tiles):
                sl = slice(s * LANES, (s + 1) * LANES)
                k_rot = rope(kc[:, sl])
                put(q_ref, c * tiles + s, rope(qc[:, sl]) * (HEAD_DIM ** -0.5))
                put(k_ref, c * tiles + s, k_rot)
                put(v_ref, c * tiles + s, vc[:, sl])
                if tails is not None:
                    width = tails[c].shape[-1]
                    tails[c][0, sl, :] = k_rot[tm - width:, :].T
                    tails[c][1, sl, :] = vc[tm - width:, sl].T
        gates_ref[...] = _sigmoid(_dot_nt(xb, wgates_ref[...])).astype(BF16)

    def gla_projections():
        qg = _dot_nt(xb, wg_ref[0:GLA_K, :]) * (GLA_DK ** -0.5)
        kg = _dot_nt(xb, wg_ref[GLA_K:2 * GLA_K, :])
        vg = _dot_nt(xb, wg_ref[2 * GLA_K:2 * GLA_K + GLA_V, :])
        rg = _dot_nt(xb, wg_ref[2 * GLA_K + GLA_V:2 * GLA_K + 2 * GLA_V, :])
        return qg, kg, vg, rg

    def log_decay(rows_bf16):
        glr = _dot_nt(rows_bf16, wlr_ref[...])
        z = _dot(glr.astype(BF16), wup_ref[...]) + bg_ref[...]
        return -(jnp.maximum(-z, 0.0) + jnp.log1p(jnp.exp(-jnp.abs(z)))) * (1.0 / GATE_NORMALIZER)

    if sample:
        loga = log_decay(xb)
        q_ref, k_ref, v_ref, qg_ref, kg_ref, vg_ref, rg_ref, la_ref, gates_ref = rest
        branch_a_and_gates(q_ref, k_ref, v_ref, gates_ref)
        qg, kg, vg, rg = gla_projections()
        for s in range(GLA_K // LANES):
            sl = slice(s * LANES, (s + 1) * LANES)
            put(qg_ref, s, qg[:, sl])
            put(kg_ref, s, kg[:, sl])
            put(la_ref, s, loga[:, sl])
        vg_ref[...] = vg
        rg_ref[...] = rg.astype(BF16)
        return

    (x_next_ref, gn_ref, qts_ref, kts_ref, lats_ref, vs_ref, rs_ref, st_ref,
     q_ref, k_ref, v_ref, gates_ref, ob_ref, stp_ref, obs_ref, st_out_ref, *tails,
     st_t, o_scr, cum_scr, range_scr) = rest
    step = pl.program_id(0)
    tile_in_seq = step % tiles_per_seq

    @pl.when(tile_in_seq == 0)
    def _():
        st_t[...] = jnp.zeros_like(st_t)

    def stage_decay(rows_bf16):
        row = lax.broadcasted_iota(jnp.int32, (tm, tm), 0)
        col = lax.broadcasted_iota(jnp.int32, (tm, tm), 1)
        tri = ((row // GLA_CHUNK == col // GLA_CHUNK) & (col <= row)).astype(BF16)
        hi, mid, lo = _split3_bf16(log_decay(rows_bf16))
        cum = _dot(tri, hi) + _dot(tri, mid) + _dot(tri, lo)
        cum_scr[...] = cum
        range_scr[0] = -jnp.min(cum)

    @pl.when(step == 0)
    def _():
        stage_decay(xb)

    def rest_of_tile(factored):
        cum = cum_scr[...]
        stage_decay(x_next_ref[...].astype(BF16))
        qg, kg, vg, rg = gla_projections()
        per_step = st_ref.shape[0]
        for j in range(per_step):
            _gla_sample_step(step * per_step + j, j, qts_ref, kts_ref, lats_ref, vs_ref,
                             st_ref, st_out_ref, o_scr)
        for ch in range(tm // GLA_CHUNK):
            rows = slice(ch * GLA_CHUNK, (ch + 1) * GLA_CHUNK)
            for h in range(GLA_HEADS):
                ks = slice(h * GLA_DK, (h + 1) * GLA_DK)
                vs = slice(h * GLA_DV, (h + 1) * GLA_DV)
                o, st_t[h] = _gla_chunk(qg[rows, ks], kg[rows, ks], vg[rows, vs], cum[rows, ks],
                                        st_t[h], factored)
                ob_ref[rows, vs] = _gla_epilogue(o, rg[rows, vs], gn_ref[:, vs]).astype(ob_ref.dtype)
        branch_a_and_gates(q_ref, k_ref, v_ref, gates_ref, tails)

    total = range_scr[0]
    pl.when(total < GLA_FACTORED_RANGE)(functools.partial(rest_of_tile, True))
    pl.when(total >= GLA_FACTORED_RANGE)(functools.partial(rest_of_tile, False))

    @pl.when(tile_in_seq == tiles_per_seq - 1)
    def _():
        for h in range(GLA_HEADS):
            stp_ref[h] = st_t[h].T

    @pl.when(step == pl.num_programs(0) - 1)
    def _():
        for h in range(GLA_HEADS):
            vs = slice(h * GLA_DV, (h + 1) * GLA_DV)
            obs_ref[:, vs] = _gla_epilogue(o_scr[:, vs], rs_ref[:, vs], gn_ref[:, vs]).astype(obs_ref.dtype)


def _in_proj(x, rope_tab, weights, *, tm, rope_period_tiles, sample, norm_g=None, gla_sample=None):
    rows = x.shape[0]
    steps = rows // tm
    w, wup, bg = weights
    row = lambda width: pl.BlockSpec((tm, width), lambda i: (i, 0))
    whole = lambda shape: pl.BlockSpec(shape, lambda i: (0,) * len(shape))
    in_specs = [row(D_MODEL), pl.BlockSpec((tm, 3 * LANES), lambda i: (i % rope_period_tiles, 0)),
                _const_spec(w.shape), _const_spec(wup.shape), _const_spec(bg.shape)]
    next_tile = pl.BlockSpec((tm, D_MODEL), lambda i: (jnp.minimum(i + 1, steps - 1), 0))
    operands = [x, rope_tab, w, wup, bg]
    scratch = []
    if sample:
        assert tm == rows == LANES
        t = lambda width: (jax.ShapeDtypeStruct((width, rows), F32), whole((width, rows)))
        outs = [t(A_WIDTH), t(A_WIDTH), t(A_WIDTH), t(GLA_K), t(GLA_K),
                (jax.ShapeDtypeStruct((rows, GLA_V), F32), row(GLA_V)),
                (jax.ShapeDtypeStruct((rows, GLA_V), BF16), row(GLA_V)),
                t(GLA_K),
                (jax.ShapeDtypeStruct((rows, 2 * D_MODEL), BF16), row(2 * D_MODEL))]
    else:
        qgt, kgt, lat, vg, rg, state = gla_sample
        n = state.shape[0]
        seqs = steps // rope_period_tiles
        assert n % steps == 0, "the same number of requests per grid step"
        assert tm % GLA_CHUNK == 0
        st_spec = pl.BlockSpec((n // steps,) + state.shape[1:], lambda i: (i, 0, 0, 0))
        in_specs += [next_tile] + [_const_spec(a.shape) for a in (norm_g, qgt, kgt, lat, vg, rg)] + [st_spec]
        operands += [x, norm_g, qgt, kgt, lat, vg, rg, state]
        outs = [(jax.ShapeDtypeStruct((rows, A_WIDTH), F32), row(A_WIDTH)),
                (jax.ShapeDtypeStruct((rows, A_WIDTH), F32), row(A_WIDTH)),
                (jax.ShapeDtypeStruct((rows, A_WIDTH), F32), row(A_WIDTH)),
                (jax.ShapeDtypeStruct((rows, 2 * D_MODEL), BF16), row(2 * D_MODEL)),
                (jax.ShapeDtypeStruct((rows, GLA_V), BF16), row(GLA_V)),
                (jax.ShapeDtypeStruct((seqs, GLA_HEADS, GLA_DK, GLA_DV), F32),
                 pl.BlockSpec((None, GLA_HEADS, GLA_DK, GLA_DV), lambda i: (i // rope_period_tiles, 0, 0, 0))),
                (jax.ShapeDtypeStruct((n, GLA_V), BF16), whole((n, GLA_V))),
                (jax.ShapeDtypeStruct(state.shape, F32), st_spec)]
        seq_len = rope_period_tiles * tm
        for window, _ in DILATED_GROUPS:
            keep = min(window, seq_len)
            width = min(keep, tm)
            first_kept = rope_period_tiles - keep // width
            outs.append((
                jax.ShapeDtypeStruct((seqs, 2, GROUP_COLS, keep), F32),
                pl.BlockSpec((None, 2, GROUP_COLS, width),
                             lambda i, first_kept=first_kept: (
                                 i // rope_period_tiles, 0, 0,
                                 jnp.maximum(i % rope_period_tiles - first_kept, 0)))))
        scratch = [pltpu.VMEM((GLA_HEADS, GLA_DV, GLA_DK), F32), pltpu.VMEM((n, GLA_V), F32),
                   pltpu.VMEM((tm, GLA_K), F32), pltpu.SMEM((1,), F32)]
    return pl.pallas_call(
        functools.partial(_in_proj_kernel, sample=sample, tiles_per_seq=rope_period_tiles),
        grid=(steps,),
        in_specs=in_specs,
        out_specs=tuple(o[1] for o in outs),
        out_shape=tuple(o[0] for o in outs),
        scratch_shapes=scratch,
        compiler_params=pltpu.CompilerParams(
            dimension_semantics=("parallel" if sample else "arbitrary",), vmem_limit_bytes=VMEM_LIMIT),
        name="in_proj_sample" if sample else "in_proj_prompt",
    )(*operands)


def _rope_table(pos):
    half = ROT_DIM // 2
    inv_freq = ROPE_THETA ** (-jnp.arange(0, ROT_DIM, 2, dtype=F32) / ROT_DIM)
    ang = pos.astype(F32)[:, None] * inv_freq[None, :]
    cos, sin = jnp.cos(ang), jnp.sin(ang)
    lane = np.arange(LANES) % HEAD_DIM
    fidx = lane % half
    first = lane < half
    second = (lane >= half) & (lane < ROT_DIM)
    cos_t = jnp.where((first | second)[None, :], cos[:, fidx], 1.0)
    sin_lo = jnp.where(first[None, :], -sin[:, fidx], 0.0)
    sin_hi = jnp.where(second[None, :], sin[:, fidx], 0.0)
    return jnp.concatenate([cos_t, sin_lo, sin_hi], axis=1)


ATT_TILE = 2048
Q_BLOCK = 128
HEADS_PER_TILE = LANES // HEAD_DIM


def _attn_prompt_kernel(q_ref, kc_ref, kp_ref, vc_ref, vp_ref, o_ref, og, lg, bias):
    t = pl.program_id(1)
    g = pl.program_id(3)
    band_keys = 2 * Q_BLOCK
    qi = lax.broadcasted_iota(jnp.int32, (Q_BLOCK, HEADS_PER_TILE * band_keys), 0)
    kc = lax.broadcasted_iota(jnp.int32, (Q_BLOCK, HEADS_PER_TILE * band_keys), 1) % band_keys
    back = qi + Q_BLOCK - kc
    band = (back >= 0) & (back <= KEYS_PER_WINDOW)
    bias[0] = jnp.where(band, 0.0, -jnp.inf)
    bias[1] = jnp.where(band & (kc >= Q_BLOCK), 0.0, -jnp.inf)
    head0 = lax.broadcasted_iota(jnp.int32, (band_keys, LANES), 1) < HEAD_DIM
    head0_q = lax.broadcasted_iota(jnp.int32, (Q_BLOCK, LANES), 1) < HEAD_DIM

    for gi, (window, dil) in enumerate(DILATED_GROUPS):

        @pl.when(g == gi)
        def _(gi=gi, window=window, dil=dil):
            nblk = ATT_TILE // dil // Q_BLOCK

            def rows(start):
                return pl.ds(start, Q_BLOCK) if dil == 1 else pl.ds(start, Q_BLOCK, stride=dil)

            def block(r, n):
                base = r + dil * Q_BLOCK * n
                q2 = q_ref[rows(base), :].astype(BF16)
                if n == 0:
                    prev = rows(ATT_TILE - window + r)
                    k_prev, v_prev = kp_ref[prev, :], vp_ref[prev, :]
                    mask = bias[(t == 0).astype(jnp.int32)]
                else:
                    prev = rows(base - dil * Q_BLOCK)
                    k_prev, v_prev = kc_ref[prev, :], vc_ref[prev, :]
                    mask = bias[0]
                k2 = jnp.concatenate([k_prev, kc_ref[rows(base), :]], axis=0)
                v2 = jnp.concatenate([v_prev, vc_ref[rows(base), :]], axis=0)
                kz = jnp.concatenate([jnp.where(head0, k2, 0.0), jnp.where(head0, 0.0, k2)], axis=0)
                s = _dot_nt(q2, kz.astype(BF16)) + mask
                s0, s1 = s[:, :band_keys], s[:, band_keys:]
                m0 = jnp.max(s0, axis=-1, keepdims=True)
                m1 = jnp.max(s1, axis=-1, keepdims=True)
                p = jnp.concatenate([jnp.exp(s0 - m0), jnp.exp(s1 - m1)], axis=1).astype(BF16)
                ones0 = head0.astype(F32)
                vz = jnp.concatenate([
                    jnp.concatenate([jnp.where(head0, v2, 0.0), ones0], axis=1),
                    jnp.concatenate([jnp.where(head0, 0.0, v2), 1.0 - ones0], axis=1)], axis=0)
                res = _dot(p, vz.astype(BF16))
                den = res[:, LANES:]
                og[gi, rows(base), :] = res[:, :LANES] / den
                m = jnp.where(head0_q, jnp.broadcast_to(m0, den.shape), jnp.broadcast_to(m1, den.shape))
                lg[gi, rows(base), :] = m + jnp.log(den)

            for r in range(dil):
                for n in range(nblk):
                    block(r, n)

    @pl.when(g == len(DILATED_GROUPS) - 1)
    def _():
        rows = 256

        def combine(c, carry):
            r0 = pl.multiple_of(c * rows, rows)
            l0 = lg[0, pl.ds(r0, rows), :]
            l1 = lg[1, pl.ds(r0, rows), :]
            l2 = lg[2, pl.ds(r0, rows), :]
            mx = jnp.maximum(jnp.maximum(l0, l1), l2)
            e0, e1, e2 = jnp.exp(l0 - mx), jnp.exp(l1 - mx), jnp.exp(l2 - mx)
            acc = (e0 * og[0, pl.ds(r0, rows), :] + e1 * og[1, pl.ds(r0, rows), :]
                   + e2 * og[2, pl.ds(r0, rows), :])
            o_ref[pl.ds(r0, rows), :] = (acc / (e0 + e1 + e2)).astype(o_ref.dtype)
            return carry

        lax.fori_loop(0, ATT_TILE // rows, combine, 0)


def _attn_prompt(q, k, v):
    b, s, _ = q.shape
    tiles_per_group = GROUP_COLS // LANES
    cur = pl.BlockSpec((None, ATT_TILE, LANES), lambda bi, t, c, g: (bi, t, g * tiles_per_group + c))
    prev = pl.BlockSpec((None, ATT_TILE, LANES),
                        lambda bi, t, c, g: (bi, jnp.maximum(t - 1, 0), g * tiles_per_group + c))
    assert all(w <= ATT_TILE for w, _ in DILATED_GROUPS), "the key halo must fit in the previous tile"
    ngroups = len(DILATED_GROUPS)
    return pl.pallas_call(
        _attn_prompt_kernel,
        grid=(b, s // ATT_TILE, tiles_per_group, ngroups),
        in_specs=[cur, cur, prev, cur, prev],
        out_specs=pl.BlockSpec((None, ATT_TILE, LANES), lambda bi, t, c, g: (bi, t, c)),
        out_shape=jax.ShapeDtypeStruct((b, s, GROUP_COLS), BF16),
        scratch_shapes=[
            pltpu.VMEM((ngroups, ATT_TILE, LANES), F32),
            pltpu.VMEM((ngroups, ATT_TILE, LANES), F32),
            pltpu.VMEM((2, Q_BLOCK, HEADS_PER_TILE * 2 * Q_BLOCK), F32),
        ],
        compiler_params=pltpu.CompilerParams(
            dimension_semantics=("parallel", "parallel", "parallel", "arbitrary"),
            vmem_limit_bytes=VMEM_LIMIT),
        name="attn_prompt",
    )(q, k, k, v, v)


def _gla_epilogue(o, r, g):
    o = o * lax.rsqrt(jnp.mean(o * o, -1, keepdims=True) + RMS_EPS)
    return o * g * _silu(r.astype(F32))


def _pick_column(ref, n):
    lane = lax.broadcasted_iota(jnp.int32, ref.shape, 1)
    return jnp.sum(jnp.where(lane == n, ref[...], 0.0), axis=1, keepdims=True)


def _attn_sample_step(n, qt_ref, kt_ref, vt_ref, caches, news, o_cols):
    ngroups = len(DILATED_GROUPS)
    qcol = _pick_column(qt_ref, n)
    kcol = _pick_column(kt_ref, n)
    vcol = _pick_column(vt_ref, n)

    def shifted(x, new):
        lb = x.shape[1]
        lane = lax.broadcasted_iota(jnp.int32, x.shape, 1)
        return jnp.where(lane == lb - 1, new, pltpu.roll(x, lb - 1, 1))

    heads = []
    for h in range(HEADS_PER_GROUP):
        s_old, s_new = [], []
        for gi, (window, dil) in enumerate(DILATED_GROUPS):
            lb = caches[gi].shape[1]
            r0 = gi * GROUP_COLS + h * HEAD_DIM
            q1 = qcol[r0:r0 + HEAD_DIM]
            k1 = kcol[r0:r0 + HEAD_DIM]
            keys = caches[gi][h * HEAD_DIM:(h + 1) * HEAD_DIM, :]
            news[gi][h * HEAD_DIM:(h + 1) * HEAD_DIM, :] = shifted(keys, k1)
            s = jnp.sum(keys * q1, axis=0, keepdims=True)
            pos = lax.broadcasted_iota(jnp.int32, (1, lb), 1)
            s_old.append(jnp.where(pos % dil == 0, s, -jnp.inf))
            s_new.append(jnp.sum(k1 * q1, axis=0, keepdims=True))
        m = s_new[0]
        for gi in range(ngroups):
            m = jnp.maximum(m, jnp.maximum(s_new[gi], jnp.max(s_old[gi], axis=1, keepdims=True)))
        den = jnp.zeros((1, 1), F32)
        acc = jnp.zeros((HEAD_DIM, 1), F32)
        for gi in range(ngroups):
            r0 = gi * GROUP_COLS + h * HEAD_DIM
            v1 = vcol[r0:r0 + HEAD_DIM]
            p_old = jnp.exp(s_old[gi] - m)
            p_new = jnp.exp(s_new[gi] - m)
            den = den + jnp.sum(p_old, axis=1, keepdims=True) + p_new
            rows = slice(GROUP_COLS + h * HEAD_DIM, GROUP_COLS + (h + 1) * HEAD_DIM)
            vals = caches[gi][rows, :]
            news[gi][rows, :] = shifted(vals, v1)
            acc = acc + jnp.sum(vals * p_old, axis=1, keepdims=True) + v1 * p_new
        heads.append(acc / den)
    o_col = jnp.concatenate(heads, axis=0)
    lane = lax.broadcasted_iota(jnp.int32, o_cols.shape, 1)
    o_cols[...] = jnp.where(lane == n, o_col, o_cols[...])


def _post_kernel(x_ref, oa_ref, ob_ref, gates_ref, pe_ref,
                 wa_ref, wb_ref, wo_ref, wfg_ref, wfu_ref, wfd_ref, wpg_ref, wpp_ref,
                 ln1g_ref, ln1b_ref, ln2g_ref, ln2b_ref,
                 qt_ref, kt_ref, vt_ref, c1_ref, c2_ref, c3_ref, xs_ref, obs_ref, gs_ref, pes_ref,
                 y_ref, n1_ref, n2_ref, n3_ref, ys_ref, o_cols, h_carry):
    i = pl.program_id(0)
    last = pl.num_programs(0) - 1

    @pl.when(i == 0)
    def _():
        o_cols[...] = jnp.zeros_like(o_cols)
        h_carry[...] = jnp.zeros_like(h_carry)

    request = jnp.minimum(i, qt_ref.shape[1] - 1)
    _attn_sample_step(request, qt_ref, kt_ref, vt_ref, (c1_ref, c2_ref, c3_ref),
                      (n1_ref, n2_ref, n3_ref), o_cols)

    def merge_and_project(x, oa, ob, gates):
        ga = gates[:, 0:D_MODEL].astype(F32)
        gb = gates[:, D_MODEL:2 * D_MODEL].astype(F32)
        merged = (ga * _dot(oa.astype(BF16), wa_ref[...]) + gb * _dot(ob.astype(BF16), wb_ref[...]))
        return DN_ALPHA * x + _dot(merged.astype(BF16), wo_ref[...])

    def norm_ffn_ple(h, pe):
        x1 = _layernorm(h, ln1g_ref[...], ln1b_ref[...])
        x1b = x1.astype(BF16)
        act = _silu(_dot(x1b, wfg_ref[...])) * _dot(x1b, wfu_ref[...])
        x2 = _layernorm(DN_ALPHA * x1 + _dot(act.astype(BF16), wfd_ref[...]), ln2g_ref[...], ln2b_ref[...])
        gate = _sigmoid(_dot(x2.astype(BF16), wpg_ref[...]))
        return x2 + gate * _dot(pe.astype(BF16), wpp_ref[...])

    h_new = merge_and_project(x_ref[...], oa_ref[...], ob_ref[...], gates_ref[...])
    y_ref[...] = norm_ffn_ple(h_carry[...], pe_ref[...])
    h_carry[...] = h_new

    @pl.when(i == last)
    def _():
        oa_s = jnp.concatenate([o_cols[c * LANES:(c + 1) * LANES, :].T for c in range(GROUP_COLS // LANES)],
                               axis=1)
        h_s = merge_and_project(xs_ref[...], oa_s, obs_ref[...], gs_ref[...])
        ys_ref[...] = norm_ffn_ple(h_s, pes_ref[...])


def _post(x, oa, ob, gates, pe, weights, *, sample_attn, sample_rows):
    rows = x.shape[0]
    qt, kt, vt, caches = sample_attn
    n = qt.shape[1]
    assert n == LANES and rows % n == 0, "one request per row tile"
    tm = rows // n
    tiles = n
    for (window, _), c in zip(DILATED_GROUPS, caches):
        assert c.shape[2] == window, "every key distance must fall inside the cache"
    stage1 = lambda w: pl.BlockSpec((tm, w), lambda i: (jnp.minimum(i, tiles - 1), 0))
    stage2 = lambda w: pl.BlockSpec((tm, w), lambda i: (jnp.maximum(i - 1, 0), 0))
    cache_specs = [pl.BlockSpec((None,) + c.shape[1:], lambda i: (jnp.minimum(i, n - 1), 0, 0)) for c in caches]
    in_specs = ([stage1(D_MODEL), stage1(GROUP_COLS), stage1(GLA_V), stage1(2 * D_MODEL), stage2(PLE_DIM)]
                + [_const_spec(w.shape) for w in weights]
                + [_const_spec(qt.shape)] * 3 + cache_specs
                + [_const_spec(a.shape) for a in sample_rows])
    out_specs = [stage2(D_MODEL)] + cache_specs + [pl.BlockSpec((n, D_MODEL), lambda i: (0, 0))]
    out_shape = ([jax.ShapeDtypeStruct((rows, D_MODEL), F32)]
                 + [jax.ShapeDtypeStruct(c.shape, c.dtype) for c in caches]
                 + [jax.ShapeDtypeStruct((n, D_MODEL), F32)])
    return pl.pallas_call(
        _post_kernel,
        grid=(tiles + 1,),
        in_specs=in_specs,
        out_specs=tuple(out_specs),
        out_shape=tuple(out_shape),
        scratch_shapes=[pltpu.VMEM((GROUP_COLS, n), F32), pltpu.VMEM((tm, D_MODEL), F32)],
        compiler_params=pltpu.CompilerParams(
            dimension_semantics=("arbitrary",), vmem_limit_bytes=VMEM_LIMIT),
        name="post",
    )(x, oa, ob, gates, pe, *weights, qt, kt, vt, *caches, *sample_rows)


PROMPT_TM = 256


def _layer_weights(w_in, w_gate_up, b_gate, w_a_out, w_b_out, w_o, ln1_g, ln1_b,
                   w_ff_gate, w_ff_up, w_ff_down, ln2_g, ln2_b, w_ple_gate, w_ple_proj):
    proj = (w_in.T.astype(BF16), w_gate_up.astype(BF16), b_gate.reshape(1, GLA_K))
    row = lambda p: p.reshape(1, -1)
    post = (w_a_out.astype(BF16), w_b_out.astype(BF16), w_o.astype(BF16), w_ff_gate.astype(BF16),
            w_ff_up.astype(BF16), w_ff_down.astype(BF16), w_ple_gate.astype(BF16),
            w_ple_proj.astype(BF16), row(ln1_g), row(ln1_b), row(ln2_g), row(ln2_b))
    return proj, post


def _positions_minor(cache):
    n, lb = cache.shape[:2]
    return jnp.transpose(cache, (0, 2, 3, 4, 1)).reshape(n, 2 * GROUP_COLS, lb)


def _positions_major(cache):
    n, lb = cache.shape[0], cache.shape[-1]
    return jnp.transpose(cache.reshape(n, 2, HEADS_PER_GROUP, HEAD_DIM, lb), (0, 4, 1, 2, 3))


def kernel(x_prompt, x_sample, cache_a1_kv, cache_a2_kv, cache_a3_kv, state_gla, p_prompt, p_sample, w_in, w_gate_up, b_gate, gla_norm_g, w_a_out, w_b_out, w_o, ln1_g, ln1_b, w_ff_gate, w_ff_up, w_ff_down, ln2_g, ln2_b, w_ple_gate, w_ple_proj):
    depth = w_in.shape[0]
    assert depth == 1 and x_sample.shape[1] == 1
    b, s, d = x_prompt.shape
    n = x_sample.shape[0]
    proj_w, post_w = _layer_weights(
        w_in[0], w_gate_up[0], b_gate[0], w_a_out[0], w_b_out[0], w_o[0], ln1_g[0], ln1_b[0],
        w_ff_gate[0], w_ff_up[0], w_ff_down[0], ln2_g[0], ln2_b[0], w_ple_gate[0], w_ple_proj[0])
    norm_g = gla_norm_g[0].reshape(1, GLA_V)

    xs = x_sample.reshape(n, d)
    rope_s = _rope_table(jnp.full((n,), PAST_LEN, jnp.int32))
    qt, kt, vt, qgt, kgt, vg_s, rg_s, lat, gates_s = _in_proj(
        xs, rope_s, proj_w, tm=n, rope_period_tiles=1, sample=True)

    xp = x_prompt.reshape(b * s, d)
    rope_p = _rope_table(jnp.arange(s))
    q, k, v, gates, ob, st_p, ob_s, st_s, *kv_tails = _in_proj(
        xp, rope_p, proj_w, tm=PROMPT_TM, rope_period_tiles=s // PROMPT_TM, sample=False,
        norm_g=norm_g, gla_sample=(qgt, kgt, lat, vg_s, rg_s, state_gla[0]))
    three = lambda t: t.reshape(b, s, t.shape[-1])
    oa = _attn_prompt(three(q), three(k), three(v))
    kv_p = [_positions_major(t)[None] for t in kv_tails]

    caches = [_positions_minor(c[0]) for c in (cache_a1_kv, cache_a2_kv, cache_a3_kv)]
    y_p, *new_caches, y_s = _post(
        xp, oa.reshape(b * s, GROUP_COLS), ob, gates, p_prompt[0].reshape(b * s, PLE_DIM), post_w,
        sample_attn=(qt, kt, vt, caches),
        sample_rows=(xs, ob_s, gates_s, p_sample[0].reshape(n, PLE_DIM)))
    kv_s = [_positions_major(c)[None] for c in new_caches]

    return (y_p.reshape(b, s, d), y_s.reshape(n, 1, d), kv_p[0], kv_p[1], kv_p[2], st_p[None],
            kv_s[0], kv_s[1], kv_s[2], st_s[None])
```

```python
import functools

import numpy as np
import jax
import jax.numpy as jnp
from jax import lax
from jax.experimental import pallas as pl
from jax.experimental.pallas import tpu as pltpu

F32 = jnp.float32
BF16 = jnp.bfloat16

D_MODEL = 1024
HEAD_DIM = 64
HEADS_PER_GROUP = 4
DILATED_GROUPS = ((128, 1), (512, 4), (2048, 16))
GROUP_COLS = HEADS_PER_GROUP * HEAD_DIM
A_WIDTH = len(DILATED_GROUPS) * GROUP_COLS
ROT_DIM = HEAD_DIM // 4
ROPE_THETA = 500000.0
KEYS_PER_WINDOW = 128
GLA_HEADS = 4
GLA_DK = 128
GLA_DV = 256
GLA_K = GLA_HEADS * GLA_DK
GLA_V = GLA_HEADS * GLA_DV
GATE_RANK = 16
GATE_NORMALIZER = 16.0
GLA_CHUNK = 128
D_FF = 2816
PLE_DIM = 256
DN_ALPHA = 2.0 ** 0.25
LN_EPS = 1e-5
RMS_EPS = 1e-6
PAST_LEN = 8192
COL_GLA = 3 * A_WIDTH
COL_LOWRANK = COL_GLA + 2 * GLA_K + 2 * GLA_V
COL_GATES = COL_LOWRANK + GATE_RANK
IN_COLS = COL_GATES + 2 * D_MODEL

LANES = 128
SUBLANES = 8
VMEM_LIMIT = 60 * 1024 * 1024

GLA_FACTORED_RANGE = 40.0


def _dot(a, b):
    return jnp.dot(a, b, preferred_element_type=F32)


def _dot_nt(a, b):
    return lax.dot_general(a, b, (((1,), (1,)), ((), ())), preferred_element_type=F32)


def _dot_tn(a, b):
    return lax.dot_general(a, b, (((0,), (0,)), ((), ())), preferred_element_type=F32)


def _const_spec(shape):
    return pl.BlockSpec(shape, lambda *_: (0,) * len(shape), pipeline_mode=pl.Buffered(1))


def _layernorm(x, g, b):
    xc = x - jnp.mean(x, -1, keepdims=True)
    var = jnp.mean(xc * xc, -1, keepdims=True)
    return xc * lax.rsqrt(var + LN_EPS) * g + b


def _sigmoid(x):
    return 1.0 / (1.0 + jnp.exp(-x))


def _silu(x):
    return x * _sigmoid(x)


def _split3_bf16(x):
    hi = x.astype(BF16)
    r1 = x - hi.astype(F32)
    mid = r1.astype(BF16)
    lo = (r1 - mid.astype(F32)).astype(BF16)
    return hi, mid, lo


def _gla_sample_step(n, j, qt_ref, kt_ref, lat_ref, v_ref, st_ref, st_out_ref, o_scr):
    a = jnp.exp(_pick_column(lat_ref, n))
    k = _pick_column(kt_ref, n)
    q = _pick_column(qt_ref, n)
    for h in range(GLA_HEADS):
        ks = slice(h * GLA_DK, (h + 1) * GLA_DK)
        vs = slice(h * GLA_DV, (h + 1) * GLA_DV)
        st = a[ks] * st_ref[j, h] + k[ks] * v_ref[pl.ds(n, 1), vs]
        st_out_ref[j, h] = st
        o_scr[pl.ds(n, 1), vs] = jnp.sum(q[ks] * st, axis=0, keepdims=True)


def _gla_chunk(q, k, v, cum, s_t, factored):
    c = q.shape[0]
    ri = lax.broadcasted_iota(jnp.int32, (c, c), 0)
    ci = lax.broadcasted_iota(jnp.int32, (c, c), 1)
    last = cum[c - 1:c]
    decay = jnp.exp(last)
    qe = (q * jnp.exp(cum)).astype(BF16)
    vb = v.astype(BF16)
    if factored:
        ke = k * jnp.exp(-cum)
        a = jnp.where(ci <= ri, _dot_nt(qe, ke.astype(BF16)), 0.0)
        kd = ke * decay
    else:
        row_id = lax.broadcasted_iota(jnp.int32, cum.shape, 0)

        def col(j, a):
            cj = jnp.sum(jnp.where(row_id == j, cum, 0.0), axis=0, keepdims=True)
            kj = jnp.sum(jnp.where(row_id == j, k, 0.0), axis=0, keepdims=True)
            e = jnp.exp(jnp.where(row_id >= j, cum - cj, -jnp.inf))
            sj = jnp.sum(q * kj * e, axis=-1, keepdims=True)
            return jnp.where(ci == j, sj, a)

        a = lax.fori_loop(0, c, col, jnp.zeros((c, c), F32))
        kd = k * jnp.exp(last - cum)
    o = _dot_nt(qe, s_t.astype(BF16)) + _dot(a.astype(BF16), vb)
    return o, s_t * decay + _dot_tn(vb, kd.astype(BF16))


def _in_proj_kernel(x_ref, rope_ref, w_ref, wup_ref, bg_ref, *rest, sample, tiles_per_seq):
    tm = x_ref.shape[0]
    wa_ref = w_ref.at[0:COL_GLA]
    wg_ref = w_ref.at[COL_GLA:COL_LOWRANK]
    wlr_ref = w_ref.at[COL_LOWRANK:COL_GATES]
    wgates_ref = w_ref.at[COL_GATES:IN_COLS]
    xb = x_ref[...].astype(BF16)
    cos = rope_ref[:, 0:LANES]
    sin_lo = rope_ref[:, LANES:2 * LANES]
    sin_hi = rope_ref[:, 2 * LANES:3 * LANES]

    def rope(t):
        return (t * cos + pltpu.roll(t, LANES - ROT_DIM // 2, 1) * sin_lo
                + pltpu.roll(t, ROT_DIM // 2, 1) * sin_hi)

    def put(ref, col, val):
        if sample:
            ref[col * LANES:(col + 1) * LANES, :] = val.T
        else:
            ref[:, col * LANES:(col + 1) * LANES] = val

    def branch_a_and_gates(q_ref, k_ref, v_ref, gates_ref, tails=None):
        tiles = GROUP_COLS // LANES
        for c in range(A_WIDTH // GROUP_COLS):
            lo, hi = c * GROUP_COLS, (c + 1) * GROUP_COLS
            qc = _dot_nt(xb, wa_ref[lo:hi, :])
            kc = _dot_nt(xb, wa_ref[A_WIDTH + lo:A_WIDTH + hi, :])
            vc = _dot_nt(xb, wa_ref[2 * A_WIDTH + lo:2 * A_WIDTH + hi, :])
            for s in range(tiles):
                sl = slice(s * LANES, (s + 1) * LANES)
                k_rot = rope(kc[:, sl])
                put(q_ref, c * tiles + s, rope(qc[:, sl]) * (HEAD_DIM ** -0.5))
                put(k_ref, c * tiles + s, k_rot)
                put(v_ref, c * tiles + s, vc[:, sl])
                if tails is not None:
                    width = tails[c].shape[-1]
                    tails[c][0, sl, :] = k_rot[tm - width:, :].T
                    tails[c][1, sl, :] = vc[tm - width:, sl].T
        gates_ref[...] = _sigmoid(_dot_nt(xb, wgates_ref[...])).astype(BF16)

    def gla_projections():
        qg = _dot_nt(xb, wg_ref[0:GLA_K, :]) * (GLA_DK ** -0.5)
        kg = _dot_nt(xb, wg_ref[GLA_K:2 * GLA_K, :])
        vg = _dot_nt(xb, wg_ref[2 * GLA_K:2 * GLA_K + GLA_V, :])
        rg = _dot_nt(xb, wg_ref[2 * GLA_K + GLA_V:2 * GLA_K + 2 * GLA_V, :])
        return qg, kg, vg, rg

    def log_decay(rows_bf16):
        glr = _dot_nt(rows_bf16, wlr_ref[...])
        z = _dot(glr.astype(BF16), wup_ref[...]) + bg_ref[...]
        return -(jnp.maximum(-z, 0.0) + jnp.log1p(jnp.exp(-jnp.abs(z)))) * (1.0 / GATE_NORMALIZER)

    if sample:
        loga = log_decay(xb)
        q_ref, k_ref, v_ref, qg_ref, kg_ref, vg_ref, rg_ref, la_ref, gates_ref = rest
        branch_a_and_gates(q_ref, k_ref, v_ref, gates_ref)
        qg, kg, vg, rg = gla_projections()
        for s in range(GLA_K // LANES):
            sl = slice(s * LANES, (s + 1) * LANES)
            put(qg_ref, s, qg[:, sl])
            put(kg_ref, s, kg[:, sl])
            put(la_ref, s, loga[:, sl])
        vg_ref[...] = vg
        rg_ref[...] = rg.astype(BF16)
        return

    (x_next_ref, gn_ref, qts_ref, kts_ref, lats_ref, vs_ref, rs_ref, st_ref,
     q_ref, k_ref, v_ref, gates_ref, ob_ref, stp_ref, obs_ref, st_out_ref, *tails,
     st_t, o_scr, cum_scr, range_scr) = rest
    step = pl.program_id(0)
    tile_in_seq = step % tiles_per_seq

    @pl.when(tile_in_seq == 0)
    def _():
        st_t[...] = jnp.zeros_like(st_t)

    def stage_decay(rows_bf16):
        row = lax.broadcasted_iota(jnp.int32, (tm, tm), 0)
        col = lax.broadcasted_iota(jnp.int32, (tm, tm), 1)
        tri = ((row // GLA_CHUNK == col // GLA_CHUNK) & (col <= row)).astype(BF16)
        hi, mid, lo = _split3_bf16(log_decay(rows_bf16))
        cum = _dot(tri, hi) + _dot(tri, mid) + _dot(tri, lo)
        cum_scr[...] = cum
        range_scr[0] = -jnp.min(cum)

    @pl.when(step == 0)
    def _():
        stage_decay(xb)

    def rest_of_tile(factored):
        cum = cum_scr[...]
        stage_decay(x_next_ref[...].astype(BF16))
        qg, kg, vg, rg = gla_projections()
        per_step = st_ref.shape[0]
        for j in range(per_step):
            _gla_sample_step(step * per_step + j, j, qts_ref, kts_ref, lats_ref, vs_ref,
                             st_ref, st_out_ref, o_scr)
        for ch in range(tm // GLA_CHUNK):
            rows = slice(ch * GLA_CHUNK, (ch + 1) * GLA_CHUNK)
            for h in range(GLA_HEADS):
                ks = slice(h * GLA_DK, (h + 1) * GLA_DK)
                vs = slice(h * GLA_DV, (h + 1) * GLA_DV)
                o, st_t[h] = _gla_chunk(qg[rows, ks], kg[rows, ks], vg[rows, vs], cum[rows, ks],
                                        st_t[h], factored)
                ob_ref[rows, vs] = _gla_epilogue(o, rg[rows, vs], gn_ref[:, vs]).astype(ob_ref.dtype)
        branch_a_and_gates(q_ref, k_ref, v_ref, gates_ref, tails)

    total = range_scr[0]
    pl.when(total < GLA_FACTORED_RANGE)(functools.partial(rest_of_tile, True))
    pl.when(total >= GLA_FACTORED_RANGE)(functools.partial(rest_of_tile, False))

    @pl.when(tile_in_seq == tiles_per_seq - 1)
    def _():
        for h in range(GLA_HEADS):
            stp_ref[h] = st_t[h].T

    @pl.when(step == pl.num_programs(0) - 1)
    def _():
        for h in range(GLA_HEADS):
            vs = slice(h * GLA_DV, (h + 1) * GLA_DV)
            obs_ref[:, vs] = _gla_epilogue(o_scr[:, vs], rs_ref[:, vs], gn_ref[:, vs]).astype(obs_ref.dtype)


def _in_proj(x, rope_tab, weights, *, tm, rope_period_tiles, sample, norm_g=None, gla_sample=None):
    rows = x.shape[0]
    steps = rows // tm
    w, wup, bg = weights
    row = lambda width: pl.BlockSpec((tm, width), lambda i: (i, 0))
    whole = lambda shape: pl.BlockSpec(shape, lambda i: (0,) * len(shape))
    in_specs = [row(D_MODEL), pl.BlockSpec((tm, 3 * LANES), lambda i: (i % rope_period_tiles, 0)),
                _const_spec(w.shape), _const_spec(wup.shape), _const_spec(bg.shape)]
    next_tile = pl.BlockSpec((tm, D_MODEL), lambda i: (jnp.minimum(i + 1, steps - 1), 0))
    operands = [x, rope_tab, w, wup, bg]
    scratch = []
    if sample:
        assert tm == rows == LANES
        t = lambda width: (jax.ShapeDtypeStruct((width, rows), F32), whole((width, rows)))
        outs = [t(A_WIDTH), t(A_WIDTH), t(A_WIDTH), t(GLA_K), t(GLA_K),
                (jax.ShapeDtypeStruct((rows, GLA_V), F32), row(GLA_V)),
                (jax.ShapeDtypeStruct((rows, GLA_V), BF16), row(GLA_V)),
                t(GLA_K),
                (jax.ShapeDtypeStruct((rows, 2 * D_MODEL), BF16), row(2 * D_MODEL))]
    else:
        qgt, kgt, lat, vg, rg, state = gla_sample
        n = state.shape[0]
        seqs = steps // rope_period_tiles
        assert n % steps == 0, "the same number of requests per grid step"
        assert tm % GLA_CHUNK == 0
        st_spec = pl.BlockSpec((n // steps,) + state.shape[1:], lambda i: (i, 0, 0, 0))
        in_specs += [next_tile] + [_const_spec(a.shape) for a in (norm_g, qgt, kgt, lat, vg, rg)] + [st_spec]
        operands += [x, norm_g, qgt, kgt, lat, vg, rg, state]
        outs = [(jax.ShapeDtypeStruct((rows, A_WIDTH), F32), row(A_WIDTH)),
                (jax.ShapeDtypeStruct((rows, A_WIDTH), F32), row(A_WIDTH)),
                (jax.ShapeDtypeStruct((rows, A_WIDTH), F32), row(A_WIDTH)),
                (jax.ShapeDtypeStruct((rows, 2 * D_MODEL), BF16), row(2 * D_MODEL)),
                (jax.ShapeDtypeStruct((rows, GLA_V), BF16), row(GLA_V)),
                (jax.ShapeDtypeStruct((seqs, GLA_HEADS, GLA_DK, GLA_DV), F32),
                 pl.BlockSpec((None, GLA_HEADS, GLA_DK, GLA_DV), lambda i: (i // rope_period_tiles, 0, 0, 0))),
                (jax.ShapeDtypeStruct((n, GLA_V), BF16), whole((n, GLA_V))),
                (jax.ShapeDtypeStruct(state.shape, F32), st_spec)]
        seq_len = rope_period_tiles * tm
        for window, _ in DILATED_GROUPS:
            keep = min(window, seq_len)
            width = min(keep, tm)
            first_kept = rope_period_tiles - keep // width
            outs.append((
                jax.ShapeDtypeStruct((seqs, 2, GROUP_COLS, keep), F32),
                pl.BlockSpec((None, 2, GROUP_COLS, width),
                             lambda i, first_kept=first_kept: (
                                 i // rope_period_tiles, 0, 0,
                                 jnp.maximum(i % rope_period_tiles - first_kept, 0)))))
        scratch = [pltpu.VMEM((GLA_HEADS, GLA_DV, GLA_DK), F32), pltpu.VMEM((n, GLA_V), F32),
                   pltpu.VMEM((tm, GLA_K), F32), pltpu.SMEM((1,), F32)]
    return pl.pallas_call(
        functools.partial(_in_proj_kernel, sample=sample, tiles_per_seq=rope_period_tiles),
        grid=(steps,),
        in_specs=in_specs,
        out_specs=tuple(o[1] for o in outs),
        out_shape=tuple(o[0] for o in outs),
        scratch_shapes=scratch,
        compiler_params=pltpu.CompilerParams(
            dimension_semantics=("parallel" if sample else "arbitrary",), vmem_limit_bytes=VMEM_LIMIT),
        name="in_proj_sample" if sample else "in_proj_prompt",
    )(*operands)


def _rope_table(pos):
    half = ROT_DIM // 2
    inv_freq = ROPE_THETA ** (-jnp.arange(0, ROT_DIM, 2, dtype=F32) / ROT_DIM)
    ang = pos.astype(F32)[:, None] * inv_freq[None, :]
    cos, sin = jnp.cos(ang), jnp.sin(ang)
    lane = np.arange(LANES) % HEAD_DIM
    fidx = lane % half
    first = lane < half
    second = (lane >= half) & (lane < ROT_DIM)
    cos_t = jnp.where((first | second)[None, :], cos[:, fidx], 1.0)
    sin_lo = jnp.where(first[None, :], -sin[:, fidx], 0.0)
    sin_hi = jnp.where(second[None, :], sin[:, fidx], 0.0)
    return jnp.concatenate([cos_t, sin_lo, sin_hi], axis=1)


ATT_TILE = 2048
Q_BLOCK = 128
HEADS_PER_TILE = LANES // HEAD_DIM


def _attn_prompt_kernel(q_ref, kc_ref, kp_ref, vc_ref, vp_ref, o_ref, og, lg, bias):
    t = pl.program_id(1)
    g = pl.program_id(3)
    band_keys = 2 * Q_BLOCK
    qi = lax.broadcasted_iota(jnp.int32, (Q_BLOCK, HEADS_PER_TILE * band_keys), 0)
    kc = lax.broadcasted_iota(jnp.int32, (Q_BLOCK, HEADS_PER_TILE * band_keys), 1) % band_keys
    back = qi + Q_BLOCK - kc
    band = (back >= 0) & (back <= KEYS_PER_WINDOW)
    @pl.when(g == 0)
    def _():
        bias[0] = jnp.where(band, 0.0, -jnp.inf)
        bias[1] = jnp.where(band & (kc >= Q_BLOCK), 0.0, -jnp.inf)
    head0 = lax.broadcasted_iota(jnp.int32, (band_keys, LANES), 1) < HEAD_DIM
    head0_q = lax.broadcasted_iota(jnp.int32, (Q_BLOCK, LANES), 1) < HEAD_DIM

    for gi, (window, dil) in enumerate(DILATED_GROUPS):

        @pl.when(g == gi)
        def _(gi=gi, window=window, dil=dil):
            nblk = ATT_TILE // dil // Q_BLOCK

            def rows(start):
                return pl.ds(start, Q_BLOCK) if dil == 1 else pl.ds(start, Q_BLOCK, stride=dil)

            def block(r, n):
                base = r + dil * Q_BLOCK * n
                q2 = q_ref[rows(base), :].astype(BF16)
                if n == 0:
                    prev = rows(ATT_TILE - window + r)
                    k_prev, v_prev = kp_ref[prev, :], vp_ref[prev, :]
                    mask = bias[(t == 0).astype(jnp.int32)]
                else:
                    prev = rows(base - dil * Q_BLOCK)
                    k_prev, v_prev = kc_ref[prev, :], vc_ref[prev, :]
                    mask = bias[0]
                k2 = jnp.concatenate([k_prev, kc_ref[rows(base), :]], axis=0)
                v2 = jnp.concatenate([v_prev, vc_ref[rows(base), :]], axis=0)
                kz = jnp.concatenate([jnp.where(head0, k2, 0.0), jnp.where(head0, 0.0, k2)], axis=0)
                s = _dot_nt(q2, kz.astype(BF16)) + mask
                s0, s1 = s[:, :band_keys], s[:, band_keys:]
                m0 = jnp.max(s0, axis=-1, keepdims=True)
                m1 = jnp.max(s1, axis=-1, keepdims=True)
                p = jnp.concatenate([jnp.exp(s0 - m0), jnp.exp(s1 - m1)], axis=1).astype(BF16)
                ones0 = head0.astype(F32)
                vz = jnp.concatenate([
                    jnp.concatenate([jnp.where(head0, v2, 0.0), ones0], axis=1),
                    jnp.concatenate([jnp.where(head0, 0.0, v2), 1.0 - ones0], axis=1)], axis=0)
                res = _dot(p, vz.astype(BF16))
                den = res[:, LANES:]
                og[gi, rows(base), :] = res[:, :LANES] / den
                m = jnp.where(head0_q, jnp.broadcast_to(m0, den.shape), jnp.broadcast_to(m1, den.shape))
                lg[gi, rows(base), :] = m + jnp.log(den)

            for r in range(dil):
                for n in range(nblk):
                    block(r, n)

    @pl.when(g == len(DILATED_GROUPS) - 1)
    def _():
        rows = 256

        def combine(c, carry):
            r0 = pl.multiple_of(c * rows, rows)
            l0 = lg[0, pl.ds(r0, rows), :]
            l1 = lg[1, pl.ds(r0, rows), :]
            l2 = lg[2, pl.ds(r0, rows), :]
            mx = jnp.maximum(jnp.maximum(l0, l1), l2)
            e0, e1, e2 = jnp.exp(l0 - mx), jnp.exp(l1 - mx), jnp.exp(l2 - mx)
            acc = (e0 * og[0, pl.ds(r0, rows), :] + e1 * og[1, pl.ds(r0, rows), :]
                   + e2 * og[2, pl.ds(r0, rows), :])
            o_ref[pl.ds(r0, rows), :] = (acc / (e0 + e1 + e2)).astype(o_ref.dtype)
            return carry

        lax.fori_loop(0, ATT_TILE // rows, combine, 0)


def _attn_prompt(q, k, v):
    b, s, _ = q.shape
    tiles_per_group = GROUP_COLS // LANES
    cur = pl.BlockSpec((None, ATT_TILE, LANES), lambda bi, t, c, g: (bi, t, g * tiles_per_group + c))
    prev = pl.BlockSpec((None, ATT_TILE, LANES),
                        lambda bi, t, c, g: (bi, jnp.maximum(t - 1, 0), g * tiles_per_group + c))
    assert all(w <= ATT_TILE for w, _ in DILATED_GROUPS), "the key halo must fit in the previous tile"
    ngroups = len(DILATED_GROUPS)
    return pl.pallas_call(
        _attn_prompt_kernel,
        grid=(b, s // ATT_TILE, tiles_per_group, ngroups),
        in_specs=[cur, cur, prev, cur, prev],
        out_specs=pl.BlockSpec((None, ATT_TILE, LANES), lambda bi, t, c, g: (bi, t, c)),
        out_shape=jax.ShapeDtypeStruct((b, s, GROUP_COLS), BF16),
        scratch_shapes=[
            pltpu.VMEM((ngroups, ATT_TILE, LANES), F32),
            pltpu.VMEM((ngroups, ATT_TILE, LANES), F32),
            pltpu.VMEM((2, Q_BLOCK, HEADS_PER_TILE * 2 * Q_BLOCK), F32),
        ],
        compiler_params=pltpu.CompilerParams(
            dimension_semantics=("parallel", "parallel", "parallel", "arbitrary"),
            vmem_limit_bytes=VMEM_LIMIT),
        name="attn_prompt",
    )(q, k, k, v, v)


def _gla_epilogue(o, r, g):
    o = o * lax.rsqrt(jnp.mean(o * o, -1, keepdims=True) + RMS_EPS)
    return o * g * _silu(r.astype(F32))


def _pick_column(ref, n):
    lane = lax.broadcasted_iota(jnp.int32, ref.shape, 1)
    return jnp.sum(jnp.where(lane == n, ref[...], 0.0), axis=1, keepdims=True)


def _attn_sample_step(n, qt_ref, kt_ref, vt_ref, caches, news, o_cols):
    ngroups = len(DILATED_GROUPS)
    qcol = _pick_column(qt_ref, n)
    kcol = _pick_column(kt_ref, n)
    vcol = _pick_column(vt_ref, n)

    def shifted(x, new):
        lb = x.shape[1]
        lane = lax.broadcasted_iota(jnp.int32, x.shape, 1)
        return jnp.where(lane == lb - 1, new, pltpu.roll(x, lb - 1, 1))

    heads = []
    for h in range(HEADS_PER_GROUP):
        s_old, s_new = [], []
        for gi, (window, dil) in enumerate(DILATED_GROUPS):
            lb = caches[gi].shape[1]
            r0 = gi * GROUP_COLS + h * HEAD_DIM
            q1 = qcol[r0:r0 + HEAD_DIM]
            k1 = kcol[r0:r0 + HEAD_DIM]
            keys = caches[gi][h * HEAD_DIM:(h + 1) * HEAD_DIM, :]
            news[gi][h * HEAD_DIM:(h + 1) * HEAD_DIM, :] = shifted(keys, k1)
            s = jnp.sum(keys * q1, axis=0, keepdims=True)
            pos = lax.broadcasted_iota(jnp.int32, (1, lb), 1)
            s_old.append(jnp.where(pos % dil == 0, s, -jnp.inf))
            s_new.append(jnp.sum(k1 * q1, axis=0, keepdims=True))
        m = s_new[0]
        for gi in range(ngroups):
            m = jnp.maximum(m, jnp.maximum(s_new[gi], jnp.max(s_old[gi], axis=1, keepdims=True)))
        den = jnp.zeros((1, 1), F32)
        acc = jnp.zeros((HEAD_DIM, 1), F32)
        for gi in range(ngroups):
            r0 = gi * GROUP_COLS + h * HEAD_DIM
            v1 = vcol[r0:r0 + HEAD_DIM]
            p_old = jnp.exp(s_old[gi] - m)
            p_new = jnp.exp(s_new[gi] - m)
            den = den + jnp.sum(p_old, axis=1, keepdims=True) + p_new
            rows = slice(GROUP_COLS + h * HEAD_DIM, GROUP_COLS + (h + 1) * HEAD_DIM)
            vals = caches[gi][rows, :]
            news[gi][rows, :] = shifted(vals, v1)
            acc = acc + jnp.sum(vals * p_old, axis=1, keepdims=True) + v1 * p_new
        heads.append(acc / den)
    o_col = jnp.concatenate(heads, axis=0)
    lane = lax.broadcasted_iota(jnp.int32, o_cols.shape, 1)
    o_cols[...] = jnp.where(lane == n, o_col, o_cols[...])


def _post_kernel(x_ref, oa_ref, ob_ref, gates_ref, pe_ref,
                 wa_ref, wb_ref, wo_ref, wfg_ref, wfu_ref, wfd_ref, wpg_ref, wpp_ref,
                 ln1g_ref, ln1b_ref, ln2g_ref, ln2b_ref,
                 qt_ref, kt_ref, vt_ref, c1_ref, c2_ref, c3_ref, xs_ref, obs_ref, gs_ref, pes_ref,
                 y_ref, n1_ref, n2_ref, n3_ref, ys_ref, o_cols, h_carry):
    i = pl.program_id(0)
    last = pl.num_programs(0) - 1

    @pl.when(i == 0)
    def _():
        o_cols[...] = jnp.zeros_like(o_cols)
        h_carry[...] = jnp.zeros_like(h_carry)

    request = jnp.minimum(i, qt_ref.shape[1] - 1)
    _attn_sample_step(request, qt_ref, kt_ref, vt_ref, (c1_ref, c2_ref, c3_ref),
                      (n1_ref, n2_ref, n3_ref), o_cols)

    def merge_and_project(x, oa, ob, gates):
        ga = gates[:, 0:D_MODEL].astype(F32)
        gb = gates[:, D_MODEL:2 * D_MODEL].astype(F32)
        merged = (ga * _dot(oa.astype(BF16), wa_ref[...]) + gb * _dot(ob.astype(BF16), wb_ref[...]))
        return DN_ALPHA * x + _dot(merged.astype(BF16), wo_ref[...])

    def norm_ffn_ple(h, pe):
        x1 = _layernorm(h, ln1g_ref[...], ln1b_ref[...])
        x1b = x1.astype(BF16)
        act = _silu(_dot(x1b, wfg_ref[...])) * _dot(x1b, wfu_ref[...])
        x2 = _layernorm(DN_ALPHA * x1 + _dot(act.astype(BF16), wfd_ref[...]), ln2g_ref[...], ln2b_ref[...])
        gate = _sigmoid(_dot(x2.astype(BF16), wpg_ref[...]))
        return x2 + gate * _dot(pe.astype(BF16), wpp_ref[...])

    h_new = merge_and_project(x_ref[...], oa_ref[...], ob_ref[...], gates_ref[...])
    y_ref[...] = norm_ffn_ple(h_carry[...], pe_ref[...])
    h_carry[...] = h_new

    @pl.when(i == last)
    def _():
        oa_s = jnp.concatenate([o_cols[c * LANES:(c + 1) * LANES, :].T for c in range(GROUP_COLS // LANES)],
                               axis=1)
        h_s = merge_and_project(xs_ref[...], oa_s, obs_ref[...], gs_ref[...])
        ys_ref[...] = norm_ffn_ple(h_s, pes_ref[...])


def _post(x, oa, ob, gates, pe, weights, *, sample_attn, sample_rows):
    rows = x.shape[0]
    qt, kt, vt, caches = sample_attn
    n = qt.shape[1]
    assert n == LANES and rows % n == 0, "one request per row tile"
    tm = rows // n
    tiles = n
    for (window, _), c in zip(DILATED_GROUPS, caches):
        assert c.shape[2] == window, "every key distance must fall inside the cache"
    stage1 = lambda w: pl.BlockSpec((tm, w), lambda i: (jnp.minimum(i, tiles - 1), 0))
    stage2 = lambda w: pl.BlockSpec((tm, w), lambda i: (jnp.maximum(i - 1, 0), 0))
    cache_specs = [pl.BlockSpec((None,) + c.shape[1:], lambda i: (jnp.minimum(i, n - 1), 0, 0)) for c in caches]
    in_specs = ([stage1(D_MODEL), stage1(GROUP_COLS), stage1(GLA_V), stage1(2 * D_MODEL), stage2(PLE_DIM)]
                + [_const_spec(w.shape) for w in weights]
                + [_const_spec(qt.shape)] * 3 + cache_specs
                + [_const_spec(a.shape) for a in sample_rows])
    out_specs = [stage2(D_MODEL)] + cache_specs + [pl.BlockSpec((n, D_MODEL), lambda i: (0, 0))]
    out_shape = ([jax.ShapeDtypeStruct((rows, D_MODEL), F32)]
                 + [jax.ShapeDtypeStruct(c.shape, c.dtype) for c in caches]
                 + [jax.ShapeDtypeStruct((n, D_MODEL), F32)])
    return pl.pallas_call(
        _post_kernel,
        grid=(tiles + 1,),
        in_specs=in_specs,
        out_specs=tuple(out_specs),
        out_shape=tuple(out_shape),
        scratch_shapes=[pltpu.VMEM((GROUP_COLS, n), F32), pltpu.VMEM((tm, D_MODEL), F32)],
        compiler_params=pltpu.CompilerParams(
            dimension_semantics=("arbitrary",), vmem_limit_bytes=VMEM_LIMIT),
        name="post",
    )(x, oa, ob, gates, pe, *weights, qt, kt, vt, *caches, *sample_rows)


PROMPT_TM = 256


def _layer_weights(w_in, w_gate_up, b_gate, w_a_out, w_b_out, w_o, ln1_g, ln1_b,
                   w_ff_gate, w_ff_up, w_ff_down, ln2_g, ln2_b, w_ple_gate, w_ple_proj):
    proj = (w_in.T.astype(BF16), w_gate_up.astype(BF16), b_gate.reshape(1, GLA_K))
    row = lambda p: p.reshape(1, -1)
    post = (w_a_out.astype(BF16), w_b_out.astype(BF16), w_o.astype(BF16), w_ff_gate.astype(BF16),
            w_ff_up.astype(BF16), w_ff_down.astype(BF16), w_ple_gate.astype(BF16),
            w_ple_proj.astype(BF16), row(ln1_g), row(ln1_b), row(ln2_g), row(ln2_b))
    return proj, post


def _positions_minor(cache):
    n, lb = cache.shape[:2]
    return jnp.transpose(cache, (0, 2, 3, 4, 1)).reshape(n, 2 * GROUP_COLS, lb)


def _positions_major(cache):
    n, lb = cache.shape[0], cache.shape[-1]
    return jnp.transpose(cache.reshape(n, 2, HEADS_PER_GROUP, HEAD_DIM, lb), (0, 4, 1, 2, 3))


def kernel(x_prompt, x_sample, cache_a1_kv, cache_a2_kv, cache_a3_kv, state_gla, p_prompt, p_sample, w_in, w_gate_up, b_gate, gla_norm_g, w_a_out, w_b_out, w_o, ln1_g, ln1_b, w_ff_gate, w_ff_up, w_ff_down, ln2_g, ln2_b, w_ple_gate, w_ple_proj):
    depth = w_in.shape[0]
    assert depth == 1 and x_sample.shape[1] == 1
    b, s, d = x_prompt.shape
    n = x_sample.shape[0]
    proj_w, post_w = _layer_weights(
        w_in[0], w_gate_up[0], b_gate[0], w_a_out[0], w_b_out[0], w_o[0], ln1_g[0], ln1_b[0],
        w_ff_gate[0], w_ff_up[0], w_ff_down[0], ln2_g[0], ln2_b[0], w_ple_gate[0], w_ple_proj[0])
    norm_g = gla_norm_g[0].reshape(1, GLA_V)

    xs = x_sample.reshape(n, d)
    rope_s = _rope_table(jnp.full((n,), PAST_LEN, jnp.int32))
    qt, kt, vt, qgt, kgt, vg_s, rg_s, lat, gates_s = _in_proj(
        xs, rope_s, proj_w, tm=n, rope_period_tiles=1, sample=True)

    xp = x_prompt.reshape(b * s, d)
    rope_p = _rope_table(jnp.arange(s))
    q, k, v, gates, ob, st_p, ob_s, st_s, *kv_tails = _in_proj(
        xp, rope_p, proj_w, tm=PROMPT_TM, rope_period_tiles=s // PROMPT_TM, sample=False,
        norm_g=norm_g, gla_sample=(qgt, kgt, lat, vg_s, rg_s, state_gla[0]))
    three = lambda t: t.reshape(b, s, t.shape[-1])
    oa = _attn_prompt(three(q), three(k), three(v))
    kv_p = [_positions_major(t)[None] for t in kv_tails]

    caches = [_positions_minor(c[0]) for c in (cache_a1_kv, cache_a2_kv, cache_a3_kv)]
    y_p, *new_caches, y_s = _post(
        xp, oa.reshape(b * s, GROUP_COLS), ob, gates, p_prompt[0].reshape(b * s, PLE_DIM), post_w,
        sample_attn=(qt, kt, vt, caches),
        sample_rows=(xs, ob_s, gates_s, p_sample[0].reshape(n, PLE_DIM)))
    kv_s = [_positions_major(c)[None] for c in new_caches]

    return (y_p.reshape(b, s, d), y_s.reshape(n, 1, d), kv_p[0], kv_p[1], kv_p[2], st_p[None],
            kv_s[0], kv_s[1], kv_s[2], st_s[None])
```

```python
import functools

import numpy as np
import jax
import jax.numpy as jnp
from jax import lax
from jax.experimental import pallas as pl
from jax.experimental.pallas import tpu as pltpu

F32 = jnp.float32
BF16 = jnp.bfloat16

D_MODEL = 1024
HEAD_DIM = 64
HEADS_PER_GROUP = 4
DILATED_GROUPS = ((128, 1), (512, 4), (2048, 16))
GROUP_COLS = HEADS_PER_GROUP * HEAD_DIM
A_WIDTH = len(DILATED_GROUPS) * GROUP_COLS
ROT_DIM = HEAD_DIM // 4
ROPE_THETA = 500000.0
KEYS_PER_WINDOW = 128
GLA_HEADS = 4
GLA_DK = 128
GLA_DV = 256
GLA_K = GLA_HEADS * GLA_DK
GLA_V = GLA_HEADS * GLA_DV
GATE_RANK = 16
GATE_NORMALIZER = 16.0
GLA_CHUNK = 128
D_FF = 2816
PLE_DIM = 256
DN_ALPHA = 2.0 ** 0.25
LN_EPS = 1e-5
RMS_EPS = 1e-6
PAST_LEN = 8192
COL_GLA = 3 * A_WIDTH
COL_LOWRANK = COL_GLA + 2 * GLA_K + 2 * GLA_V
COL_GATES = COL_LOWRANK + GATE_RANK
IN_COLS = COL_GATES + 2 * D_MODEL

LANES = 128
SUBLANES = 8
VMEM_LIMIT = 60 * 1024 * 1024

GLA_FACTORED_RANGE = 40.0


def _dot(a, b):
    return jnp.dot(a, b, preferred_element_type=F32)


def _dot_nt(a, b):
    return lax.dot_general(a, b, (((1,), (1,)), ((), ())), preferred_element_type=F32)


def _dot_tn(a, b):
    return lax.dot_general(a, b, (((0,), (0,)), ((), ())), preferred_element_type=F32)


def _const_spec(shape):
    return pl.BlockSpec(shape, lambda *_: (0,) * len(shape), pipeline_mode=pl.Buffered(1))


def _layernorm(x, g, b):
    xc = x - jnp.mean(x, -1, keepdims=True)
    var = jnp.mean(xc * xc, -1, keepdims=True)
    return xc * lax.rsqrt(var + LN_EPS) * g + b


def _sigmoid(x):
    return 1.0 / (1.0 + jnp.exp(-x))


def _silu(x):
    return x * _sigmoid(x)


def _split3_bf16(x):
    hi = x.astype(BF16)
    r1 = x - hi.astype(F32)
    mid = r1.astype(BF16)
    lo = (r1 - mid.astype(F32)).astype(BF16)
    return hi, mid, lo


def _gla_sample_step(n, j, qt_ref, kt_ref, lat_ref, v_ref, st_ref, st_out_ref, o_scr):
    a = jnp.exp(_pick_column(lat_ref, n))
    k = _pick_column(kt_ref, n)
    q = _pick_column(qt_ref, n)
    for h in range(GLA_HEADS):
        ks = slice(h * GLA_DK, (h + 1) * GLA_DK)
        vs = slice(h * GLA_DV, (h + 1) * GLA_DV)
        st = a[ks] * st_ref[j, h] + k[ks] * v_ref[pl.ds(n, 1), vs]
        st_out_ref[j, h] = st
        o_scr[pl.ds(n, 1), vs] = jnp.sum(q[ks] * st, axis=0, keepdims=True)


def _gla_chunk(q, k, v, cum, s_t, factored):
    c = q.shape[0]
    ri = lax.broadcasted_iota(jnp.int32, (c, c), 0)
    ci = lax.broadcasted_iota(jnp.int32, (c, c), 1)
    last = cum[c - 1:c]
    decay = jnp.exp(last)
    qe = (q * jnp.exp(cum)).astype(BF16)
    vb = v.astype(BF16)
    if factored:
        ke = k * jnp.exp(-cum)
        a = jnp.where(ci <= ri, _dot_nt(qe, ke.astype(BF16)), 0.0)
        kd = ke * decay
    else:
        row_id = lax.broadcasted_iota(jnp.int32, cum.shape, 0)

        def col(j, a):
            cj = jnp.sum(jnp.where(row_id == j, cum, 0.0), axis=0, keepdims=True)
            kj = jnp.sum(jnp.where(row_id == j, k, 0.0), axis=0, keepdims=True)
            e = jnp.exp(jnp.where(row_id >= j, cum - cj, -jnp.inf))
            sj = jnp.sum(q * kj * e, axis=-1, keepdims=True)
            return jnp.where(ci == j, sj, a)

        a = lax.fori_loop(0, c, col, jnp.zeros((c, c), F32))
        kd = k * jnp.exp(last - cum)
    o = _dot_nt(qe, s_t.astype(BF16)) + _dot(a.astype(BF16), vb)
    return o, s_t * decay + _dot_tn(vb, kd.astype(BF16))


def _in_proj_kernel(x_ref, rope_ref, w_ref, wup_ref, bg_ref, *rest, sample, tiles_per_seq):
    tm = x_ref.shape[0]
    wa_ref = w_ref.at[0:COL_GLA]
    wg_ref = w_ref.at[COL_GLA:COL_LOWRANK]
    wlr_ref = w_ref.at[COL_LOWRANK:COL_GATES]
    wgates_ref = w_ref.at[COL_GATES:IN_COLS]
    xb = x_ref[...].astype(BF16)
    cos = rope_ref[:, 0:LANES]
    sin_lo = rope_ref[:, LANES:2 * LANES]
    sin_hi = rope_ref[:, 2 * LANES:3 * LANES]

    def rope(t):
        return (t * cos + pltpu.roll(t, LANES - ROT_DIM // 2, 1) * sin_lo
                + pltpu.roll(t, ROT_DIM // 2, 1) * sin_hi)

    def put(ref, col, val):
        if sample:
            ref[col * LANES:(col + 1) * LANES, :] = val.T
        else:
            ref[:, col * LANES:(col + 1) * LANES] = val

    def branch_a_and_gates(q_ref, k_ref, v_ref, gates_ref, tails=None):
        tiles = GROUP_COLS // LANES
        for c in range(A_WIDTH // GROUP_COLS):
            lo, hi = c * GROUP_COLS, (c + 1) * GROUP_COLS
            qc = _dot_nt(xb, wa_ref[lo:hi, :])
            kc = _dot_nt(xb, wa_ref[A_WIDTH + lo:A_WIDTH + hi, :])
            vc = _dot_nt(xb, wa_ref[2 * A_WIDTH + lo:2 * A_WIDTH + hi, :])
            for s in range(tiles):
                sl = slice(s * LANES, (s + 1) * LANES)
                k_rot = rope(kc[:, sl])
                put(q_ref, c * tiles + s, rope(qc[:, sl]) * (HEAD_DIM ** -0.5))
                put(k_ref, c * tiles + s, k_rot)
                put(v_ref, c * tiles + s, vc[:, sl])
                if tails is not None:
                    width = tails[c].shape[-1]
                    tails[c][0, sl, :] = k_rot[tm - width:, :].T
                    tails[c][1, sl, :] = vc[tm - width:, sl].T
        gates_ref[...] = _sigmoid(_dot_nt(xb, wgates_ref[...])).astype(BF16)

    def gla_projections():
        qg = _dot_nt(xb, wg_ref[0:GLA_K, :]) * (GLA_DK ** -0.5)
        kg = _dot_nt(xb, wg_ref[GLA_K:2 * GLA_K, :])
        vg = _dot_nt(xb, wg_ref[2 * GLA_K:2 * GLA_K + GLA_V, :])
        rg = _dot_nt(xb, wg_ref[2 * GLA_K + GLA_V:2 * GLA_K + 2 * GLA_V, :])
        return qg, kg, vg, rg

    def log_decay(rows_bf16):
        glr = _dot_nt(rows_bf16, wlr_ref[...])
        z = _dot(glr.astype(BF16), wup_ref[...]) + bg_ref[...]
        return -(jnp.maximum(-z, 0.0) + jnp.log1p(jnp.exp(-jnp.abs(z)))) * (1.0 / GATE_NORMALIZER)

    if sample:
        loga = log_decay(xb)
        q_ref, k_ref, v_ref, qg_ref, kg_ref, vg_ref, rg_ref, la_ref, gates_ref = rest
        branch_a_and_gates(q_ref, k_ref, v_ref, gates_ref)
        qg, kg, vg, rg = gla_projections()
        for s in range(GLA_K // LANES):
            sl = slice(s * LANES, (s + 1) * LANES)
            put(qg_ref, s, qg[:, sl])
            put(kg_ref, s, kg[:, sl])
            put(la_ref, s, loga[:, sl])
        vg_ref[...] = vg
        rg_ref[...] = rg.astype(BF16)
        return

    (x_next_ref, gn_ref, qts_ref, kts_ref, lats_ref, vs_ref, rs_ref, st_ref,
     q_ref, k_ref, v_ref, gates_ref, ob_ref, stp_ref, obs_ref, st_out_ref, *tails,
     st_t, o_scr, cum_scr, range_scr) = rest
    step = pl.program_id(0)
    tile_in_seq = step % tiles_per_seq

    @pl.when(tile_in_seq == 0)
    def _():
        st_t[...] = jnp.zeros_like(st_t)

    def stage_decay(rows_bf16):
        row = lax.broadcasted_iota(jnp.int32, (tm, tm), 0)
        col = lax.broadcasted_iota(jnp.int32, (tm, tm), 1)
        tri = ((row // GLA_CHUNK == col // GLA_CHUNK) & (col <= row)).astype(BF16)
        hi, mid, lo = _split3_bf16(log_decay(rows_bf16))
        cum = _dot(tri, hi) + _dot(tri, mid) + _dot(tri, lo)
        cum_scr[...] = cum
        range_scr[0] = -jnp.min(cum)

    @pl.when(step == 0)
    def _():
        stage_decay(xb)

    def rest_of_tile(factored):
        cum = cum_scr[...]
        stage_decay(x_next_ref[...].astype(BF16))
        qg, kg, vg, rg = gla_projections()
        per_step = st_ref.shape[0]
        for j in range(per_step):
            _gla_sample_step(step * per_step + j, j, qts_ref, kts_ref, lats_ref, vs_ref,
                             st_ref, st_out_ref, o_scr)
        for ch in range(tm // GLA_CHUNK):
            rows = slice(ch * GLA_CHUNK, (ch + 1) * GLA_CHUNK)
            for h in range(GLA_HEADS):
                ks = slice(h * GLA_DK, (h + 1) * GLA_DK)
                vs = slice(h * GLA_DV, (h + 1) * GLA_DV)
                o, st_t[h] = _gla_chunk(qg[rows, ks], kg[rows, ks], vg[rows, vs], cum[rows, ks],
                                        st_t[h], factored)
                ob_ref[rows, vs] = _gla_epilogue(o, rg[rows, vs], gn_ref[:, vs]).astype(ob_ref.dtype)
        branch_a_and_gates(q_ref, k_ref, v_ref, gates_ref, tails)

    total = range_scr[0]
    pl.when(total < GLA_FACTORED_RANGE)(functools.partial(rest_of_tile, True))
    pl.when(total >= GLA_FACTORED_RANGE)(functools.partial(rest_of_tile, False))

    @pl.when(tile_in_seq == tiles_per_seq - 1)
    def _():
        for h in range(GLA_HEADS):
            stp_ref[h] = st_t[h].T

    @pl.when(step == pl.num_programs(0) - 1)
    def _():
        for h in range(GLA_HEADS):
            vs = slice(h * GLA_DV, (h + 1) * GLA_DV)
            obs_ref[:, vs] = _gla_epilogue(o_scr[:, vs], rs_ref[:, vs], gn_ref[:, vs]).astype(obs_ref.dtype)


def _in_proj(x, rope_tab, weights, *, tm, rope_period_tiles, sample, norm_g=None, gla_sample=None):
    rows = x.shape[0]
    steps = rows // tm
    w, wup, bg = weights
    row = lambda width: pl.BlockSpec((tm, width), lambda i: (i, 0))
    whole = lambda shape: pl.BlockSpec(shape, lambda i: (0,) * len(shape))
    in_specs = [row(D_MODEL), pl.BlockSpec((tm, 3 * LANES), lambda i: (i % rope_period_tiles, 0)),
                _const_spec(w.shape), _const_spec(wup.shape), _const_spec(bg.shape)]
    next_tile = pl.BlockSpec((tm, D_MODEL), lambda i: (jnp.minimum(i + 1, steps - 1), 0))
    operands = [x, rope_tab, w, wup, bg]
    scratch = []
    if sample:
        assert tm == rows == LANES
        t = lambda width: (jax.ShapeDtypeStruct((width, rows), F32), whole((width, rows)))
        outs = [t(A_WIDTH), t(A_WIDTH), t(A_WIDTH), t(GLA_K), t(GLA_K),
                (jax.ShapeDtypeStruct((rows, GLA_V), F32), row(GLA_V)),
                (jax.ShapeDtypeStruct((rows, GLA_V), BF16), row(GLA_V)),
                t(GLA_K),
                (jax.ShapeDtypeStruct((rows, 2 * D_MODEL), BF16), row(2 * D_MODEL))]
    else:
        qgt, kgt, lat, vg, rg, state = gla_sample
        n = state.shape[0]
        seqs = steps // rope_period_tiles
        assert n % steps == 0, "the same number of requests per grid step"
        assert tm % GLA_CHUNK == 0
        st_spec = pl.BlockSpec((n // steps,) + state.shape[1:], lambda i: (i, 0, 0, 0))
        in_specs += [next_tile] + [_const_spec(a.shape) for a in (norm_g, qgt, kgt, lat, vg, rg)] + [st_spec]
        operands += [x, norm_g, qgt, kgt, lat, vg, rg, state]
        outs = [(jax.ShapeDtypeStruct((rows, A_WIDTH), F32), row(A_WIDTH)),
                (jax.ShapeDtypeStruct((rows, A_WIDTH), F32), row(A_WIDTH)),
                (jax.ShapeDtypeStruct((rows, A_WIDTH), F32), row(A_WIDTH)),
                (jax.ShapeDtypeStruct((rows, 2 * D_MODEL), BF16), row(2 * D_MODEL)),
                (jax.ShapeDtypeStruct((rows, GLA_V), BF16), row(GLA_V)),
                (jax.ShapeDtypeStruct((seqs, GLA_HEADS, GLA_DK, GLA_DV), F32),
                 pl.BlockSpec((None, GLA_HEADS, GLA_DK, GLA_DV), lambda i: (i // rope_period_tiles, 0, 0, 0))),
                (jax.ShapeDtypeStruct((n, GLA_V), BF16), whole((n, GLA_V))),
                (jax.ShapeDtypeStruct(state.shape, F32), st_spec)]
        seq_len = rope_period_tiles * tm
        for window, _ in DILATED_GROUPS:
            keep = min(window, seq_len)
            width = min(keep, tm)
            first_kept = rope_period_tiles - keep // width
            outs.append((
                jax.ShapeDtypeStruct((seqs, 2, GROUP_COLS, keep), F32),
                pl.BlockSpec((None, 2, GROUP_COLS, width),
                             lambda i, first_kept=first_kept: (
                                 i // rope_period_tiles, 0, 0,
                                 jnp.maximum(i % rope_period_tiles - first_kept, 0)))))
        scratch = [pltpu.VMEM((GLA_HEADS, GLA_DV, GLA_DK), F32), pltpu.VMEM((n, GLA_V), F32),
                   pltpu.VMEM((tm, GLA_K), F32), pltpu.SMEM((1,), F32)]
    return pl.pallas_call(
        functools.partial(_in_proj_kernel, sample=sample, tiles_per_seq=rope_period_tiles),
        grid=(steps,),
        in_specs=in_specs,
        out_specs=tuple(o[1] for o in outs),
        out_shape=tuple(o[0] for o in outs),
        scratch_shapes=scratch,
        compiler_params=pltpu.CompilerParams(
            dimension_semantics=("parallel" if sample else "arbitrary",), vmem_limit_bytes=VMEM_LIMIT),
        name="in_proj_sample" if sample else "in_proj_prompt",
    )(*operands)


def _rope_table(pos):
    half = ROT_DIM // 2
    inv_freq = ROPE_THETA ** (-jnp.arange(0, ROT_DIM, 2, dtype=F32) / ROT_DIM)
    ang = pos.astype(F32)[:, None] * inv_freq[None, :]
    cos, sin = jnp.cos(ang), jnp.sin(ang)
    lane = np.arange(LANES) % HEAD_DIM
    fidx = lane % half
    first = lane < half
    second = (lane >= half) & (lane < ROT_DIM)
    cos_t = jnp.where((first | second)[None, :], cos[:, fidx], 1.0)
    sin_lo = jnp.where(first[None, :], -sin[:, fidx], 0.0)
    sin_hi = jnp.where(second[None, :], sin[:, fidx], 0.0)
    return jnp.concatenate([cos_t, sin_lo, sin_hi], axis=1)


ATT_TILE = 2048
Q_BLOCK = 128
HEADS_PER_TILE = LANES // HEAD_DIM


def _attn_prompt_kernel(q_ref, kc_ref, kp_ref, vc_ref, vp_ref, o_ref, og, lg, bias):
    t = pl.program_id(1)
    g = pl.program_id(3)
    band_keys = 2 * Q_BLOCK
    qi = lax.broadcasted_iota(jnp.int32, (Q_BLOCK, HEADS_PER_TILE * band_keys), 0)
    kc = lax.broadcasted_iota(jnp.int32, (Q_BLOCK, HEADS_PER_TILE * band_keys), 1) % band_keys
    back = qi + Q_BLOCK - kc
    band = (back >= 0) & (back <= KEYS_PER_WINDOW)
    @pl.when(g == 0)
    def _():
        bias[0] = jnp.where(band, 0.0, -jnp.inf)
        bias[1] = jnp.where(band & (kc >= Q_BLOCK), 0.0, -jnp.inf)
    head0 = lax.broadcasted_iota(jnp.int32, (band_keys, LANES), 1) < HEAD_DIM
    head0_q = lax.broadcasted_iota(jnp.int32, (Q_BLOCK, LANES), 1) < HEAD_DIM

    for gi, (window, dil) in enumerate(DILATED_GROUPS):

        @pl.when(g == gi)
        def _(gi=gi, window=window, dil=dil):
            nblk = ATT_TILE // dil // Q_BLOCK

            def rows(start):
                return pl.ds(start, Q_BLOCK) if dil == 1 else pl.ds(start, Q_BLOCK, stride=dil)

            def block(r, n):
                base = r + dil * Q_BLOCK * n
                q2 = q_ref[rows(base), :].astype(BF16)
                if n == 0:
                    prev = rows(ATT_TILE - window + r)
                    k_prev, v_prev = kp_ref[prev, :], vp_ref[prev, :]
                    mask = bias[(t == 0).astype(jnp.int32)]
                else:
                    prev = rows(base - dil * Q_BLOCK)
                    k_prev, v_prev = kc_ref[prev, :], vc_ref[prev, :]
                    mask = bias[0]
                k2 = jnp.concatenate([k_prev, kc_ref[rows(base), :]], axis=0)
                v2 = jnp.concatenate([v_prev, vc_ref[rows(base), :]], axis=0)
                kz = jnp.concatenate([jnp.where(head0, k2, 0.0), jnp.where(head0, 0.0, k2)], axis=0)
                s = _dot_nt(q2, kz.astype(BF16)) + mask
                s0, s1 = s[:, :band_keys], s[:, band_keys:]
                m0 = jnp.max(s0, axis=-1, keepdims=True)
                m1 = jnp.max(s1, axis=-1, keepdims=True)
                p = jnp.concatenate([jnp.exp(s0 - m0), jnp.exp(s1 - m1)], axis=1).astype(BF16)
                ones0 = head0.astype(F32)
                vz = jnp.concatenate([
                    jnp.concatenate([jnp.where(head0, v2, 0.0), ones0], axis=1),
                    jnp.concatenate([jnp.where(head0, 0.0, v2), 1.0 - ones0], axis=1)], axis=0)
                res = _dot(p, vz.astype(BF16))
                den = res[:, LANES:]
                og[gi, rows(base), :] = res[:, :LANES] / den
                m = jnp.where(head0_q, jnp.broadcast_to(m0, den.shape), jnp.broadcast_to(m1, den.shape))
                lg[gi, rows(base), :] = m + jnp.log(den)

            for r in range(dil):
                for n in range(nblk):
                    block(r, n)

    @pl.when(g == len(DILATED_GROUPS) - 1)
    def _():
        rows = 256

        def combine(c, carry):
            r0 = pl.multiple_of(c * rows, rows)
            l0 = lg[0, pl.ds(r0, rows), :]
            l1 = lg[1, pl.ds(r0, rows), :]
            l2 = lg[2, pl.ds(r0, rows), :]
            mx = jnp.maximum(jnp.maximum(l0, l1), l2)
            e0, e1, e2 = jnp.exp(l0 - mx), jnp.exp(l1 - mx), jnp.exp(l2 - mx)
            acc = (e0 * og[0, pl.ds(r0, rows), :] + e1 * og[1, pl.ds(r0, rows), :]
                   + e2 * og[2, pl.ds(r0, rows), :])
            o_ref[pl.ds(r0, rows), :] = (acc / (e0 + e1 + e2)).astype(o_ref.dtype)
            return carry

        lax.fori_loop(0, ATT_TILE // rows, combine, 0)


def _attn_prompt(q, k, v):
    b, s, _ = q.shape
    tiles_per_group = GROUP_COLS // LANES
    cur = pl.BlockSpec((None, ATT_TILE, LANES), lambda bi, t, c, g: (bi, t, g * tiles_per_group + c))
    prev = pl.BlockSpec((None, ATT_TILE, LANES),
                        lambda bi, t, c, g: (bi, jnp.maximum(t - 1, 0), g * tiles_per_group + c))
    assert all(w <= ATT_TILE for w, _ in DILATED_GROUPS), "the key halo must fit in the previous tile"
    ngroups = len(DILATED_GROUPS)
    return pl.pallas_call(
        _attn_prompt_kernel,
        grid=(b, s // ATT_TILE, tiles_per_group, ngroups),
        in_specs=[cur, cur, prev, cur, prev],
        out_specs=pl.BlockSpec((None, ATT_TILE, LANES), lambda bi, t, c, g: (bi, t, c)),
        out_shape=jax.ShapeDtypeStruct((b, s, GROUP_COLS), BF16),
        scratch_shapes=[
            pltpu.VMEM((ngroups, ATT_TILE, LANES), F32),
            pltpu.VMEM((ngroups, ATT_TILE, LANES), F32),
            pltpu.VMEM((2, Q_BLOCK, HEADS_PER_TILE * 2 * Q_BLOCK), F32),
        ],
        compiler_params=pltpu.CompilerParams(
            dimension_semantics=("parallel", "parallel", "parallel", "arbitrary"),
            vmem_limit_bytes=VMEM_LIMIT),
        name="attn_prompt",
    )(q, k, k, v, v)


def _gla_epilogue(o, r, g):
    o = o * lax.rsqrt(jnp.mean(o * o, -1, keepdims=True) + RMS_EPS)
    return o * g * _silu(r.astype(F32))


SIDE_CHUNK = 512


def _pick_column(ref, n):
    lane = lax.broadcasted_iota(jnp.int32, ref.shape, 1)
    return jnp.sum(jnp.where(lane == n, ref[...], 0.0), axis=1, keepdims=True)


def _attn_sample_step(n, qt_ref, kt_ref, vt_ref, caches, news, o_cols):
    ngroups = len(DILATED_GROUPS)
    qcol = _pick_column(qt_ref, n)
    kcol = _pick_column(kt_ref, n)
    vcol = _pick_column(vt_ref, n)

    def chunks(lb):
        width = min(lb, SIDE_CHUNK)
        return [slice(c * width, (c + 1) * width) for c in range(lb // width)]

    def shift_rows(gi, rows, new):
        lb = caches[gi].shape[1]
        parts = chunks(lb)
        for c, sl in enumerate(parts):
            x = caches[gi][rows, sl]
            width = x.shape[1]
            if c + 1 < len(parts):
                nxt = caches[gi][rows, parts[c + 1].start:parts[c + 1].start + LANES][:, 0:1]
            else:
                nxt = new
            lane = lax.broadcasted_iota(jnp.int32, x.shape, 1)
            news[gi][rows, sl] = jnp.where(lane == width - 1, nxt, pltpu.roll(x, width - 1, 1))

    heads = []
    for h in range(HEADS_PER_GROUP):
        s_old, s_new = [], []
        for gi, (window, dil) in enumerate(DILATED_GROUPS):
            lb = caches[gi].shape[1]
            r0 = gi * GROUP_COLS + h * HEAD_DIM
            q1 = qcol[r0:r0 + HEAD_DIM]
            k1 = kcol[r0:r0 + HEAD_DIM]
            krows = slice(h * HEAD_DIM, (h + 1) * HEAD_DIM)
            shift_rows(gi, krows, k1)
            s = jnp.concatenate([jnp.sum(caches[gi][krows, sl] * q1, axis=0, keepdims=True)
                                 for sl in chunks(lb)], axis=1)
            pos = lax.broadcasted_iota(jnp.int32, (1, lb), 1)
            s_old.append(jnp.where(pos % dil == 0, s, -jnp.inf))
            s_new.append(jnp.sum(k1 * q1, axis=0, keepdims=True))
        m = s_new[0]
        for gi in range(ngroups):
            m = jnp.maximum(m, jnp.maximum(s_new[gi], jnp.max(s_old[gi], axis=1, keepdims=True)))
        den = jnp.zeros((1, 1), F32)
        acc = jnp.zeros((HEAD_DIM, 1), F32)
        for gi in range(ngroups):
            r0 = gi * GROUP_COLS + h * HEAD_DIM
            v1 = vcol[r0:r0 + HEAD_DIM]
            p_old = jnp.exp(s_old[gi] - m)
            p_new = jnp.exp(s_new[gi] - m)
            den = den + jnp.sum(p_old, axis=1, keepdims=True) + p_new
            rows = slice(GROUP_COLS + h * HEAD_DIM, GROUP_COLS + (h + 1) * HEAD_DIM)
            shift_rows(gi, rows, v1)
            for sl in chunks(caches[gi].shape[1]):
                acc = acc + jnp.sum(caches[gi][rows, sl] * p_old[:, sl], axis=1, keepdims=True)
            acc = acc + v1 * p_new
        heads.append(acc / den)
    o_col = jnp.concatenate(heads, axis=0)
    lane = lax.broadcasted_iota(jnp.int32, o_cols.shape, 1)
    o_cols[...] = jnp.where(lane == n, o_col, o_cols[...])


def _post_kernel(x_ref, oa_ref, ob_ref, gates_ref, pe_ref,
                 wa_ref, wb_ref, wo_ref, wfg_ref, wfu_ref, wfd_ref, wpg_ref, wpp_ref,
                 ln1g_ref, ln1b_ref, ln2g_ref, ln2b_ref,
                 qt_ref, kt_ref, vt_ref, c1_ref, c2_ref, c3_ref, xs_ref, obs_ref, gs_ref, pes_ref,
                 y_ref, n1_ref, n2_ref, n3_ref, ys_ref, o_cols, h_carry):
    i = pl.program_id(0)
    last = pl.num_programs(0) - 1

    @pl.when(i == 0)
    def _():
        o_cols[...] = jnp.zeros_like(o_cols)
        h_carry[...] = jnp.zeros_like(h_carry)

    request = jnp.minimum(i, qt_ref.shape[1] - 1)
    _attn_sample_step(request, qt_ref, kt_ref, vt_ref, (c1_ref, c2_ref, c3_ref),
                      (n1_ref, n2_ref, n3_ref), o_cols)

    def merge_and_project(x, oa, ob, gates):
        ga = gates[:, 0:D_MODEL].astype(F32)
        gb = gates[:, D_MODEL:2 * D_MODEL].astype(F32)
        merged = (ga * _dot(oa.astype(BF16), wa_ref[...]) + gb * _dot(ob.astype(BF16), wb_ref[...]))
        return DN_ALPHA * x + _dot(merged.astype(BF16), wo_ref[...])

    def norm_ffn_ple(h, pe):
        x1 = _layernorm(h, ln1g_ref[...], ln1b_ref[...])
        x1b = x1.astype(BF16)
        act = _silu(_dot(x1b, wfg_ref[...])) * _dot(x1b, wfu_ref[...])
        x2 = _layernorm(DN_ALPHA * x1 + _dot(act.astype(BF16), wfd_ref[...]), ln2g_ref[...], ln2b_ref[...])
        gate = _sigmoid(_dot(x2.astype(BF16), wpg_ref[...]))
        return x2 + gate * _dot(pe.astype(BF16), wpp_ref[...])

    h_new = merge_and_project(x_ref[...], oa_ref[...], ob_ref[...], gates_ref[...])
    y_ref[...] = norm_ffn_ple(h_carry[...], pe_ref[...])
    h_carry[...] = h_new

    @pl.when(i == last)
    def _():
        oa_s = jnp.concatenate([o_cols[c * LANES:(c + 1) * LANES, :].T for c in range(GROUP_COLS // LANES)],
                               axis=1)
        h_s = merge_and_project(xs_ref[...], oa_s, obs_ref[...], gs_ref[...])
        ys_ref[...] = norm_ffn_ple(h_s, pes_ref[...])


def _post(x, oa, ob, gates, pe, weights, *, sample_attn, sample_rows):
    rows = x.shape[0]
    qt, kt, vt, caches = sample_attn
    n = qt.shape[1]
    assert n == LANES and rows % n == 0, "one request per row tile"
    tm = rows // n
    tiles = n
    for (window, _), c in zip(DILATED_GROUPS, caches):
        assert c.shape[2] == window, "every key distance must fall inside the cache"
    stage1 = lambda w: pl.BlockSpec((tm, w), lambda i: (jnp.minimum(i, tiles - 1), 0))
    stage2 = lambda w: pl.BlockSpec((tm, w), lambda i: (jnp.maximum(i - 1, 0), 0))
    cache_specs = [pl.BlockSpec((None,) + c.shape[1:], lambda i: (jnp.minimum(i, n - 1), 0, 0)) for c in caches]
    in_specs = ([stage1(D_MODEL), stage1(GROUP_COLS), stage1(GLA_V), stage1(2 * D_MODEL), stage2(PLE_DIM)]
                + [_const_spec(w.shape) for w in weights]
                + [_const_spec(qt.shape)] * 3 + cache_specs
                + [_const_spec(a.shape) for a in sample_rows])
    out_specs = [stage2(D_MODEL)] + cache_specs + [pl.BlockSpec((n, D_MODEL), lambda i: (0, 0))]
    out_shape = ([jax.ShapeDtypeStruct((rows, D_MODEL), F32)]
                 + [jax.ShapeDtypeStruct(c.shape, c.dtype) for c in caches]
                 + [jax.ShapeDtypeStruct((n, D_MODEL), F32)])
    return pl.pallas_call(
        _post_kernel,
        grid=(tiles + 1,),
        in_specs=in_specs,
        out_specs=tuple(out_specs),
        out_shape=tuple(out_shape),
        scratch_shapes=[pltpu.VMEM((GROUP_COLS, n), F32), pltpu.VMEM((tm, D_MODEL), F32)],
        compiler_params=pltpu.CompilerParams(
            dimension_semantics=("arbitrary",), vmem_limit_bytes=VMEM_LIMIT),
        name="post",
    )(x, oa, ob, gates, pe, *weights, qt, kt, vt, *caches, *sample_rows)


PROMPT_TM = 256


def _layer_weights(w_in, w_gate_up, b_gate, w_a_out, w_b_out, w_o, ln1_g, ln1_b,
                   w_ff_gate, w_ff_up, w_ff_down, ln2_g, ln2_b, w_ple_gate, w_ple_proj):
    proj = (w_in.T.astype(BF16), w_gate_up.astype(BF16), b_gate.reshape(1, GLA_K))
    row = lambda p: p.reshape(1, -1)
    post = (w_a_out.astype(BF16), w_b_out.astype(BF16), w_o.astype(BF16), w_ff_gate.astype(BF16),
            w_ff_up.astype(BF16), w_ff_down.astype(BF16), w_ple_gate.astype(BF16),
            w_ple_proj.astype(BF16), row(ln1_g), row(ln1_b), row(ln2_g), row(ln2_b))
    return proj, post


def _positions_minor(cache):
    n, lb = cache.shape[:2]
    return jnp.transpose(cache, (0, 2, 3, 4, 1)).reshape(n, 2 * GROUP_COLS, lb)


def _positions_major(cache):
    n, lb = cache.shape[0], cache.shape[-1]
    return jnp.transpose(cache.reshape(n, 2, HEADS_PER_GROUP, HEAD_DIM, lb), (0, 4, 1, 2, 3))


def kernel(x_prompt, x_sample, cache_a1_kv, cache_a2_kv, cache_a3_kv, state_gla, p_prompt, p_sample, w_in, w_gate_up, b_gate, gla_norm_g, w_a_out, w_b_out, w_o, ln1_g, ln1_b, w_ff_gate, w_ff_up, w_ff_down, ln2_g, ln2_b, w_ple_gate, w_ple_proj):
    depth = w_in.shape[0]
    assert depth == 1 and x_sample.shape[1] == 1
    b, s, d = x_prompt.shape
    n = x_sample.shape[0]
    proj_w, post_w = _layer_weights(
        w_in[0], w_gate_up[0], b_gate[0], w_a_out[0], w_b_out[0], w_o[0], ln1_g[0], ln1_b[0],
        w_ff_gate[0], w_ff_up[0], w_ff_down[0], ln2_g[0], ln2_b[0], w_ple_gate[0], w_ple_proj[0])
    norm_g = gla_norm_g[0].reshape(1, GLA_V)

    xs = x_sample.reshape(n, d)
    rope_s = _rope_table(jnp.full((n,), PAST_LEN, jnp.int32))
    qt, kt, vt, qgt, kgt, vg_s, rg_s, lat, gates_s = _in_proj(
        xs, rope_s, proj_w, tm=n, rope_period_tiles=1, sample=True)

    xp = x_prompt.reshape(b * s, d)
    rope_p = _rope_table(jnp.arange(s))
    q, k, v, gates, ob, st_p, ob_s, st_s, *kv_tails = _in_proj(
        xp, rope_p, proj_w, tm=PROMPT_TM, rope_period_tiles=s // PROMPT_TM, sample=False,
        norm_g=norm_g, gla_sample=(qgt, kgt, lat, vg_s, rg_s, state_gla[0]))
    three = lambda t: t.reshape(b, s, t.shape[-1])
    oa = _attn_prompt(three(q), three(k), three(v))
    kv_p = [_positions_major(t)[None] for t in kv_tails]

    caches = [_positions_minor(c[0]) for c in (cache_a1_kv, cache_a2_kv, cache_a3_kv)]
    y_p, *new_caches, y_s = _post(
        xp, oa.reshape(b * s, GROUP_COLS), ob, gates, p_prompt[0].reshape(b * s, PLE_DIM), post_w,
        sample_attn=(qt, kt, vt, caches),
        sample_rows=(xs, ob_s, gates_s, p_sample[0].reshape(n, PLE_DIM)))
    kv_s = [_positions_major(c)[None] for c in new_caches]

    return (y_p.reshape(b, s, d), y_s.reshape(n, 1, d), kv_p[0], kv_p[1], kv_p[2], st_p[None],
            kv_s[0], kv_s[1], kv_s[2], st_s[None])
```
